```python
import math
import jax, jax.numpy as jnp
from jax import lax
import numpy as np

D_MODEL = 2048
BATCH = 4
SEQ = 4096
DEPTH = 2

GRID_W = 64
CTX_LEN = 256
HEAD_DIM = 128
N_GROUP_HEADS = D_MODEL // (2 * HEAD_DIM)
MIX_WIDTH = 2 * N_GROUP_HEADS * HEAD_DIM
Q_BLOCK = 128
ROPE_BASE = 10000.0
EPS = 1e-6
NEG_INF = -1e30
A_HEADS = N_GROUP_HEADS
A_SUB = HEAD_DIM // 2
A_SCALE = A_SUB ** -0.5
LAMBDA_INIT_L0 = 0.8 - 0.6 * math.exp(-0.3 * 0)
B_HEADS = N_GROUP_HEADS
B_KV = B_HEADS // 4
B_GROUP = B_HEADS // B_KV
C_HEADS = N_GROUP_HEADS
C_KV = C_HEADS // 4
C_GROUP = C_HEADS // C_KV
C_WINDOW = 128
D_HEADS = N_GROUP_HEADS
NA_ROWS = 8
NA_COLS = 16
NA_QCOLS = 16
NA_KCOLS = NA_COLS + NA_QCOLS
ATT_SCALE = HEAD_DIM ** -0.5
D_FF = 5632
N_EXPERTS = 8
TOP_K = 2
D_EXPERT = 7168
MOE_BLOCK = 128

L0_SIZES = (A_HEADS * HEAD_DIM, B_HEADS * HEAD_DIM, A_HEADS * HEAD_DIM, A_HEADS * HEAD_DIM, B_KV * HEAD_DIM, B_KV * HEAD_DIM)
L1_SIZES = (C_HEADS * HEAD_DIM, D_HEADS * HEAD_DIM, C_KV * HEAD_DIM, C_KV * HEAD_DIM, D_HEADS * HEAD_DIM, D_HEADS * HEAD_DIM)
L0_Q = L0_SIZES[0] + L0_SIZES[1]
L1_Q = L1_SIZES[0] + L1_SIZES[1]
PROJ_COLS = sum(L0_SIZES)

kernel_name = 'hybrid_diffusion_prefix_block'


def _split(t, sizes):
    return jnp.split(t, [int(i) for i in np.cumsum(sizes)[:-1]], axis=-1)


def rmsnorm(x, g):
    xf = x.astype(jnp.float32)
    y = xf * lax.rsqrt(jnp.mean(xf * xf, axis=-1, keepdims=True) + EPS)
    return (y * g.astype(jnp.float32)).astype(x.dtype)


def modulate(x, g, shift, scale):
    return rmsnorm(x, g) * (1.0 + scale) + shift


def ada_mod(c_vec, w_ada, b_ada, n_chunks):
    m = jax.nn.silu(c_vec) @ w_ada[:, :n_chunks * D_MODEL] + b_ada[:n_chunks * D_MODEL]
    return jnp.split(m, n_chunks, axis=-1)


def axial_rope(n_tok, dim):
    t = jnp.arange(n_tok)
    pos = jnp.stack([t // GRID_W, t % GRID_W], axis=-1).astype(jnp.float32)
    n_pairs = dim // 4
    freq = ROPE_BASE ** (-jnp.arange(n_pairs, dtype=jnp.float32) / n_pairs)
    ang = (pos[:, :, None] * freq).reshape(n_tok, 2 * n_pairs)
    return jnp.cos(ang), jnp.sin(ang)


def apply_rope(x, cos, sin):
    xf = x.astype(jnp.float32).reshape(x.shape[:-1] + (-1, 2))
    x0, x1 = xf[..., 0], xf[..., 1]
    shape = (1, cos.shape[0]) + (1,) * (x.ndim - 3) + (cos.shape[1],)
    c, s = cos.reshape(shape), sin.reshape(shape)
    out = jnp.stack([x0 * c - x1 * s, x0 * s + x1 * c], axis=-1)
    return out.reshape(x.shape).astype(x.dtype)


def gqa_attend(q, k, v, sink=None):
    s = jnp.einsum('bqhgd,bkhd->bhgqk', q, k).astype(jnp.float32) * ATT_SCALE
    if sink is None:
        p = jax.nn.softmax(s, axis=-1)
    else:
        sk = jnp.broadcast_to(sink.astype(jnp.float32).reshape(1, k.shape[2], -1, 1, 1), s.shape[:-1] + (1,))
        p = jax.nn.softmax(jnp.concatenate([s, sk], axis=-1), axis=-1)[..., :-1]
    o = jnp.einsum('bhgqk,bkhd->bqhgd', p.astype(v.dtype), v)
    return o.reshape(q.shape[:2] + (-1,))


def diff_attend(q, k, v, lam, subln_g):
    s = jnp.einsum('bqhcd,bkhcd->bhcqk', q, k).astype(jnp.float32) * A_SCALE
    p = jax.nn.softmax(s, axis=-1)
    a = p[:, :, 0] - lam * p[:, :, 1]
    o = jnp.einsum('bhqk,bkhd->bqhd', a.astype(v.dtype), v)
    o = rmsnorm(o, subln_g) * (1.0 - LAMBDA_INIT_L0)
    return o.reshape(q.shape[:2] + (-1,))


def sweep_query_blocks(fn, *qs):
    b, s = qs[0].shape[:2]
    nb = s // Q_BLOCK
    blocks = tuple(jnp.moveaxis(q.reshape((b, nb, Q_BLOCK) + q.shape[2:]), 1, 0) for q in qs)
    out = lax.map(lambda a: fn(*a), blocks)
    return jnp.moveaxis(out, 0, 1).reshape((b, s) + out.shape[3:])


def window_attend(q, k, v, kc, vc, sinks):
    b, s, hkv, g, d = q.shape
    nb = s // Q_BLOCK
    ns = C_WINDOW // Q_BLOCK
    band = (2 * ns + 1) * Q_BLOCK
    pad = ((0, 0), (C_WINDOW, C_WINDOW), (0, 0), (0, 0))
    kb = jnp.pad(k, pad).reshape(b, nb + 2 * ns, Q_BLOCK, hkv, d)
    vb = jnp.pad(v, pad).reshape(b, nb + 2 * ns, Q_BLOCK, hkv, d)
    k_band = jnp.concatenate([kb[:, j:j + nb] for j in range(2 * ns + 1)], axis=2)
    v_band = jnp.concatenate([vb[:, j:j + nb] for j in range(2 * ns + 1)], axis=2)
    qb = q.reshape(b, nb, Q_BLOCK, hkv, g, d)
    qpos = jnp.arange(s).reshape(nb, Q_BLOCK)
    kpos = jnp.arange(nb)[:, None] * Q_BLOCK - C_WINDOW + jnp.arange(band)[None, :]
    ok = (jnp.abs(kpos[:, None, :] - qpos[:, :, None]) <= C_WINDOW) & (kpos[:, None, :] >= 0) & (kpos[:, None, :] < s)
    s_loc = jnp.einsum('bnqhgd,bnkhd->bhgnqk', qb, k_band).astype(jnp.float32) * ATT_SCALE
    s_loc = jnp.where(ok, s_loc, NEG_INF)
    s_ctx = jnp.einsum('bnqhgd,bkhd->bhgnqk', qb, kc).astype(jnp.float32) * ATT_SCALE
    sk = jnp.broadcast_to(sinks.astype(jnp.float32).reshape(1, hkv, g, 1, 1, 1), s_loc.shape[:-1] + (1,))
    p = jax.nn.softmax(jnp.concatenate([s_loc, s_ctx, sk], axis=-1), axis=-1).astype(v.dtype)
    o = (jnp.einsum('bhgnqk,bnkhd->bnqhgd', p[..., :band], v_band)
         + jnp.einsum('bhgnqk,bkhd->bnqhgd', p[..., band:-1], vc))
    return o.reshape(b, s, hkv * g * d)


def neighbourhood_attend(q, k, v, kc, vc, rpb):
    b, s, h, d = q.shape
    rows = s // GRID_W
    wr = min(NA_ROWS, rows)
    n_cb = GRID_W // NA_QCOLS
    cb = np.clip(np.arange(n_cb) * NA_QCOLS - NA_COLS // 2, 0, GRID_W - NA_KCOLS)
    kcol = cb[:, None] + np.arange(NA_KCOLS)[None, :]
    qcol = np.arange(GRID_W).reshape(n_cb, NA_QCOLS)
    cstart = np.clip(qcol - NA_COLS // 2, 0, GRID_W - NA_COLS)
    col_ok = (kcol[:, None, :] >= cstart[:, :, None]) & (kcol[:, None, :] < cstart[:, :, None] + NA_COLS)
    dcol = np.clip(kcol[:, None, :] - qcol[:, :, None] + NA_COLS - 1, 0, 2 * NA_COLS - 2)
    kg = k.reshape(b, rows, GRID_W, h, d)
    vg = v.reshape(b, rows, GRID_W, h, d)
    qg = jnp.moveaxis(q.reshape(b, rows, GRID_W, h, d), 1, 0)

    def row_block(args):
        r, q_row = args
        rs = jnp.clip(r - wr // 2, 0, rows - wr)
        k_nb = lax.dynamic_slice_in_dim(kg, rs, wr, axis=1)[:, :, kcol]
        v_nb = lax.dynamic_slice_in_dim(vg, rs, wr, axis=1)[:, :, kcol]
        qb = q_row.reshape(b, n_cb, NA_QCOLS, h, d)
        drow = rs + jnp.arange(wr) - r + NA_ROWS - 1
        bias = rpb[:, drow[None, None, :, None], dcol[:, :, None, :]].astype(jnp.float32)
        s_nb = jnp.einsum('bcqhd,bwckhd->bhcqwk', qb, k_nb).astype(jnp.float32) * ATT_SCALE + bias
        s_nb = jnp.where(col_ok[:, :, None, :], s_nb, NEG_INF).reshape(b, h, n_cb, NA_QCOLS, wr * NA_KCOLS)
        s_ctx = jnp.einsum('bcqhd,bkhd->bhcqk', qb, kc).astype(jnp.float32) * ATT_SCALE
        p = jax.nn.softmax(jnp.concatenate([s_nb, s_ctx], axis=-1), axis=-1).astype(v.dtype)
        p_nb = p[..., :wr * NA_KCOLS].reshape(b, h, n_cb, NA_QCOLS, wr, NA_KCOLS)
        o = (jnp.einsum('bhcqwk,bwckhd->bcqhd', p_nb, v_nb)
             + jnp.einsum('bhcqk,bkhd->bcqhd', p[..., wr * NA_KCOLS:], vc))
        return o.reshape(b, GRID_W, h * d)

    o = lax.map(row_block, (jnp.arange(rows), qg))
    return jnp.moveaxis(o, 0, 1).reshape(b, s, h * d)


def mixer_diff_gqa(h, hc, w_in, lam_q1, lam_k1, lam_q2, lam_k2, subln_g, q_norm_g, k_norm_g, rope, ctx_out):
    cos_a, sin_a, cos_b, sin_b = rope
    b, s, _ = h.shape
    n_ctx = hc.shape[1]
    aq, bq, ak, av, bk, bv = _split(h @ w_in, L0_SIZES)
    aq = apply_rope(aq.reshape(b, s, A_HEADS, 2, A_SUB), cos_a, sin_a)
    ak = apply_rope(ak.reshape(b, s, A_HEADS, 2, A_SUB), cos_a, sin_a)
    bq = apply_rope(rmsnorm(bq.reshape(b, s, B_KV, B_GROUP, HEAD_DIM), q_norm_g), cos_b, sin_b)
    bk = apply_rope(rmsnorm(bk.reshape(b, s, B_KV, HEAD_DIM), k_norm_g), cos_b, sin_b)
    if ctx_out:
        aqc, bqc, akc, avc, bkc, bvc = _split(hc @ w_in, L0_SIZES)
    else:
        akc, avc, bkc, bvc = _split(hc @ w_in[:, L0_Q:], L0_SIZES[2:])
    akc = akc.reshape(b, n_ctx, A_HEADS, 2, A_SUB)
    avc = avc.reshape(b, n_ctx, A_HEADS, HEAD_DIM)
    bkc = rmsnorm(bkc.reshape(b, n_ctx, B_KV, HEAD_DIM), k_norm_g)
    bvc = bvc.reshape(b, n_ctx, B_KV, HEAD_DIM)
    lam = (jnp.exp(jnp.sum((lam_q1 * lam_k1).astype(jnp.float32)))
           - jnp.exp(jnp.sum((lam_q2 * lam_k2).astype(jnp.float32))) + LAMBDA_INIT_L0)
    ak_all = jnp.concatenate([ak, akc], axis=1)
    av_all = jnp.concatenate([av.reshape(b, s, A_HEADS, HEAD_DIM), avc], axis=1)
    bk_all = jnp.concatenate([bk, bkc], axis=1)
    bv_all = jnp.concatenate([bv.reshape(b, s, B_KV, HEAD_DIM), bvc], axis=1)

    def block(aq_blk, bq_blk):
        return jnp.concatenate([diff_attend(aq_blk, ak_all, av_all, lam, subln_g),
                                gqa_attend(bq_blk, bk_all, bv_all)], axis=-1)

    o_lat = sweep_query_blocks(block, aq, bq)
    if not ctx_out:
        return o_lat, None
    aqc = aqc.reshape(b, n_ctx, A_HEADS, 2, A_SUB)
    bqc = rmsnorm(bqc.reshape(b, n_ctx, B_KV, B_GROUP, HEAD_DIM), q_norm_g)
    o_ctx = jnp.concatenate([diff_attend(aqc, akc, avc, lam, subln_g), gqa_attend(bqc, bkc, bvc)], axis=-1)
    return o_lat, o_ctx


def mixer_window_na(h, hc, w_in, sinks, rpb, rope, ctx_out):
    _, _, cos_b, sin_b = rope
    b, s, _ = h.shape
    n_ctx = hc.shape[1]
    cq, dq, ck, cv, dk, dv = _split(h @ w_in, L1_SIZES)
    cq = apply_rope(cq.reshape(b, s, C_KV, C_GROUP, HEAD_DIM), cos_b, sin_b)
    ck = apply_rope(ck.reshape(b, s, C_KV, HEAD_DIM), cos_b, sin_b)
    cv = cv.reshape(b, s, C_KV, HEAD_DIM)
    dq = dq.reshape(b, s, D_HEADS, HEAD_DIM)
    dk = dk.reshape(b, s, D_HEADS, HEAD_DIM)
    dv = dv.reshape(b, s, D_HEADS, HEAD_DIM)
    if ctx_out:
        cqc, dqc, ckc, cvc, dkc, dvc = _split(hc @ w_in, L1_SIZES)
    else:
        ckc, cvc, dkc, dvc = _split(hc @ w_in[:, L1_Q:], L1_SIZES[2:])
    ckc = ckc.reshape(b, n_ctx, C_KV, HEAD_DIM)
    cvc = cvc.reshape(b, n_ctx, C_KV, HEAD_DIM)
    dkc = dkc.reshape(b, n_ctx, D_HEADS, HEAD_DIM)
    dvc = dvc.reshape(b, n_ctx, D_HEADS, HEAD_DIM)
    o_lat = jnp.concatenate([window_attend(cq, ck, cv, ckc, cvc, sinks),
                             neighbourhood_attend(dq, dk, dv, dkc, dvc, rpb)], axis=-1)
    if not ctx_out:
        return o_lat, None
    o_ctx = jnp.concatenate([gqa_attend(cqc.reshape(b, n_ctx, C_KV, C_GROUP, HEAD_DIM), ckc, cvc, sinks),
                             gqa_attend(dqc.reshape(b, n_ctx, D_HEADS, 1, HEAD_DIM), dkc, dvc)], axis=-1)
    return o_lat, o_ctx


def dense_swiglu(h, w_gate, w_up, w_down):
    return (jax.nn.silu(h @ w_gate) * (h @ w_up)) @ w_down


def moe_swiglu(h, w_router, w_gate_e, w_up_e, w_down_e):
    b, s, d = h.shape
    xt = h.reshape(-1, d)
    n = xt.shape[0]
    logits = (xt @ w_router).astype(jnp.float32)
    top_logit, top_idx = lax.top_k(logits, TOP_K)
    gates = jax.nn.softmax(top_logit, axis=-1)
    e_flat = top_idx.reshape(-1)
    tok_flat = jnp.repeat(jnp.arange(n, dtype=jnp.int32), TOP_K)
    order = jnp.argsort(e_flat)
    e_sorted = e_flat[order]
    tok_sorted = tok_flat[order]
    g_sorted = gates.reshape(-1)[order]
    counts = jnp.bincount(e_flat, length=N_EXPERTS)
    padded = (counts + MOE_BLOCK - 1) // MOE_BLOCK * MOE_BLOCK
    pad_end = jnp.cumsum(padded)
    pad_start = pad_end - padded
    sort_start = jnp.cumsum(counts) - counts
    slot = pad_start[e_sorted] + (jnp.arange(n * TOP_K) - sort_start[e_sorted])
    n_blocks = (n * TOP_K + N_EXPERTS * (MOE_BLOCK - 1) + MOE_BLOCK - 1) // MOE_BLOCK
    cap = n_blocks * MOE_BLOCK
    slot_tok = jnp.full((cap,), n, dtype=jnp.int32).at[slot].set(tok_sorted)
    x_pad = jnp.concatenate([xt, jnp.zeros((1, d), xt.dtype)], axis=0)
    block_expert = jnp.minimum(jnp.searchsorted(pad_end, jnp.arange(n_blocks) * MOE_BLOCK, side='right'), N_EXPERTS - 1)

    def run_block(args):
        e, idx = args
        xb = x_pad[idx]
        return (jax.nn.silu(xb @ w_gate_e[e]) * (xb @ w_up_e[e])) @ w_down_e[e]

    y_pad = lax.map(run_block, (block_expert, slot_tok.reshape(n_blocks, MOE_BLOCK))).reshape(cap, d)
    y_sorted = y_pad[slot] * g_sorted[:, None].astype(y_pad.dtype)
    y = jax.ops.segment_sum(y_sorted, tok_sorted, num_segments=n)
    return y.reshape(b, s, d)


def hybrid_layer(x, xc, c, c_ctx, base, mixer_fn, mix_prm, ffn_fn, ffn_prm, rope, ctx_out):
    norm1_g, norm2_g, w_ada, b_ada, w_in, w_out = base
    sh1, sc1, gt1, sh2, sc2, gt2 = [m[:, None] for m in ada_mod(c, w_ada, b_ada, 6)]
    cm = ada_mod(c_ctx, w_ada, b_ada, 6 if ctx_out else 2)
    o_lat, o_ctx = mixer_fn(modulate(x, norm1_g, sh1, sc1), modulate(xc, norm1_g, cm[0], cm[1]), w_in,
                            *mix_prm, rope=rope, ctx_out=ctx_out)
    x = x + gt1 * (o_lat @ w_out)
    x = x + gt2 * ffn_fn(modulate(x, norm2_g, sh2, sc2), *ffn_prm)
    if not ctx_out:
        return x, None
    xc = xc + cm[2] * (o_ctx @ w_out)
    xc = xc + cm[5] * ffn_fn(modulate(xc, norm2_g, cm[3], cm[4]), *ffn_prm)
    return x, xc


def setup_inputs(seed: int = 0) -> dict:
    key = jax.random.key(seed)
    ks = iter(jax.random.split(key, 40))
    d = D_MODEL

    def nrm(shape, scale):
        return jax.random.normal(next(ks), shape, jnp.float32) * scale

    def gain(n):
        return 1.0 + nrm((n,), 0.1)

    inp = {}
    inp['x'] = nrm((BATCH, SEQ, d), 1.0)
    inp['c'] = nrm((BATCH, d), 1.0)
    inp['ctx'] = nrm((BATCH, CTX_LEN, d), 1.0)
    inp['c_ctx'] = nrm((d,), 1.0)
    inp['l0_norm1_g'] = gain(d)
    inp['l0_norm2_g'] = gain(d)
    inp['l0_w_ada'] = nrm((d, 6 * d), 0.5 * d ** -0.5)
    inp['l0_b_ada'] = nrm((6 * d,), 0.01)
    inp['l0_w_in'] = nrm((d, PROJ_COLS), d ** -0.5)
    inp['l0_w_out'] = nrm((MIX_WIDTH, d), MIX_WIDTH ** -0.5)
    inp['l0_lam_q1'] = nrm((A_SUB,), 0.1)
    inp['l0_lam_k1'] = nrm((A_SUB,), 0.1)
    inp['l0_lam_q2'] = nrm((A_SUB,), 0.1)
    inp['l0_lam_k2'] = nrm((A_SUB,), 0.1)
    inp['l0_subln_g'] = gain(HEAD_DIM)
    inp['l0_q_norm_g'] = gain(HEAD_DIM)
    inp['l0_k_norm_g'] = gain(HEAD_DIM)
    inp['l0_ffn_w_gate'] = nrm((d, D_FF), d ** -0.5)
    inp['l0_ffn_w_up'] = nrm((d, D_FF), d ** -0.5)
    inp['l0_ffn_w_down'] = nrm((D_FF, d), D_FF ** -0.5)
    inp['l1_norm1_g'] = gain(d)
    inp['l1_norm2_g'] = gain(d)
    inp['l1_w_ada'] = nrm((d, 6 * d), 0.5 * d ** -0.5)
    inp['l1_b_ada'] = nrm((6 * d,), 0.01)
    inp['l1_w_in'] = nrm((d, PROJ_COLS), d ** -0.5)
    inp['l1_w_out'] = nrm((MIX_WIDTH, d), MIX_WIDTH ** -0.5)
    inp['l1_sinks'] = nrm((C_HEADS,), 0.5)
    inp['l1_rpb'] = nrm((D_HEADS, 2 * NA_ROWS - 1, 2 * NA_COLS - 1), 0.1)
    inp['l1_router'] = nrm((d, N_EXPERTS), d ** -0.5)
    inp['l1_exp_w_gate'] = nrm((N_EXPERTS, d, D_EXPERT), d ** -0.5)
    inp['l1_exp_w_up'] = nrm((N_EXPERTS, d, D_EXPERT), d ** -0.5)
    inp['l1_exp_w_down'] = nrm((N_EXPERTS, D_EXPERT, d), D_EXPERT ** -0.5)
    inp['final_norm_g'] = gain(d)
    return inp


def reference(x, c, ctx, c_ctx,
              l0_norm1_g, l0_norm2_g, l0_w_ada, l0_b_ada, l0_w_in, l0_w_out,
              l0_lam_q1, l0_lam_k1, l0_lam_q2, l0_lam_k2, l0_subln_g, l0_q_norm_g, l0_k_norm_g,
              l0_ffn_w_gate, l0_ffn_w_up, l0_ffn_w_down,
              l1_norm1_g, l1_norm2_g, l1_w_ada, l1_b_ada, l1_w_in, l1_w_out,
              l1_sinks, l1_rpb, l1_router, l1_exp_w_gate, l1_exp_w_up, l1_exp_w_down,
              final_norm_g):
    s = x.shape[1]
    rope = axial_rope(s, A_SUB) + axial_rope(s, HEAD_DIM)
    layers = [
        ((l0_norm1_g, l0_norm2_g, l0_w_ada, l0_b_ada, l0_w_in, l0_w_out), mixer_diff_gqa,
         (l0_lam_q1, l0_lam_k1, l0_lam_q2, l0_lam_k2, l0_subln_g, l0_q_norm_g, l0_k_norm_g),
         dense_swiglu, (l0_ffn_w_gate, l0_ffn_w_up, l0_ffn_w_down)),
        ((l1_norm1_g, l1_norm2_g, l1_w_ada, l1_b_ada, l1_w_in, l1_w_out), mixer_window_na,
         (l1_sinks, l1_rpb),
         moe_swiglu, (l1_router, l1_exp_w_gate, l1_exp_w_up, l1_exp_w_down)),
    ]
    xc = ctx
    for i in range(DEPTH):
        base, mixer_fn, mix_prm, ffn_fn, ffn_prm = layers[i]
        x, xc = hybrid_layer(x, xc, c, c_ctx, base, mixer_fn, mix_prm, ffn_fn, ffn_prm, rope,
                             ctx_out=(i < DEPTH - 1))
    return rmsnorm(x, final_norm_g)
```

```python
import functools
import math

import numpy as np
import jax
import jax.numpy as jnp
from jax import lax
from jax.experimental import pallas as pl
from jax.experimental.pallas import tpu as pltpu

F32 = jnp.float32
BF16 = jnp.bfloat16

HEAD_DIM = 128
LANES = 128
GRID_W = 64
ROPE_BASE = 10000.0
EPS = 1e-6
NEG_INF = -1e30
A_SUB = HEAD_DIM // 2
A_SCALE = A_SUB ** -0.5
ATT_SCALE = HEAD_DIM ** -0.5
LAMBDA_INIT_L0 = 0.8 - 0.6 * math.exp(-0.3 * 0)
LOG2E = 1.4426950408889634
C_WINDOW = 128
NA_ROWS = 8
NA_COLS = 16
TOP_K = 2
VMEM_LIMIT = 56 * 1024 * 1024

T_ROPE, T_QNORM, T_KNORM, T_SCALE_B, T_SCALE_A = 1, 2, 4, 8, 16


def _cparams(sem):
    return pltpu.CompilerParams(dimension_semantics=sem, vmem_limit_bytes=VMEM_LIMIT)


def _modulate(xf, g, scale, shift):
    ms = jnp.mean(xf * xf, axis=-1, keepdims=True)
    y = xf * lax.rsqrt(ms + EPS) * g
    return y * (1.0 + scale) + shift


def _ada_kernel(c_ref, w_ref, b_ref, o_ref):
    c = c_ref[...]
    a = c * (1.0 / (1.0 + jnp.exp(-c)))
    o_ref[...] = jnp.dot(a.astype(BF16), w_ref[...].astype(BF16), preferred_element_type=F32) + b_ref[...]


def _ada(cvecs, w_ada, b_ada):
    r, d = cvecs.shape
    n = w_ada.shape[1]
    tn = 1024
    return pl.pallas_call(
        _ada_kernel,
        out_shape=jax.ShapeDtypeStruct((r, n), F32),
        grid=(n // tn,),
        in_specs=[pl.BlockSpec((r, d), lambda j: (0, 0)),
                  pl.BlockSpec((d, tn), lambda j: (0, j)),
                  pl.BlockSpec((1, tn), lambda j: (0, j))],
        out_specs=pl.BlockSpec((r, tn), lambda j: (0, j)),
        compiler_params=_cparams(("parallel",)),
        name="ada",
    )(cvecs, w_ada, b_ada.reshape(1, n))


def _inproj_kernel(types_ref, tab_ref, x_ref, g_ref, sc_ref, sh_ref, w_ref, cos_ref, sin_ref, qkg_ref,
                   o_ref, h_scr, *, heads_per_tile):
    j = pl.program_id(1)

    @pl.when(j == 0)
    def _():
        h_scr[...] = _modulate(x_ref[...], g_ref[...], sc_ref[...], sh_ref[...]).astype(BF16)

    acc = jnp.dot(h_scr[...], w_ref[...], preferred_element_type=F32)
    lane = lax.broadcasted_iota(jnp.int32, (1, LANES), 1)
    even = (lane & 1) == 0
    for hh in range(heads_per_tile):
        t = types_ref[j * heads_per_tile + hh]
        sl = slice(hh * HEAD_DIM, (hh + 1) * HEAD_DIM)
        y = acc[:, sl]

        @pl.when(t == 0)
        def _():
            o_ref[:, sl] = y.astype(o_ref.dtype)

        @pl.when(t != 0)
        def _():
            use_q = (t & T_QNORM) != 0
            use_k = (t & T_KNORM) != 0
            ms = jnp.mean(y * y, axis=-1, keepdims=True)
            gain = jnp.where(use_q, qkg_ref[0:1, :], qkg_ref[1:2, :])
            yn = y * lax.rsqrt(ms + EPS) * gain
            y1 = jnp.where(jnp.logical_or(use_q, use_k), yn, y)
            sw = jnp.where(even, pltpu.roll(y1, LANES - 1, 1), pltpu.roll(y1, 1, 1))
            yr = y1 * cos_ref[...] + sw * sin_ref[...]
            y2 = jnp.where((t & T_ROPE) != 0, yr, y1)
            f = jnp.where((t & T_SCALE_B) != 0, ATT_SCALE * LOG2E,
                          jnp.where((t & T_SCALE_A) != 0, A_SCALE * LOG2E, 1.0)).astype(F32)
            o_ref[:, sl] = (y2 * f).astype(o_ref.dtype)


def _inproj(x2d, rows_per_batch, g, scale, shift, w_bf, cos_tab, sin_tab, qkg, types, tabsel, tm):
    m, d = x2d.shape
    n = w_bf.shape[1]
    tn = 512
    hpt = tn // HEAD_DIM
    tiles_per_batch = rows_per_batch // tm
    tab_rows = cos_tab.shape[1] // tm
    kern = functools.partial(_inproj_kernel, heads_per_tile=hpt)
    grid_spec = pltpu.PrefetchScalarGridSpec(
        num_scalar_prefetch=2,
        grid=(m // tm, n // tn),
        in_specs=[
            pl.BlockSpec((tm, d), lambda i, j, ty, tb: (i, 0)),
            pl.BlockSpec((1, d), lambda i, j, ty, tb: (0, 0)),
            pl.BlockSpec((None, 1, d), lambda i, j, ty, tb: (i // tiles_per_batch, 0, 0)),
            pl.BlockSpec((None, 1, d), lambda i, j, ty, tb: (i // tiles_per_batch, 0, 0)),
            pl.BlockSpec((d, tn), lambda i, j, ty, tb: (0, j)),
            pl.BlockSpec((None, tm, LANES), lambda i, j, ty, tb: (tb[j], i % tab_rows, 0)),
            pl.BlockSpec((None, tm, LANES), lambda i, j, ty, tb: (tb[j], i % tab_rows, 0)),
            pl.BlockSpec((2, HEAD_DIM), lambda i, j, ty, tb: (0, 0)),
        ],
        out_specs=pl.BlockSpec((tm, tn), lambda i, j, ty, tb: (i, j)),
        scratch_shapes=[pltpu.VMEM((tm, d), BF16)],
    )
    return pl.pallas_call(
        kern,
        out_shape=jax.ShapeDtypeStruct((m, n), BF16),
        grid_spec=grid_spec,
        compiler_params=_cparams(("parallel", "arbitrary")),
        name="inproj",
    )(types, tabsel, x2d, g.reshape(1, d), scale, shift, w_bf, cos_tab, sin_tab, qkg)


def _flash_kernel(*refs, mode, tq, has_ctx, nk):
    refs = list(refs)
    q_ref, k_ref, v_ref = refs[:3]
    pos = 3
    if has_ctx:
        kc_ref, vc_ref = refs[3:5]
        pos = 5
    if mode == "diff":
        lam_ref, subg_ref = refs[pos:pos + 2]
        pos += 2
    o_ref, qs, m_s, l_s, acc_s = refs[pos:pos + 5]
    ki = pl.program_id(3)

    @pl.when(ki == 0)
    def _():
        q = q_ref[...]
        if mode == "diff":
            lane = lax.broadcasted_iota(jnp.int32, (1, LANES), 1)
            zero = jnp.zeros_like(q)
            qs[0:tq, :] = jnp.where(lane < A_SUB, q, zero)
            qs[tq:2 * tq, :] = jnp.where(lane >= A_SUB, q, zero)
        else:
            for gi in range(4):
                qs[gi * tq:(gi + 1) * tq, :] = q[:, gi * HEAD_DIM:(gi + 1) * HEAD_DIM]
        m_s[...] = jnp.full_like(m_s, NEG_INF)
        l_s[...] = jnp.zeros_like(l_s)
        acc_s[...] = jnp.zeros_like(acc_s)

    def step(k, v):
        s = lax.dot_general(qs[...], k, (((1,), (1,)), ((), ())), preferred_element_type=F32)
        m_prev = m_s[...]
        m_new = jnp.maximum(m_prev, jnp.max(s, axis=-1, keepdims=True))
        alpha = jnp.exp2(m_prev - m_new)
        p = jnp.exp2(s - m_new)
        l_s[...] = alpha * l_s[...] + jnp.sum(p, axis=-1, keepdims=True)
        acc_s[...] = alpha * acc_s[...] + jnp.dot(p.astype(BF16), v, preferred_element_type=F32)
        m_s[...] = m_new

    step(k_ref[...], v_ref[...])

    @pl.when(ki == nk - 1)
    def _():
        if has_ctx:
            step(kc_ref[...], vc_ref[...])
        o = acc_s[...] / l_s[...]
        if mode == "diff":
            lam1 = jnp.sum(lam_ref[0:1, :] * lam_ref[1:2, :], axis=-1, keepdims=True)
            lam2 = jnp.sum(lam_ref[2:3, :] * lam_ref[3:4, :], axis=-1, keepdims=True)
            lam = jnp.exp(lam1) - jnp.exp(lam2) + LAMBDA_INIT_L0
            od = o[0:tq, :] - lam * o[tq:2 * tq, :]
            ms = jnp.mean(od * od, axis=-1, keepdims=True)
            on = od * lax.rsqrt(ms + EPS) * subg_ref[...]
            o_ref[...] = (on * (1.0 - LAMBDA_INIT_L0)).astype(o_ref.dtype)
        else:
            for gi in range(4):
                o_ref[:, gi * HEAD_DIM:(gi + 1) * HEAD_DIM] = o[gi * tq:(gi + 1) * tq, :].astype(o_ref.dtype)


def _flash(mode, q_arr, kv_arr, ctx_arr, batch, tq, tk, qcol, kcol, vcol, lam=None, subg=None):
    mq = q_arr.shape[0]
    sq = mq // batch
    sk = kv_arr.shape[0] // batch
    nq, nk = sq // tq, sk // tk
    has_ctx = ctx_arr is not None
    if mode == "diff":
        nh, qw, reps = 8, HEAD_DIM, 2
        q_spec = pl.BlockSpec((tq, qw), lambda b, h, qi, ki: (b * nq + qi, qcol + h))
        o_spec = pl.BlockSpec((tq, qw), lambda b, h, qi, ki: (b * nq + qi, h))
    else:
        nh, qw, reps = 2, 4 * HEAD_DIM, 4
        q_spec = pl.BlockSpec((tq, qw), lambda b, h, qi, ki: (b * nq + qi, qcol // 4 + h))
        o_spec = pl.BlockSpec((tq, qw), lambda b, h, qi, ki: (b * nq + qi, h))
    in_specs = [q_spec,
                pl.BlockSpec((tk, HEAD_DIM), lambda b, h, qi, ki: (b * nk + ki, kcol + h)),
                pl.BlockSpec((tk, HEAD_DIM), lambda b, h, qi, ki: (b * nk + ki, vcol + h))]
    args = [q_arr, kv_arr, kv_arr]
    if has_ctx:
        nc = ctx_arr.shape[0] // batch
        in_specs += [pl.BlockSpec((nc, HEAD_DIM), lambda b, h, qi, ki: (b, kcol + h)),
                     pl.BlockSpec((nc, HEAD_DIM), lambda b, h, qi, ki: (b, vcol + h))]
        args += [ctx_arr, ctx_arr]
    if mode == "diff":
        in_specs += [pl.BlockSpec((4, A_SUB), lambda b, h, qi, ki: (0, 0)),
                     pl.BlockSpec((1, HEAD_DIM), lambda b, h, qi, ki: (0, 0))]
        args += [lam, subg]
    r = reps * tq
    kern = functools.partial(_flash_kernel, mode=mode, tq=tq, has_ctx=has_ctx, nk=nk)
    return pl.pallas_call(
        kern,
        out_shape=jax.ShapeDtypeStruct((mq, 8 * HEAD_DIM), BF16),
        grid=(batch, nh, nq, nk),
        in_specs=in_specs,
        out_specs=o_spec,
        scratch_shapes=[pltpu.VMEM((r, HEAD_DIM), BF16), pltpu.VMEM((r, 1), F32),
                        pltpu.VMEM((r, 1), F32), pltpu.VMEM((r, HEAD_DIM), F32)],
        compiler_params=_cparams(("parallel", "parallel", "parallel", "arbitrary")),
        name="flash_" + mode,
    )(*args)


def _banded_kernel(start_ref, var_ref, q_ref, k_ref, v_ref, kc_ref, vc_ref, bias_ref, sink_ref, o_ref,
                   *, mode, tq, band):
    del var_ref
    h = pl.program_id(1)
    j = pl.program_id(2)
    start = pl.multiple_of(start_ref[j], LANES)
    kb = k_ref[pl.ds(start, band), :]
    vb = v_ref[pl.ds(start, band), :]
    q = q_ref[...]
    groups = 4 if mode == "window" else 1
    outs = []
    for gi in range(groups):
        qg = q[:, gi * HEAD_DIM:(gi + 1) * HEAD_DIM]
        s_loc = lax.dot_general(qg, kb, (((1,), (1,)), ((), ())), preferred_element_type=F32) + bias_ref[...]
        s_ctx = lax.dot_general(qg, kc_ref[...], (((1,), (1,)), ((), ())), preferred_element_type=F32)
        m = jnp.maximum(jnp.max(s_loc, axis=-1, keepdims=True), jnp.max(s_ctx, axis=-1, keepdims=True))
        if mode == "window":
            sink = sink_ref[h * 4 + gi] * LOG2E
            m = jnp.maximum(m, sink)
        p_loc = jnp.exp2(s_loc - m)
        p_ctx = jnp.exp2(s_ctx - m)
        l = jnp.sum(p_loc, axis=-1, keepdims=True) + jnp.sum(p_ctx, axis=-1, keepdims=True)
        if mode == "window":
            l = l + jnp.exp2(sink - m)
        o = (jnp.dot(p_loc.astype(BF16), vb, preferred_element_type=F32)
             + jnp.dot(p_ctx.astype(BF16), vc_ref[...], preferred_element_type=F32))
        outs.append((o / l).astype(o_ref.dtype))
    for gi in range(groups):
        o_ref[:, gi * HEAD_DIM:(gi + 1) * HEAD_DIM] = outs[gi]


def _banded(mode, proj, proj_ctx, batch, starts, variants, bias, sinks, qcol, kcol, vcol, kc_col, vc_col):
    m = proj.shape[0]
    s = m // batch
    tq = 128
    nq = s // tq
    nc = proj_ctx.shape[0] // batch
    band = bias.shape[-1]
    if mode == "window":
        nh, qw = 2, 4 * HEAD_DIM
        q_spec = pl.BlockSpec((tq, qw), lambda b, h, j, st, va: (b * nq + j, qcol // 4 + h))
        bias_spec = pl.BlockSpec((None, tq, band), lambda b, h, j, st, va: (va[j], 0, 0))
    else:
        nh, qw = 8, HEAD_DIM
        q_spec = pl.BlockSpec((tq, qw), lambda b, h, j, st, va: (b * nq + j, qcol + h))
        bias_spec = pl.BlockSpec((None, None, tq, band), lambda b, h, j, st, va: (h, va[j], 0, 0))
    grid_spec = pltpu.PrefetchScalarGridSpec(
        num_scalar_prefetch=2,
        grid=(batch, nh, nq),
        in_specs=[
            q_spec,
            pl.BlockSpec((s, HEAD_DIM), lambda b, h, j, st, va: (b, kcol + h)),
            pl.BlockSpec((s, HEAD_DIM), lambda b, h, j, st, va: (b, vcol + h)),
            pl.BlockSpec((nc, HEAD_DIM), lambda b, h, j, st, va: (b, kc_col + h)),
            pl.BlockSpec((nc, HEAD_DIM), lambda b, h, j, st, va: (b, vc_col + h)),
            bias_spec,
            pl.BlockSpec(memory_space=pltpu.SMEM),
        ],
        out_specs=pl.BlockSpec((tq, qw), lambda b, h, j, st, va: (b * nq + j, h)),
    )
    kern = functools.partial(_banded_kernel, mode=mode, tq=tq, band=band)
    return pl.pallas_call(
        kern,
        out_shape=jax.ShapeDtypeStruct((m, 8 * HEAD_DIM), BF16),
        grid_spec=grid_spec,
        compiler_params=_cparams(("parallel", "parallel", "arbitrary")),
        name="banded_" + mode,
    )(starts, variants, proj, proj, proj, proj_ctx, proj_ctx, bias, sinks)


def _outproj_kernel(oa_ref, ob_ref, wa_ref, wb_ref, x_ref, gt_ref, o_ref):
    acc = jnp.dot(oa_ref[...], wa_ref[...], preferred_element_type=F32)
    acc = acc + jnp.dot(ob_ref[...], wb_ref[...], preferred_element_type=F32)
    o_ref[...] = x_ref[...] + gt_ref[...] * acc


def _outproj(oa, ob, w_bf, x2d, gate, rows_per_batch, tm):
    m, d = x2d.shape
    half = oa.shape[1]
    tiles_per_batch = rows_per_batch // tm
    return pl.pallas_call(
        _outproj_kernel,
        out_shape=jax.ShapeDtypeStruct((m, d), F32),
        grid=(m // tm,),
        in_specs=[pl.BlockSpec((tm, half), lambda i: (i, 0)),
                  pl.BlockSpec((tm, half), lambda i: (i, 0)),
                  pl.BlockSpec((half, d), lambda i: (0, 0)),
                  pl.BlockSpec((half, d), lambda i: (1, 0)),
                  pl.BlockSpec((tm, d), lambda i: (i, 0)),
                  pl.BlockSpec((None, 1, d), lambda i: (i // tiles_per_batch, 0, 0))],
        out_specs=pl.BlockSpec((tm, d), lambda i: (i, 0)),
        compiler_params=_cparams(("parallel",)),
        name="outproj",
    )(oa, ob, w_bf, w_bf, x2d, gate)


def _swiglu_step(h, wg, wu, wd):
    g = jnp.dot(h, wg, preferred_element_type=F32)
    u = jnp.dot(h, wu, preferred_element_type=F32)
    a = (g * (1.0 / (1.0 + jnp.exp(-g))) * u).astype(BF16)
    return jnp.dot(a, wd, preferred_element_type=F32)


def _ffn_kernel(x_ref, g_ref, sc_ref, sh_ref, gt_ref, wg_ref, wu_ref, wd_ref, o_ref, h_scr, acc_scr, *, nf):
    j = pl.program_id(1)

    @pl.when(j == 0)
    def _():
        h_scr[...] = _modulate(x_ref[...], g_ref[...], sc_ref[...], sh_ref[...]).astype(BF16)
        acc_scr[...] = jnp.zeros_like(acc_scr)

    acc_scr[...] += _swiglu_step(h_scr[...], wg_ref[...], wu_ref[...], wd_ref[...])

    @pl.when(j == nf - 1)
    def _():
        o_ref[...] = x_ref[...] + gt_ref[...] * acc_scr[...]


def _ffn(x2d, rows_per_batch, g, scale, shift, gate, wg, wu, wd, tm):
    m, d = x2d.shape
    f = wg.shape[1]
    tf = 512
    nf = f // tf
    tiles_per_batch = rows_per_batch // tm
    vec = pl.BlockSpec((None, 1, d), lambda i, j: (i // tiles_per_batch, 0, 0))
    return pl.pallas_call(
        functools.partial(_ffn_kernel, nf=nf),
        out_shape=jax.ShapeDtypeStruct((m, d), F32),
        grid=(m // tm, nf),
        in_specs=[pl.BlockSpec((tm, d), lambda i, j: (i, 0)),
                  pl.BlockSpec((1, d), lambda i, j: (0, 0)),
                  vec, vec, vec,
                  pl.BlockSpec((d, tf), lambda i, j: (0, j)),
                  pl.BlockSpec((d, tf), lambda i, j: (0, j)),
                  pl.BlockSpec((tf, d), lambda i, j: (j, 0))],
        out_specs=pl.BlockSpec((tm, d), lambda i, j: (i, 0)),
        scratch_shapes=[pltpu.VMEM((tm, d), BF16), pltpu.VMEM((tm, d), F32)],
        compiler_params=_cparams(("parallel", "arbitrary")),
        name="ffn",
    )(x2d, g.reshape(1, d), scale, shift, gate, wg, wu, wd)


def _router_kernel(x_ref, g_ref, sc_ref, sh_ref, wr_ref, h_ref, ri_ref, rg_ref, *, n_exp):
    h = _modulate(x_ref[...], g_ref[...], sc_ref[...], sh_ref[...])
    h_ref[...] = h
    logits = jnp.dot(h, wr_ref[...], preferred_element_type=F32, precision=lax.Precision.HIGHEST)
    lane = lax.broadcasted_iota(jnp.int32, logits.shape, 1)
    logits = jnp.where(lane < n_exp, logits, -jnp.inf)
    lane_f = lane.astype(F32)
    m1 = jnp.max(logits, axis=-1, keepdims=True)
    i1 = jnp.min(jnp.where(logits == m1, lane_f, float(LANES)), axis=-1, keepdims=True)
    rest = jnp.where(lane_f == i1, -jnp.inf, logits)
    m2 = jnp.max(rest, axis=-1, keepdims=True)
    i2 = jnp.min(jnp.where(rest == m2, lane_f, float(LANES)), axis=-1, keepdims=True)
    e = jnp.exp(m2 - m1)
    g1 = 1.0 / (1.0 + e)
    g2 = e / (1.0 + e)
    ri_ref[...] = jnp.where(lane == 0, i1, jnp.where(lane == 1, i2, 0.0)).astype(jnp.int32)
    rg_ref[...] = jnp.where(lane == 0, g1, jnp.where(lane == 1, g2, 0.0))


def _router(x2d, rows_per_batch, g, scale, shift, wr_pad, n_exp, tm):
    m, d = x2d.shape
    tiles_per_batch = rows_per_batch // tm
    vec = pl.BlockSpec((None, 1, d), lambda i: (i // tiles_per_batch, 0, 0))
    return pl.pallas_call(
        functools.partial(_router_kernel, n_exp=n_exp),
        out_shape=(jax.ShapeDtypeStruct((m, d), F32),
                   jax.ShapeDtypeStruct((m, LANES), jnp.int32),
                   jax.ShapeDtypeStruct((m, LANES), F32)),
        grid=(m // tm,),
        in_specs=[pl.BlockSpec((tm, d), lambda i: (i, 0)),
                  pl.BlockSpec((1, d), lambda i: (0, 0)),
                  vec, vec,
                  pl.BlockSpec((d, LANES), lambda i: (0, 0))],
        out_specs=(pl.BlockSpec((tm, d), lambda i: (i, 0)),
                   pl.BlockSpec((tm, LANES), lambda i: (i, 0)),
                   pl.BlockSpec((tm, LANES), lambda i: (i, 0))),
        compiler_params=_cparams(("parallel",)),
        name="router",
    )(x2d, g.reshape(1, d), scale, shift, wr_pad)


def _dispatch_kernel(tok_ref, h_hbm, z_hbm, xs_hbm, sem, *, tb, n_tok):
    base = pl.program_id(0) * tb

    def row_copy(r):
        tok = tok_ref[base + r]
        dst = xs_hbm.at[pl.ds(base + r, 1)]

        @pl.when(tok < n_tok)
        def _():
            pltpu.make_async_copy(h_hbm.at[pl.ds(tok, 1)], dst, sem).start()

        @pl.when(tok >= n_tok)
        def _():
            pltpu.make_async_copy(z_hbm, dst, sem).start()

    def issue(r, carry):
        row_copy(r)
        return carry

    lax.fori_loop(0, tb, issue, 0)

    def drain(r, carry):
        pltpu.make_async_copy(z_hbm, xs_hbm.at[pl.ds(base + r, 1)], sem).wait()
        return carry

    lax.fori_loop(0, tb, drain, 0)


def _dispatch(slot_tok, h2d, tb):
    n_tok, d = h2d.shape
    cap = slot_tok.shape[0]
    zrow = jnp.zeros((1, d), h2d.dtype)
    grid_spec = pltpu.PrefetchScalarGridSpec(
        num_scalar_prefetch=1,
        grid=(cap // tb,),
        in_specs=[pl.BlockSpec(memory_space=pl.ANY), pl.BlockSpec(memory_space=pl.ANY)],
        out_specs=pl.BlockSpec(memory_space=pl.ANY),
        scratch_shapes=[pltpu.SemaphoreType.DMA],
    )
    return pl.pallas_call(
        functools.partial(_dispatch_kernel, tb=tb, n_tok=n_tok),
        out_shape=jax.ShapeDtypeStruct((cap, d), h2d.dtype),
        grid_spec=grid_spec,
        compiler_params=_cparams(("arbitrary",)),
        name="dispatch",
    )(slot_tok, h2d, zrow)


def _moe_kernel(be_ref, nu_ref, x_ref, wg_ref, wu_ref, wd_ref, o_ref, xb_scr, acc_scr, *, nf):
    del be_ref
    i = pl.program_id(0)
    j = pl.program_id(1)
    used = i < nu_ref[0]

    @pl.when(jnp.logical_and(used, j == 0))
    def _():
        xb_scr[...] = x_ref[...].astype(BF16)
        acc_scr[...] = jnp.zeros_like(acc_scr)

    @pl.when(used)
    def _():
        acc_scr[...] += _swiglu_step(xb_scr[...], wg_ref[...], wu_ref[...], wd_ref[...])

    @pl.when(jnp.logical_and(used, j == nf - 1))
    def _():
        o_ref[...] = acc_scr[...]

    @pl.when(jnp.logical_and(jnp.logical_not(used), j == nf - 1))
    def _():
        o_ref[...] = jnp.zeros_like(o_ref)


def _moe(xs, block_expert, n_used, wg, wu, wd, tm):
    cap, d = xs.shape
    f = wg.shape[2]
    tf = 512
    nf = f // tf

    def row_map(i, j, be, nu):
        return (jnp.minimum(i, nu[0] - 1), 0)

    def col_j(i, j, nu):
        return jnp.where(i < nu[0], j, nf - 1)

    grid_spec = pltpu.PrefetchScalarGridSpec(
        num_scalar_prefetch=2,
        grid=(cap // tm, nf),
        in_specs=[pl.BlockSpec((tm, d), row_map),
                  pl.BlockSpec((None, d, tf), lambda i, j, be, nu: (be[i], 0, col_j(i, j, nu))),
                  pl.BlockSpec((None, d, tf), lambda i, j, be, nu: (be[i], 0, col_j(i, j, nu))),
                  pl.BlockSpec((None, tf, d), lambda i, j, be, nu: (be[i], col_j(i, j, nu), 0))],
        out_specs=pl.BlockSpec((tm, d), lambda i, j, be, nu: (i, 0)),
        scratch_shapes=[pltpu.VMEM((tm, d), BF16), pltpu.VMEM((tm, d), F32)],
    )
    return pl.pallas_call(
        functools.partial(_moe_kernel, nf=nf),
        out_shape=jax.ShapeDtypeStruct((cap, d), F32),
        grid_spec=grid_spec,
        compiler_params=_cparams(("parallel", "arbitrary")),
        name="moe",
    )(block_expert, n_used, xs, wg, wu, wd)


def _combine_kernel(slot_ref, x_ref, gt_ref, rg_ref, fg_ref, y_hbm, o_ref, buf, sem, *, tc):
    base = pl.program_id(0) * tc

    def issue(r, carry):
        for k in range(TOP_K):
            s = slot_ref[TOP_K * (base + r) + k]
            pltpu.make_async_copy(y_hbm.at[pl.ds(s, 1)], buf.at[k, pl.ds(r, 1)], sem).start()
        return carry

    lax.fori_loop(0, tc, issue, 0)

    def drain(r, carry):
        for k in range(TOP_K):
            pltpu.make_async_copy(y_hbm.at[pl.ds(0, 1)], buf.at[k, pl.ds(r, 1)], sem).wait()
        return carry

    lax.fori_loop(0, tc, drain, 0)
    rg = rg_ref[...]
    y = rg[:, 0:1] * buf[0] + rg[:, 1:2] * buf[1]
    xo = x_ref[...] + gt_ref[...] * y
    ms = jnp.mean(xo * xo, axis=-1, keepdims=True)
    o_ref[...] = xo * lax.rsqrt(ms + EPS) * fg_ref[...]


def _combine(slot, x2d, rows_per_batch, gate, route_g, final_g, y, tc):
    m, d = x2d.shape
    tiles_per_batch = rows_per_batch // tc
    grid_spec = pltpu.PrefetchScalarGridSpec(
        num_scalar_prefetch=1,
        grid=(m // tc,),
        in_specs=[pl.BlockSpec((tc, d), lambda i, sl: (i, 0)),
                  pl.BlockSpec((None, 1, d), lambda i, sl: (i // tiles_per_batch, 0, 0)),
                  pl.BlockSpec((tc, LANES), lambda i, sl: (i, 0)),
                  pl.BlockSpec((1, d), lambda i, sl: (0, 0)),
                  pl.BlockSpec(memory_space=pl.ANY)],
        out_specs=pl.BlockSpec((tc, d), lambda i, sl: (i, 0)),
        scratch_shapes=[pltpu.VMEM((TOP_K, tc, d), F32), pltpu.SemaphoreType.DMA],
    )
    return pl.pallas_call(
        functools.partial(_combine_kernel, tc=tc),
        out_shape=jax.ShapeDtypeStruct((m, d), F32),
        grid_spec=grid_spec,
        compiler_params=_cparams(("arbitrary",)),
        name="combine",
    )(slot, x2d, gate, route_g, final_g.reshape(1, d), y)


def _rope_tables(n_tok, dim):
    t = jnp.arange(n_tok)
    pos = jnp.stack([t // GRID_W, t % GRID_W], axis=-1).astype(F32)
    n_pairs = dim // 4
    freq = ROPE_BASE ** (-jnp.arange(n_pairs, dtype=F32) / n_pairs)
    ang = (pos[:, :, None] * freq).reshape(n_tok, 2 * n_pairs)
    cos = jnp.repeat(jnp.cos(ang), 2, axis=1)
    sin = jnp.repeat(jnp.sin(ang), 2, axis=1) * jnp.tile(jnp.array([-1.0, 1.0], F32), dim // 2)
    return cos, sin


def _dedupe(patterns):
    keys, variant = {}, []
    for p in patterns:
        variant.append(keys.setdefault(p.tobytes(), len(keys)))
    first = [variant.index(v) for v in range(len(keys))]
    return np.asarray(variant, np.int32), first


def _window_tables(s):
    tq = 128
    nb = s // tq
    band = 3 * tq
    starts = np.clip(np.arange(nb) - 1, 0, nb - 3) * tq
    qpos = np.arange(nb)[:, None] * tq + np.arange(tq)[None, :]
    kpos = starts[:, None] + np.arange(band)[None, :]
    ok = np.abs(kpos[:, None, :] - qpos[:, :, None]) <= C_WINDOW
    variant, first = _dedupe(list(ok))
    bias = np.where(ok[first], 0.0, NEG_INF).astype(np.float32)
    return starts.astype(np.int32), variant, jnp.asarray(bias)


def _na_tables(s, rpb):
    tq = 128
    rows = s // GRID_W
    wr = min(NA_ROWS, rows)
    nb = s // tq
    band_blocks = 5
    band = band_blocks * tq
    starts = np.clip(np.arange(nb) - 2, 0, nb - band_blocks) * tq
    qpos = np.arange(nb)[:, None] * tq + np.arange(tq)[None, :]
    kpos = starts[:, None] + np.arange(band)[None, :]
    qrow, qcol = qpos // GRID_W, qpos % GRID_W
    krow, kcol = kpos // GRID_W, kpos % GRID_W
    rs = np.clip(qrow - wr // 2, 0, rows - wr)[:, :, None]
    cstart = np.clip(qcol - NA_COLS // 2, 0, GRID_W - NA_COLS)[:, :, None]
    kr, kc = krow[:, None, :], kcol[:, None, :]
    ok = (kr >= rs) & (kr < rs + wr) & (kc >= cstart) & (kc < cstart + NA_COLS)
    drow = np.clip(kr - qrow[:, :, None] + NA_ROWS - 1, 0, 2 * NA_ROWS - 2)
    dcol = np.clip(kc - qcol[:, :, None] + NA_COLS - 1, 0, 2 * NA_COLS - 2)
    pat = [np.stack([ok[j].astype(np.int32), drow[j], dcol[j]]) for j in range(nb)]
    variant, first = _dedupe(pat)
    okf, drf, dcf = ok[first], drow[first], dcol[first]
    bias = jnp.where(jnp.asarray(okf)[None], rpb.astype(F32)[:, drf, dcf] * LOG2E, NEG_INF)
    return starts.astype(np.int32), variant, bias


def kernel(x, c, ctx, c_ctx, l0_norm1_g, l0_norm2_g, l0_w_ada, l0_b_ada, l0_w_in, l0_w_out, l0_lam_q1, l0_lam_k1, l0_lam_q2, l0_lam_k2, l0_subln_g, l0_q_norm_g, l0_k_norm_g, l0_ffn_w_gate, l0_ffn_w_up, l0_ffn_w_down, l1_norm1_g, l1_norm2_g, l1_w_ada, l1_b_ada, l1_w_in, l1_w_out, l1_sinks, l1_rpb, l1_router, l1_exp_w_gate, l1_exp_w_up, l1_exp_w_down, final_norm_g):
    b, s, d = x.shape
    n_ctx = ctx.shape[1]
    n_exp = l1_router.shape[1]
    n = b * s
    assert d == 2048 and s % 512 == 0 and n_ctx == 256 and s // GRID_W >= 10

    x2d = x.reshape(n, d)
    xc2d = ctx.reshape(b * n_ctx, d)

    cvecs = jnp.zeros((8, d), F32).at[:b].set(c).at[b].set(c_ctx)
    mod0 = _ada(cvecs, l0_w_ada, l0_b_ada)
    mod1 = _ada(cvecs, l1_w_ada, l1_b_ada)

    def lat(mod, k):
        return mod[:b, k * d:(k + 1) * d].reshape(b, 1, d)

    def cx(mod, k):
        return jnp.broadcast_to(mod[b, k * d:(k + 1) * d].reshape(1, 1, d), (b, 1, d))

    cos_a, sin_a = _rope_tables(s, A_SUB)
    cos_b, sin_b = _rope_tables(s, HEAD_DIM)
    cos_tab = jnp.stack([jnp.tile(cos_a, (1, 2)), cos_b])
    sin_tab = jnp.stack([jnp.tile(sin_a, (1, 2)), sin_b])
    tm = 512

    qa = T_ROPE | T_SCALE_A
    qb = T_ROPE | T_QNORM | T_SCALE_B
    kb = T_ROPE | T_KNORM
    types0 = np.array([qa] * 8 + [qb] * 8 + [T_ROPE] * 8 + [0] * 8 + [kb] * 2 + [0] * 2, np.int32)
    tabsel0 = np.array([0, 0, 1, 1, 0, 0, 0, 0, 1], np.int32)
    qkg0 = jnp.stack([l0_q_norm_g, l0_k_norm_g]).astype(F32)
    w_in0 = l0_w_in.astype(BF16)
    p0 = _inproj(x2d, s, l0_norm1_g, lat(mod0, 1), lat(mod0, 0), w_in0, cos_tab, sin_tab, qkg0,
                 jnp.asarray(types0), jnp.asarray(tabsel0), tm)
    p0c = _inproj(xc2d, n_ctx, l0_norm1_g, cx(mod0, 1), cx(mod0, 0), w_in0, cos_tab, sin_tab, qkg0,
                  jnp.asarray(types0 & ~T_ROPE), jnp.asarray(tabsel0), n_ctx)
    lam = jnp.stack([l0_lam_q1, l0_lam_k1, l0_lam_q2, l0_lam_k2]).astype(F32)
    subg = l0_subln_g.reshape(1, HEAD_DIM).astype(F32)
    oa = _flash("diff", p0, p0, p0c, b, 512, 512, 0, 16, 24, lam, subg)
    ob = _flash("gqa", p0, p0, p0c, b, 256, 512, 8, 32, 34)
    oac = _flash("diff", p0c, p0c, None, b, n_ctx, n_ctx, 0, 16, 24, lam, subg)
    obc = _flash("gqa", p0c, p0c, None, b, n_ctx, n_ctx, 8, 32, 34)
    w_out0 = l0_w_out.astype(BF16)
    x1 = _outproj(oa, ob, w_out0, x2d, lat(mod0, 2), s, tm)
    xc1 = _outproj(oac, obc, w_out0, xc2d, cx(mod0, 2), n_ctx, n_ctx)
    wg0, wu0, wd0 = l0_ffn_w_gate.astype(BF16), l0_ffn_w_up.astype(BF16), l0_ffn_w_down.astype(BF16)
    x2 = _ffn(x1, s, l0_norm2_g, lat(mod0, 4), lat(mod0, 3), lat(mod0, 5), wg0, wu0, wd0, tm)
    xc2 = _ffn(xc1, n_ctx, l0_norm2_g, cx(mod0, 4), cx(mod0, 3), cx(mod0, 5), wg0, wu0, wd0, n_ctx)

    types1 = np.array([T_ROPE | T_SCALE_B] * 8 + [T_SCALE_B] * 8 + [T_ROPE] * 2 + [0] * 2 + [0] * 16, np.int32)
    tabsel1 = np.array([1] * 9, np.int32)
    w_in1 = l1_w_in.astype(BF16)
    qkg1 = jnp.ones((2, HEAD_DIM), F32)
    p1 = _inproj(x2, s, l1_norm1_g, lat(mod1, 1), lat(mod1, 0), w_in1, cos_tab, sin_tab, qkg1,
                 jnp.asarray(types1), jnp.asarray(tabsel1), tm)
    l1_q = 16 * HEAD_DIM
    n_kv_tiles = (w_in1.shape[1] - l1_q) // 512
    p1c = _inproj(xc2, n_ctx, l1_norm1_g, cx(mod1, 1), cx(mod1, 0), w_in1[:, l1_q:], cos_tab, sin_tab, qkg1,
                  jnp.zeros((n_kv_tiles * 4,), jnp.int32), jnp.zeros((n_kv_tiles,), jnp.int32), n_ctx)
    w_st, w_var, w_bias = _window_tables(s)
    n_st, n_var, n_bias = _na_tables(s, l1_rpb)
    sinks = l1_sinks.astype(F32)
    oc = _banded("window", p1, p1c, b, jnp.asarray(w_st), jnp.asarray(w_var), w_bias, sinks, 0, 16, 18, 0, 2)
    od = _banded("na", p1, p1c, b, jnp.asarray(n_st), jnp.asarray(n_var), n_bias, sinks, 8, 20, 28, 4, 12)
    x3 = _outproj(oc, od, l1_w_out.astype(BF16), x2, lat(mod1, 2), s, tm)

    wr_pad = jnp.zeros((d, LANES), F32).at[:, :n_exp].set(l1_router.astype(F32))
    h2, route_i, route_g = _router(x3, s, l1_norm2_g, lat(mod1, 4), lat(mod1, 3), wr_pad, n_exp, tm)
    e_flat = route_i[:, :TOP_K].reshape(-1)
    onehot = (e_flat[:, None] == jnp.arange(n_exp, dtype=jnp.int32)[None, :]).astype(jnp.int32)
    csum = jnp.cumsum(onehot, axis=0)
    rank = jnp.sum((csum - onehot) * onehot, axis=1)
    counts = csum[-1]
    padded = (counts + tm - 1) // tm * tm
    pad_end = jnp.cumsum(padded)
    pad_start = pad_end - padded
    slot = (pad_start[e_flat] + rank).astype(jnp.int32)
    n_blocks = (n * TOP_K + n_exp * (tm - 1) + tm - 1) // tm
    cap = n_blocks * tm
    tok_flat = jnp.repeat(jnp.arange(n, dtype=jnp.int32), TOP_K)
    slot_tok = jnp.full((cap,), n, jnp.int32).at[slot].set(tok_flat)
    block_expert = jnp.minimum(jnp.searchsorted(pad_end, jnp.arange(n_blocks, dtype=jnp.int32) * tm, side="right"),
                               n_exp - 1).astype(jnp.int32)
    n_used = (pad_end[-1] // tm).astype(jnp.int32).reshape(1)

    xs = _dispatch(slot_tok, h2, 256)
    y = _moe(xs, block_expert, n_used, l1_exp_w_gate.astype(BF16), l1_exp_w_up.astype(BF16),
             l1_exp_w_down.astype(BF16), tm)
    out = _combine(slot, x3, s, lat(mod1, 5), route_g, final_norm_g, y, 256)
    return out.reshape(b, s, d)
```

```python
import functools
import math

import numpy as np
import jax
import jax.numpy as jnp
from jax import lax
from jax.experimental import pallas as pl
from jax.experimental.pallas import tpu as pltpu

F32 = jnp.float32
BF16 = jnp.bfloat16

HEAD_DIM = 128
LANES = 128
GRID_W = 64
ROPE_BASE = 10000.0
EPS = 1e-6
NEG_INF = -1e30
A_SUB = HEAD_DIM // 2
A_SCALE = A_SUB ** -0.5
ATT_SCALE = HEAD_DIM ** -0.5
LAMBDA_INIT_L0 = 0.8 - 0.6 * math.exp(-0.3 * 0)
LOG2E = 1.4426950408889634
C_WINDOW = 128
NA_ROWS = 8
NA_COLS = 16
TOP_K = 2
VMEM_LIMIT = 56 * 1024 * 1024

T_ROPE, T_QNORM, T_KNORM, T_SCALE_B, T_SCALE_A = 1, 2, 4, 8, 16


def _cparams(sem):
    return pltpu.CompilerParams(dimension_semantics=sem, vmem_limit_bytes=VMEM_LIMIT)


def _modulate(xf, g, scale, shift):
    ms = jnp.mean(xf * xf, axis=-1, keepdims=True)
    y = xf * lax.rsqrt(ms + EPS) * g
    return y * (1.0 + scale) + shift


def _ada_kernel(c_ref, w_ref, b_ref, o_ref):
    c = c_ref[...]
    a = c * (1.0 / (1.0 + jnp.exp(-c)))
    o_ref[...] = jnp.dot(a.astype(BF16), w_ref[...].astype(BF16), preferred_element_type=F32) + b_ref[...]


def _ada(cvecs, w_ada, b_ada):
    r, d = cvecs.shape
    n = w_ada.shape[1]
    tn = 1024
    return pl.pallas_call(
        _ada_kernel,
        out_shape=jax.ShapeDtypeStruct((r, n), F32),
        grid=(n // tn,),
        in_specs=[pl.BlockSpec((r, d), lambda j: (0, 0)),
                  pl.BlockSpec((d, tn), lambda j: (0, j)),
                  pl.BlockSpec((1, tn), lambda j: (0, j))],
        out_specs=pl.BlockSpec((r, tn), lambda j: (0, j)),
        compiler_params=_cparams(("parallel",)),
        name="ada",
    )(cvecs, w_ada, b_ada.reshape(1, n))


def _inproj_kernel(types_ref, tab_ref, x_ref, g_ref, sc_ref, sh_ref, w_ref, cos_ref, sin_ref, qkg_ref,
                   o_ref, h_scr, *, heads_per_tile):
    j = pl.program_id(1)

    @pl.when(j == 0)
    def _():
        h_scr[...] = _modulate(x_ref[...], g_ref[...], sc_ref[...], sh_ref[...]).astype(BF16)

    acc = jnp.dot(h_scr[...], w_ref[...], preferred_element_type=F32)
    lane = lax.broadcasted_iota(jnp.int32, (1, LANES), 1)
    even = (lane & 1) == 0
    for hh in range(heads_per_tile):
        t = types_ref[j * heads_per_tile + hh]
        sl = slice(hh * HEAD_DIM, (hh + 1) * HEAD_DIM)
        y = acc[:, sl]

        @pl.when(t == 0)
        def _():
            o_ref[:, sl] = y.astype(o_ref.dtype)

        @pl.when(t != 0)
        def _():
            use_q = (t & T_QNORM) != 0
            use_k = (t & T_KNORM) != 0
            ms = jnp.mean(y * y, axis=-1, keepdims=True)
            gain = jnp.where(use_q, qkg_ref[0:1, :], qkg_ref[1:2, :])
            yn = y * lax.rsqrt(ms + EPS) * gain
            y1 = jnp.where(jnp.logical_or(use_q, use_k), yn, y)
            sw = jnp.where(even, pltpu.roll(y1, LANES - 1, 1), pltpu.roll(y1, 1, 1))
            yr = y1 * cos_ref[...] + sw * sin_ref[...]
            y2 = jnp.where((t & T_ROPE) != 0, yr, y1)
            f = jnp.where((t & T_SCALE_B) != 0, ATT_SCALE * LOG2E,
                          jnp.where((t & T_SCALE_A) != 0, A_SCALE * LOG2E, 1.0)).astype(F32)
            o_ref[:, sl] = (y2 * f).astype(o_ref.dtype)


def _inproj(x2d, rows_per_batch, g, scale, shift, w_bf, cos_tab, sin_tab, qkg, types, tabsel, tm):
    m, d = x2d.shape
    n = w_bf.shape[1]
    tn = 512
    hpt = tn // HEAD_DIM
    tiles_per_batch = rows_per_batch // tm
    tab_rows = cos_tab.shape[1] // tm
    kern = functools.partial(_inproj_kernel, heads_per_tile=hpt)
    grid_spec = pltpu.PrefetchScalarGridSpec(
        num_scalar_prefetch=2,
        grid=(m // tm, n // tn),
        in_specs=[
            pl.BlockSpec((tm, d), lambda i, j, ty, tb: (i, 0)),
            pl.BlockSpec((1, d), lambda i, j, ty, tb: (0, 0)),
            pl.BlockSpec((None, 1, d), lambda i, j, ty, tb: (i // tiles_per_batch, 0, 0)),
            pl.BlockSpec((None, 1, d), lambda i, j, ty, tb: (i // tiles_per_batch, 0, 0)),
            pl.BlockSpec((d, tn), lambda i, j, ty, tb: (0, j)),
            pl.BlockSpec((None, tm, LANES), lambda i, j, ty, tb: (tb[j], i % tab_rows, 0)),
            pl.BlockSpec((None, tm, LANES), lambda i, j, ty, tb: (tb[j], i % tab_rows, 0)),
            pl.BlockSpec((2, HEAD_DIM), lambda i, j, ty, tb: (0, 0)),
        ],
        out_specs=pl.BlockSpec((tm, tn), lambda i, j, ty, tb: (i, j)),
        scratch_shapes=[pltpu.VMEM((tm, d), BF16)],
    )
    return pl.pallas_call(
        kern,
        out_shape=jax.ShapeDtypeStruct((m, n), BF16),
        grid_spec=grid_spec,
        compiler_params=_cparams(("parallel", "arbitrary")),
        name="inproj",
    )(types, tabsel, x2d, g.reshape(1, d), scale, shift, w_bf, cos_tab, sin_tab, qkg)


def _flash_kernel(*refs, mode, tq, has_ctx, nk):
    refs = list(refs)
    q_ref, k_ref, v_ref = refs[:3]
    pos = 3
    if has_ctx:
        kc_ref, vc_ref = refs[3:5]
        pos = 5
    if mode == "diff":
        lam_ref, subg_ref = refs[pos:pos + 2]
        pos += 2
    o_ref, qs, m_s, l_s, acc_s = refs[pos:pos + 5]
    ki = pl.program_id(3)

    @pl.when(ki == 0)
    def _():
        q = q_ref[...]
        if mode == "diff":
            lane = lax.broadcasted_iota(jnp.int32, (1, LANES), 1)
            zero = jnp.zeros_like(q)
            qs[0:tq, :] = jnp.where(lane < A_SUB, q, zero)
            qs[tq:2 * tq, :] = jnp.where(lane >= A_SUB, q, zero)
        else:
            for gi in range(4):
                qs[gi * tq:(gi + 1) * tq, :] = q[:, gi * HEAD_DIM:(gi + 1) * HEAD_DIM]
        m_s[...] = jnp.full_like(m_s, NEG_INF)
        l_s[...] = jnp.zeros_like(l_s)
        acc_s[...] = jnp.zeros_like(acc_s)

    def step(k, v):
        s = lax.dot_general(qs[...], k, (((1,), (1,)), ((), ())), preferred_element_type=F32)
        slabs = [s[:, c * LANES:(c + 1) * LANES] for c in range(k.shape[0] // LANES)]
        mx = functools.reduce(jnp.maximum, slabs)
        m_prev = m_s[...]
        m_new = jnp.maximum(m_prev, jnp.max(mx, axis=-1, keepdims=True))
        alpha = jnp.exp2(m_prev - m_new)
        ps = [jnp.exp2(sl - m_new) for sl in slabs]
        l_s[...] = alpha * l_s[...] + functools.reduce(jnp.add, ps)
        p = jnp.concatenate([x.astype(BF16) for x in ps], axis=1)
        acc_s[...] = alpha * acc_s[...] + jnp.dot(p, v, preferred_element_type=F32)
        m_s[...] = m_new

    step(k_ref[...], v_ref[...])

    @pl.when(ki == nk - 1)
    def _():
        if has_ctx:
            step(kc_ref[...], vc_ref[...])
        o = acc_s[...] / jnp.sum(l_s[...], axis=-1, keepdims=True)
        if mode == "diff":
            lam1 = jnp.sum(lam_ref[0:1, :] * lam_ref[1:2, :], axis=-1, keepdims=True)
            lam2 = jnp.sum(lam_ref[2:3, :] * lam_ref[3:4, :], axis=-1, keepdims=True)
            lam = jnp.exp(lam1) - jnp.exp(lam2) + LAMBDA_INIT_L0
            od = o[0:tq, :] - lam * o[tq:2 * tq, :]
            ms = jnp.mean(od * od, axis=-1, keepdims=True)
            on = od * lax.rsqrt(ms + EPS) * subg_ref[...]
            o_ref[...] = (on * (1.0 - LAMBDA_INIT_L0)).astype(o_ref.dtype)
        else:
            for gi in range(4):
                o_ref[:, gi * HEAD_DIM:(gi + 1) * HEAD_DIM] = o[gi * tq:(gi + 1) * tq, :].astype(o_ref.dtype)


def _flash(mode, q_arr, kv_arr, ctx_arr, batch, tq, tk, qcol, kcol, vcol, lam=None, subg=None):
    mq = q_arr.shape[0]
    sq = mq // batch
    sk = kv_arr.shape[0] // batch
    nq, nk = sq // tq, sk // tk
    has_ctx = ctx_arr is not None
    if mode == "diff":
        nh, qw, reps = 8, HEAD_DIM, 2
        q_spec = pl.BlockSpec((tq, qw), lambda b, h, qi, ki: (b * nq + qi, qcol + h))
        o_spec = pl.BlockSpec((tq, qw), lambda b, h, qi, ki: (b * nq + qi, h))
    else:
        nh, qw, reps = 2, 4 * HEAD_DIM, 4
        q_spec = pl.BlockSpec((tq, qw), lambda b, h, qi, ki: (b * nq + qi, qcol // 4 + h))
        o_spec = pl.BlockSpec((tq, qw), lambda b, h, qi, ki: (b * nq + qi, h))
    in_specs = [q_spec,
                pl.BlockSpec((tk, HEAD_DIM), lambda b, h, qi, ki: (b * nk + ki, kcol + h)),
                pl.BlockSpec((tk, HEAD_DIM), lambda b, h, qi, ki: (b * nk + ki, vcol + h))]
    args = [q_arr, kv_arr, kv_arr]
    if has_ctx:
        nc = ctx_arr.shape[0] // batch
        in_specs += [pl.BlockSpec((nc, HEAD_DIM), lambda b, h, qi, ki: (b, kcol + h)),
                     pl.BlockSpec((nc, HEAD_DIM), lambda b, h, qi, ki: (b, vcol + h))]
        args += [ctx_arr, ctx_arr]
    if mode == "diff":
        in_specs += [pl.BlockSpec((4, A_SUB), lambda b, h, qi, ki: (0, 0)),
                     pl.BlockSpec((1, HEAD_DIM), lambda b, h, qi, ki: (0, 0))]
        args += [lam, subg]
    r = reps * tq
    kern = functools.partial(_flash_kernel, mode=mode, tq=tq, has_ctx=has_ctx, nk=nk)
    return pl.pallas_call(
        kern,
        out_shape=jax.ShapeDtypeStruct((mq, 8 * HEAD_DIM), BF16),
        grid=(batch, nh, nq, nk),
        in_specs=in_specs,
        out_specs=o_spec,
        scratch_shapes=[pltpu.VMEM((r, HEAD_DIM), BF16), pltpu.VMEM((r, LANES), F32),
                        pltpu.VMEM((r, LANES), F32), pltpu.VMEM((r, HEAD_DIM), F32)],
        compiler_params=_cparams(("parallel", "parallel", "parallel", "arbitrary")),
        name="flash_" + mode,
    )(*args)


def _banded_kernel(start_ref, var_ref, q_ref, k_ref, v_ref, kc_ref, vc_ref, bias_ref, sink_ref, o_ref,
                   *, mode, tq, band):
    del var_ref
    h = pl.program_id(1)
    j = pl.program_id(2)
    start = pl.multiple_of(start_ref[j], LANES)
    kb = k_ref[pl.ds(start, band), :]
    vb = v_ref[pl.ds(start, band), :]
    q = q_ref[...]
    groups = 4 if mode == "window" else 1
    outs = []
    for gi in range(groups):
        qg = q[:, gi * HEAD_DIM:(gi + 1) * HEAD_DIM]
        s_loc = lax.dot_general(qg, kb, (((1,), (1,)), ((), ())), preferred_element_type=F32) + bias_ref[...]
        s_ctx = lax.dot_general(qg, kc_ref[...], (((1,), (1,)), ((), ())), preferred_element_type=F32)
        n_loc = band // LANES
        slabs = ([s_loc[:, c * LANES:(c + 1) * LANES] for c in range(n_loc)]
                 + [s_ctx[:, c * LANES:(c + 1) * LANES] for c in range(s_ctx.shape[1] // LANES)])
        mx = functools.reduce(jnp.maximum, slabs)
        m = jnp.broadcast_to(jnp.max(mx, axis=-1, keepdims=True), mx.shape)
        if mode == "window":
            sink = sink_ref[h * 4 + gi] * LOG2E
            m = jnp.maximum(m, sink)
        ps = [jnp.exp2(sl - m) for sl in slabs]
        l = jnp.sum(functools.reduce(jnp.add, ps), axis=-1, keepdims=True)
        if mode == "window":
            l = l + jnp.exp2(sink - m[:, 0:1])
        p_loc = jnp.concatenate([x.astype(BF16) for x in ps[:n_loc]], axis=1)
        p_ctx = jnp.concatenate([x.astype(BF16) for x in ps[n_loc:]], axis=1)
        o = (jnp.dot(p_loc, vb, preferred_element_type=F32)
             + jnp.dot(p_ctx, vc_ref[...], preferred_element_type=F32))
        outs.append((o / l).astype(o_ref.dtype))
    for gi in range(groups):
        o_ref[:, gi * HEAD_DIM:(gi + 1) * HEAD_DIM] = outs[gi]


def _banded(mode, proj, proj_ctx, batch, starts, variants, bias, sinks, qcol, kcol, vcol, kc_col, vc_col):
    m = proj.shape[0]
    s = m // batch
    tq = 128
    nq = s // tq
    nc = proj_ctx.shape[0] // batch
    band = bias.shape[-1]
    if mode == "window":
        nh, qw = 2, 4 * HEAD_DIM
        q_spec = pl.BlockSpec((tq, qw), lambda b, h, j, st, va: (b * nq + j, qcol // 4 + h))
        bias_spec = pl.BlockSpec((None, tq, band), lambda b, h, j, st, va: (va[j], 0, 0))
    else:
        nh, qw = 8, HEAD_DIM
        q_spec = pl.BlockSpec((tq, qw), lambda b, h, j, st, va: (b * nq + j, qcol + h))
        bias_spec = pl.BlockSpec((None, None, tq, band), lambda b, h, j, st, va: (h, va[j], 0, 0))
    grid_spec = pltpu.PrefetchScalarGridSpec(
        num_scalar_prefetch=2,
        grid=(batch, nh, nq),
        in_specs=[
            q_spec,
            pl.BlockSpec((s, HEAD_DIM), lambda b, h, j, st, va: (b, kcol + h)),
            pl.BlockSpec((s, HEAD_DIM), lambda b, h, j, st, va: (b, vcol + h)),
            pl.BlockSpec((nc, HEAD_DIM), lambda b, h, j, st, va: (b, kc_col + h)),
            pl.BlockSpec((nc, HEAD_DIM), lambda b, h, j, st, va: (b, vc_col + h)),
            bias_spec,
            pl.BlockSpec(memory_space=pltpu.SMEM),
        ],
        out_specs=pl.BlockSpec((tq, qw), lambda b, h, j, st, va: (b * nq + j, h)),
    )
    kern = functools.partial(_banded_kernel, mode=mode, tq=tq, band=band)
    return pl.pallas_call(
        kern,
        out_shape=jax.ShapeDtypeStruct((m, 8 * HEAD_DIM), BF16),
        grid_spec=grid_spec,
        compiler_params=_cparams(("parallel", "parallel", "arbitrary")),
        name="banded_" + mode,
    )(starts, variants, proj, proj, proj, proj_ctx, proj_ctx, bias, sinks)


def _outproj_kernel(oa_ref, ob_ref, wa_ref, wb_ref, x_ref, gt_ref, o_ref):
    acc = jnp.dot(oa_ref[...], wa_ref[...], preferred_element_type=F32)
    acc = acc + jnp.dot(ob_ref[...], wb_ref[...], preferred_element_type=F32)
    o_ref[...] = x_ref[...] + gt_ref[...] * acc


def _outproj(oa, ob, w_bf, x2d, gate, rows_per_batch, tm):
    m, d = x2d.shape
    half = oa.shape[1]
    tiles_per_batch = rows_per_batch // tm
    return pl.pallas_call(
        _outproj_kernel,
        out_shape=jax.ShapeDtypeStruct((m, d), F32),
        grid=(m // tm,),
        in_specs=[pl.BlockSpec((tm, half), lambda i: (i, 0)),
                  pl.BlockSpec((tm, half), lambda i: (i, 0)),
                  pl.BlockSpec((half, d), lambda i: (0, 0)),
                  pl.BlockSpec((half, d), lambda i: (1, 0)),
                  pl.BlockSpec((tm, d), lambda i: (i, 0)),
                  pl.BlockSpec((None, 1, d), lambda i: (i // tiles_per_batch, 0, 0))],
        out_specs=pl.BlockSpec((tm, d), lambda i: (i, 0)),
        compiler_params=_cparams(("parallel",)),
        name="outproj",
    )(oa, ob, w_bf, w_bf, x2d, gate)


def _swiglu_step(h, wg, wu, wd):
    g = jnp.dot(h, wg, preferred_element_type=F32)
    u = jnp.dot(h, wu, preferred_element_type=F32)
    a = (g * (1.0 / (1.0 + jnp.exp(-g))) * u).astype(BF16)
    return jnp.dot(a, wd, preferred_element_type=F32)


def _ffn_kernel(x_ref, g_ref, sc_ref, sh_ref, gt_ref, wg_ref, wu_ref, wd_ref, o_ref, h_scr, acc_scr, *, nf):
    j = pl.program_id(1)

    @pl.when(j == 0)
    def _():
        h_scr[...] = _modulate(x_ref[...], g_ref[...], sc_ref[...], sh_ref[...]).astype(BF16)
        acc_scr[...] = jnp.zeros_like(acc_scr)

    acc_scr[...] += _swiglu_step(h_scr[...], wg_ref[...], wu_ref[...], wd_ref[...])

    @pl.when(j == nf - 1)
    def _():
        o_ref[...] = x_ref[...] + gt_ref[...] * acc_scr[...]


def _ffn(x2d, rows_per_batch, g, scale, shift, gate, wg, wu, wd, tm):
    m, d = x2d.shape
    f = wg.shape[1]
    tf = 512
    nf = f // tf
    tiles_per_batch = rows_per_batch // tm
    vec = pl.BlockSpec((None, 1, d), lambda i, j: (i // tiles_per_batch, 0, 0))
    return pl.pallas_call(
        functools.partial(_ffn_kernel, nf=nf),
        out_shape=jax.ShapeDtypeStruct((m, d), F32),
        grid=(m // tm, nf),
        in_specs=[pl.BlockSpec((tm, d), lambda i, j: (i, 0)),
                  pl.BlockSpec((1, d), lambda i, j: (0, 0)),
                  vec, vec, vec,
                  pl.BlockSpec((d, tf), lambda i, j: (0, j)),
                  pl.BlockSpec((d, tf), lambda i, j: (0, j)),
                  pl.BlockSpec((tf, d), lambda i, j: (j, 0))],
        out_specs=pl.BlockSpec((tm, d), lambda i, j: (i, 0)),
        scratch_shapes=[pltpu.VMEM((tm, d), BF16), pltpu.VMEM((tm, d), F32)],
        compiler_params=_cparams(("parallel", "arbitrary")),
        name="ffn",
    )(x2d, g.reshape(1, d), scale, shift, gate, wg, wu, wd)


def _router_kernel(x_ref, g_ref, sc_ref, sh_ref, wr_ref, h_ref, ri_ref, rg_ref, *, n_exp):
    h = _modulate(x_ref[...], g_ref[...], sc_ref[...], sh_ref[...])
    h_ref[...] = h
    logits = jnp.dot(h, wr_ref[...], preferred_element_type=F32, precision=lax.Precision.HIGHEST)
    lane = lax.broadcasted_iota(jnp.int32, logits.shape, 1)
    logits = jnp.where(lane < n_exp, logits, -jnp.inf)
    lane_f = lane.astype(F32)
    m1 = jnp.max(logits, axis=-1, keepdims=True)
    i1 = jnp.min(jnp.where(logits == m1, lane_f, float(LANES)), axis=-1, keepdims=True)
    rest = jnp.where(lane_f == i1, -jnp.inf, logits)
    m2 = jnp.max(rest, axis=-1, keepdims=True)
    i2 = jnp.min(jnp.where(rest == m2, lane_f, float(LANES)), axis=-1, keepdims=True)
    e = jnp.exp(m2 - m1)
    g1 = 1.0 / (1.0 + e)
    g2 = e / (1.0 + e)
    ri_ref[...] = jnp.where(lane == 0, i1, jnp.where(lane == 1, i2, 0.0)).astype(jnp.int32)
    rg_ref[...] = jnp.where(lane == 0, g1, jnp.where(lane == 1, g2, 0.0))


def _router(x2d, rows_per_batch, g, scale, shift, wr_pad, n_exp, tm):
    m, d = x2d.shape
    tiles_per_batch = rows_per_batch // tm
    vec = pl.BlockSpec((None, 1, d), lambda i: (i // tiles_per_batch, 0, 0))
    return pl.pallas_call(
        functools.partial(_router_kernel, n_exp=n_exp),
        out_shape=(jax.ShapeDtypeStruct((m, d), F32),
                   jax.ShapeDtypeStruct((m, LANES), jnp.int32),
                   jax.ShapeDtypeStruct((m, LANES), F32)),
        grid=(m // tm,),
        in_specs=[pl.BlockSpec((tm, d), lambda i: (i, 0)),
                  pl.BlockSpec((1, d), lambda i: (0, 0)),
                  vec, vec,
                  pl.BlockSpec((d, LANES), lambda i: (0, 0))],
        out_specs=(pl.BlockSpec((tm, d), lambda i: (i, 0)),
                   pl.BlockSpec((tm, LANES), lambda i: (i, 0)),
                   pl.BlockSpec((tm, LANES), lambda i: (i, 0))),
        compiler_params=_cparams(("parallel",)),
        name="router",
    )(x2d, g.reshape(1, d), scale, shift, wr_pad)


def _dispatch_kernel(tok_ref, h_hbm, z_hbm, xs_ref, sem, *, tb, n_tok):
    base = pl.program_id(0) * tb

    def issue(r, carry):
        tok = tok_ref[base + r]
        dst = xs_ref.at[pl.ds(r, 1)]

        @pl.when(tok < n_tok)
        def _():
            pltpu.make_async_copy(h_hbm.at[pl.ds(tok, 1)], dst, sem).start()

        @pl.when(tok >= n_tok)
        def _():
            pltpu.make_async_copy(z_hbm, dst, sem).start()

        return carry

    lax.fori_loop(0, tb, issue, 0)

    def drain(r, carry):
        pltpu.make_async_copy(z_hbm, xs_ref.at[pl.ds(r, 1)], sem).wait()
        return carry

    lax.fori_loop(0, tb, drain, 0)


def _dispatch(slot_tok, h2d, tb):
    n_tok, d = h2d.shape
    cap = slot_tok.shape[0]
    zrow = jnp.zeros((1, d), h2d.dtype)
    grid_spec = pltpu.PrefetchScalarGridSpec(
        num_scalar_prefetch=1,
        grid=(cap // tb,),
        in_specs=[pl.BlockSpec(memory_space=pl.ANY), pl.BlockSpec(memory_space=pl.ANY)],
        out_specs=pl.BlockSpec((tb, d), lambda i, tok: (i, 0)),
        scratch_shapes=[pltpu.SemaphoreType.DMA],
    )
    return pl.pallas_call(
        functools.partial(_dispatch_kernel, tb=tb, n_tok=n_tok),
        out_shape=jax.ShapeDtypeStruct((cap, d), h2d.dtype),
        grid_spec=grid_spec,
        compiler_params=_cparams(("arbitrary",)),
        name="dispatch",
    )(slot_tok, h2d, zrow)


def _moe_kernel(be_ref, nu_ref, x_ref, wg_ref, wu_ref, wd_ref, o_ref, xb_scr, acc_scr, *, nf):
    del be_ref
    i = pl.program_id(0)
    j = pl.program_id(1)
    used = i < nu_ref[0]

    @pl.when(jnp.logical_and(used, j == 0))
    def _():
        xb_scr[...] = x_ref[...].astype(BF16)
        acc_scr[...] = jnp.zeros_like(acc_scr)

    @pl.when(used)
    def _():
        acc_scr[...] += _swiglu_step(xb_scr[...], wg_ref[...], wu_ref[...], wd_ref[...])

    @pl.when(jnp.logical_and(used, j == nf - 1))
    def _():
        o_ref[...] = acc_scr[...]

    @pl.when(jnp.logical_and(jnp.logical_not(used), j == nf - 1))
    def _():
        o_ref[...] = jnp.zeros_like(o_ref)


def _moe(xs, block_expert, n_used, wg, wu, wd, tm):
    cap, d = xs.shape
    f = wg.shape[2]
    tf = 512
    nf = f // tf

    def row_map(i, j, be, nu):
        return (jnp.minimum(i, nu[0] - 1), 0)

    def col_j(i, j, nu):
        return jnp.where(i < nu[0], j, nf - 1)

    grid_spec = pltpu.PrefetchScalarGridSpec(
        num_scalar_prefetch=2,
        grid=(cap // tm, nf),
        in_specs=[pl.BlockSpec((tm, d), row_map),
                  pl.BlockSpec((None, d, tf), lambda i, j, be, nu: (be[i], 0, col_j(i, j, nu))),
                  pl.BlockSpec((None, d, tf), lambda i, j, be, nu: (be[i], 0, col_j(i, j, nu))),
                  pl.BlockSpec((None, tf, d), lambda i, j, be, nu: (be[i], col_j(i, j, nu), 0))],
        out_specs=pl.BlockSpec((tm, d), lambda i, j, be, nu: (i, 0)),
        scratch_shapes=[pltpu.VMEM((tm, d), BF16), pltpu.VMEM((tm, d), F32)],
    )
    return pl.pallas_call(
        functools.partial(_moe_kernel, nf=nf),
        out_shape=jax.ShapeDtypeStruct((cap, d), F32),
        grid_spec=grid_spec,
        compiler_params=_cparams(("parallel", "arbitrary")),
        name="moe",
    )(block_expert, n_used, xs, wg, wu, wd)


def _combine_kernel(slot_ref, x_ref, gt_ref, rg_ref, fg_ref, y_hbm, o_ref, buf, sem, *, tc):
    base = pl.program_id(0) * tc

    def issue(r, carry):
        for k in range(TOP_K):
            s = slot_ref[TOP_K * (base + r) + k]
            pltpu.make_async_copy(y_hbm.at[pl.ds(s, 1)], buf.at[k, pl.ds(r, 1)], sem).start()
        return carry

    lax.fori_loop(0, tc, issue, 0)

    def drain(r, carry):
        for k in range(TOP_K):
            pltpu.make_async_copy(y_hbm.at[pl.ds(0, 1)], buf.at[k, pl.ds(r, 1)], sem).wait()
        return carry

    lax.fori_loop(0, tc, drain, 0)
    rg = rg_ref[...]
    y = rg[:, 0:1] * buf[0] + rg[:, 1:2] * buf[1]
    xo = x_ref[...] + gt_ref[...] * y
    ms = jnp.mean(xo * xo, axis=-1, keepdims=True)
    o_ref[...] = xo * lax.rsqrt(ms + EPS) * fg_ref[...]


def _combine(slot, x2d, rows_per_batch, gate, route_g, final_g, y, tc):
    m, d = x2d.shape
    tiles_per_batch = rows_per_batch // tc
    grid_spec = pltpu.PrefetchScalarGridSpec(
        num_scalar_prefetch=1,
        grid=(m // tc,),
        in_specs=[pl.BlockSpec((tc, d), lambda i, sl: (i, 0)),
                  pl.BlockSpec((None, 1, d), lambda i, sl: (i // tiles_per_batch, 0, 0)),
                  pl.BlockSpec((tc, LANES), lambda i, sl: (i, 0)),
                  pl.BlockSpec((1, d), lambda i, sl: (0, 0)),
                  pl.BlockSpec(memory_space=pl.ANY)],
        out_specs=pl.BlockSpec((tc, d), lambda i, sl: (i, 0)),
        scratch_shapes=[pltpu.VMEM((TOP_K, tc, d), F32), pltpu.SemaphoreType.DMA],
    )
    return pl.pallas_call(
        functools.partial(_combine_kernel, tc=tc),
        out_shape=jax.ShapeDtypeStruct((m, d), F32),
        grid_spec=grid_spec,
        compiler_params=_cparams(("arbitrary",)),
        name="combine",
    )(slot, x2d, gate, route_g, final_g.reshape(1, d), y)


def _rope_tables(n_tok, dim):
    t = jnp.arange(n_tok)
    pos = jnp.stack([t // GRID_W, t % GRID_W], axis=-1).astype(F32)
    n_pairs = dim // 4
    freq = ROPE_BASE ** (-jnp.arange(n_pairs, dtype=F32) / n_pairs)
    ang = (pos[:, :, None] * freq).reshape(n_tok, 2 * n_pairs)
    cos = jnp.repeat(jnp.cos(ang), 2, axis=1)
    sin = jnp.repeat(jnp.sin(ang), 2, axis=1) * jnp.tile(jnp.array([-1.0, 1.0], F32), dim // 2)
    return cos, sin


def _dedupe(patterns):
    keys, variant = {}, []
    for p in patterns:
        variant.append(keys.setdefault(p.tobytes(), len(keys)))
    first = [variant.index(v) for v in range(len(keys))]
    return np.asarray(variant, np.int32), first


def _window_tables(s):
    tq = 128
    nb = s // tq
    band = 3 * tq
    starts = np.clip(np.arange(nb) - 1, 0, nb - 3) * tq
    qpos = np.arange(nb)[:, None] * tq + np.arange(tq)[None, :]
    kpos = starts[:, None] + np.arange(band)[None, :]
    ok = np.abs(kpos[:, None, :] - qpos[:, :, None]) <= C_WINDOW
    variant, first = _dedupe(list(ok))
    bias = np.where(ok[first], 0.0, NEG_INF).astype(np.float32)
    return starts.astype(np.int32), variant, jnp.asarray(bias)


def _na_tables(s, rpb):
    tq = 128
    rows = s // GRID_W
    wr = min(NA_ROWS, rows)
    nb = s // tq
    band_blocks = 5
    band = band_blocks * tq
    starts = np.clip(np.arange(nb) - 2, 0, nb - band_blocks) * tq
    n_h = rpb.shape[0]
    n_dc = 2 * NA_COLS - 1
    period = GRID_W + n_dc - 1
    wrap = jnp.zeros((n_h, 2 * NA_ROWS - 1, period), F32)
    wrap = wrap.at[..., :NA_COLS].set(rpb[..., NA_COLS - 1:].astype(F32))
    wrap = wrap.at[..., period - (NA_COLS - 1):].set(rpb[..., :NA_COLS - 1].astype(F32))
    toep = jnp.tile(wrap, (1, 1, GRID_W))[..., :GRID_W * (period - 1)]
    toep = toep.reshape(n_h, 2 * NA_ROWS - 1, GRID_W, period - 1)[..., :GRID_W]
    qc = np.arange(GRID_W)[:, None]
    kc = np.arange(GRID_W)[None, :]
    cstart = np.clip(qc - NA_COLS // 2, 0, GRID_W - NA_COLS)
    col_ok = (kc >= cstart) & (kc < cstart + NA_COLS)
    toep = jnp.where(jnp.asarray(col_ok), toep * LOG2E, NEG_INF)
    masked = jnp.full((n_h, GRID_W, GRID_W), NEG_INF, F32)
    pats = []
    for j in range(nb):
        pat = np.full((tq // GRID_W, band // GRID_W), -1, np.int32)
        for a in range(tq // GRID_W):
            qr = j * (tq // GRID_W) + a
            rs = min(max(qr - wr // 2, 0), rows - wr)
            for w in range(band // GRID_W):
                kr = starts[j] // GRID_W + w
                if rs <= kr < rs + wr:
                    pat[a, w] = kr - qr + NA_ROWS - 1
        pats.append(pat)
    variant, first = _dedupe(pats)
    tiles = []
    for j in first:
        rows_ = [jnp.concatenate([toep[:, dr] if dr >= 0 else masked for dr in pats[j][a]], axis=-1)
                 for a in range(tq // GRID_W)]
        tiles.append(jnp.concatenate(rows_, axis=-2))
    bias = jnp.stack(tiles, axis=1)
    return starts.astype(np.int32), variant, bias


def kernel(x, c, ctx, c_ctx, l0_norm1_g, l0_norm2_g, l0_w_ada, l0_b_ada, l0_w_in, l0_w_out, l0_lam_q1, l0_lam_k1, l0_lam_q2, l0_lam_k2, l0_subln_g, l0_q_norm_g, l0_k_norm_g, l0_ffn_w_gate, l0_ffn_w_up, l0_ffn_w_down, l1_norm1_g, l1_norm2_g, l1_w_ada, l1_b_ada, l1_w_in, l1_w_out, l1_sinks, l1_rpb, l1_router, l1_exp_w_gate, l1_exp_w_up, l1_exp_w_down, final_norm_g):
    b, s, d = x.shape
    n_ctx = ctx.shape[1]
    n_exp = l1_router.shape[1]
    n = b * s
    assert d == 2048 and s % 512 == 0 and n_ctx == 256 and s // GRID_W >= 10

    x2d = x.reshape(n, d)
    xc2d = ctx.reshape(b * n_ctx, d)

    cvecs = jnp.zeros((8, d), F32).at[:b].set(c).at[b].set(c_ctx)
    mod0 = _ada(cvecs, l0_w_ada, l0_b_ada)
    mod1 = _ada(cvecs, l1_w_ada, l1_b_ada)

    def lat(mod, k):
        return mod[:b, k * d:(k + 1) * d].reshape(b, 1, d)

    def cx(mod, k):
        return jnp.broadcast_to(mod[b, k * d:(k + 1) * d].reshape(1, 1, d), (b, 1, d))

    cos_a, sin_a = _rope_tables(s, A_SUB)
    cos_b, sin_b = _rope_tables(s, HEAD_DIM)
    cos_tab = jnp.stack([jnp.tile(cos_a, (1, 2)), cos_b])
    sin_tab = jnp.stack([jnp.tile(sin_a, (1, 2)), sin_b])
    tm = 512

    qa = T_ROPE | T_SCALE_A
    qb = T_ROPE | T_QNORM | T_SCALE_B
    kb = T_ROPE | T_KNORM
    types0 = np.array([qa] * 8 + [qb] * 8 + [T_ROPE] * 8 + [0] * 8 + [kb] * 2 + [0] * 2, np.int32)
    tabsel0 = np.array([0, 0, 1, 1, 0, 0, 0, 0, 1], np.int32)
    qkg0 = jnp.stack([l0_q_norm_g, l0_k_norm_g]).astype(F32)
    w_in0 = l0_w_in.astype(BF16)
    p0 = _inproj(x2d, s, l0_norm1_g, lat(mod0, 1), lat(mod0, 0), w_in0, cos_tab, sin_tab, qkg0,
                 jnp.asarray(types0), jnp.asarray(tabsel0), tm)
    p0c = _inproj(xc2d, n_ctx, l0_norm1_g, cx(mod0, 1), cx(mod0, 0), w_in0, cos_tab, sin_tab, qkg0,
                  jnp.asarray(types0 & ~T_ROPE), jnp.asarray(tabsel0), n_ctx)
    lam = jnp.stack([l0_lam_q1, l0_lam_k1, l0_lam_q2, l0_lam_k2]).astype(F32)
    subg = l0_subln_g.reshape(1, HEAD_DIM).astype(F32)
    tk = min(2048, s)
    oa = _flash("diff", p0, p0, p0c, b, 512, tk, 0, 16, 24, lam, subg)
    ob = _flash("gqa", p0, p0, p0c, b, 256, tk, 8, 32, 34)
    oac = _flash("diff", p0c, p0c, None, b, n_ctx, n_ctx, 0, 16, 24, lam, subg)
    obc = _flash("gqa", p0c, p0c, None, b, n_ctx, n_ctx, 8, 32, 34)
    w_out0 = l0_w_out.astype(BF16)
    x1 = _outproj(oa, ob, w_out0, x2d, lat(mod0, 2), s, tm)
    xc1 = _outproj(oac, obc, w_out0, xc2d, cx(mod0, 2), n_ctx, n_ctx)
    wg0, wu0, wd0 = l0_ffn_w_gate.astype(BF16), l0_ffn_w_up.astype(BF16), l0_ffn_w_down.astype(BF16)
    x2 = _ffn(x1, s, l0_norm2_g, lat(mod0, 4), lat(mod0, 3), lat(mod0, 5), wg0, wu0, wd0, tm)
    xc2 = _ffn(xc1, n_ctx, l0_norm2_g, cx(mod0, 4), cx(mod0, 3), cx(mod0, 5), wg0, wu0, wd0, n_ctx)

    types1 = np.array([T_ROPE | T_SCALE_B] * 8 + [T_SCALE_B] * 8 + [T_ROPE] * 2 + [0] * 2 + [0] * 16, np.int32)
    tabsel1 = np.array([1] * 9, np.int32)
    w_in1 = l1_w_in.astype(BF16)
    qkg1 = jnp.ones((2, HEAD_DIM), F32)
    p1 = _inproj(x2, s, l1_norm1_g, lat(mod1, 1), lat(mod1, 0), w_in1, cos_tab, sin_tab, qkg1,
                 jnp.asarray(types1), jnp.asarray(tabsel1), tm)
    l1_q = 16 * HEAD_DIM
    n_kv_tiles = (w_in1.shape[1] - l1_q) // 512
    p1c = _inproj(xc2, n_ctx, l1_norm1_g, cx(mod1, 1), cx(mod1, 0), w_in1[:, l1_q:], cos_tab, sin_tab, qkg1,
                  jnp.zeros((n_kv_tiles * 4,), jnp.int32), jnp.zeros((n_kv_tiles,), jnp.int32), n_ctx)
    w_st, w_var, w_bias = _window_tables(s)
    n_st, n_var, n_bias = _na_tables(s, l1_rpb)
    sinks = l1_sinks.astype(F32)
    oc = _banded("window", p1, p1c, b, jnp.asarray(w_st), jnp.asarray(w_var), w_bias, sinks, 0, 16, 18, 0, 2)
    od = _banded("na", p1, p1c, b, jnp.asarray(n_st), jnp.asarray(n_var), n_bias, sinks, 8, 20, 28, 4, 12)
    x3 = _outproj(oc, od, l1_w_out.astype(BF16), x2, lat(mod1, 2), s, tm)

    wr_pad = jnp.zeros((d, LANES), F32).at[:, :n_exp].set(l1_router.astype(F32))
    h2, route_i, route_g = _router(x3, s, l1_norm2_g, lat(mod1, 4), lat(mod1, 3), wr_pad, n_exp, tm)
    e_flat = route_i[:, :TOP_K].reshape(-1)
    onehot = (e_flat[:, None] == jnp.arange(n_exp, dtype=jnp.int32)[None, :]).astype(jnp.int32)
    csum = jnp.cumsum(onehot, axis=0)
    rank = jnp.sum((csum - onehot) * onehot, axis=1)
    counts = csum[-1]
    padded = (counts + tm - 1) // tm * tm
    pad_end = jnp.cumsum(padded)
    pad_start = pad_end - padded
    slot = (pad_start[e_flat] + rank).astype(jnp.int32)
    n_blocks = (n * TOP_K + n_exp * (tm - 1) + tm - 1) // tm
    cap = n_blocks * tm
    tok_flat = jnp.repeat(jnp.arange(n, dtype=jnp.int32), TOP_K)
    slot_tok = jnp.full((cap,), n, jnp.int32).at[slot].set(tok_flat)
    block_expert = jnp.minimum(jnp.searchsorted(pad_end, jnp.arange(n_blocks, dtype=jnp.int32) * tm, side="right"),
                               n_exp - 1).astype(jnp.int32)
    n_used = (pad_end[-1] // tm).astype(jnp.int32).reshape(1)

    xs = _dispatch(slot_tok, h2, 256)
    y = _moe(xs, block_expert, n_used, l1_exp_w_gate.astype(BF16), l1_exp_w_up.astype(BF16),
             l1_exp_w_down.astype(BF16), tm)
    out = _combine(slot, x3, s, lat(mod1, 5), route_g, final_norm_g, y, 256)
    return out.reshape(b, s, d)
```

```python
import functools
import math

import numpy as np
import jax
import jax.numpy as jnp
from jax import lax
from jax.experimental import pallas as pl
from jax.experimental.pallas import tpu as pltpu

F32 = jnp.float32
BF16 = jnp.bfloat16

HEAD_DIM = 128
LANES = 128
GRID_W = 64
ROPE_BASE = 10000.0
EPS = 1e-6
NEG_INF = -1e30
A_SUB = HEAD_DIM // 2
A_SCALE = A_SUB ** -0.5
ATT_SCALE = HEAD_DIM ** -0.5
LAMBDA_INIT_L0 = 0.8 - 0.6 * math.exp(-0.3 * 0)
LOG2E = 1.4426950408889634
C_WINDOW = 128
NA_ROWS = 8
NA_COLS = 16
TOP_K = 2
VMEM_LIMIT = 56 * 1024 * 1024
T_ROPE, T_QNORM, T_KNORM, T_SCALE_B, T_SCALE_A = 1, 2, 4, 8, 16


def _cparams(sem):
    return pltpu.CompilerParams(dimension_semantics=sem, vmem_limit_bytes=VMEM_LIMIT)


def _modulate(xf, g, scale, shift):
    ms = jnp.mean(xf * xf, axis=-1, keepdims=True)
    y = xf * lax.rsqrt(ms + EPS) * g
    return y * (1.0 + scale) + shift


def _ada_kernel(c_ref, w_ref, b_ref, o_ref):
    c = c_ref[...]
    a = c * (1.0 / (1.0 + jnp.exp(-c)))
    o_ref[...] = jnp.dot(a.astype(BF16), w_ref[...].astype(BF16), preferred_element_type=F32) + b_ref[...]


def _ada(cvecs, w_ada, b_ada):
    r, d = cvecs.shape
    n = w_ada.shape[1]
    tn = 1024
    return pl.pallas_call(
        _ada_kernel,
        out_shape=jax.ShapeDtypeStruct((r, n), F32),
        grid=(n // tn,),
        in_specs=[pl.BlockSpec((r, d), lambda j: (0, 0)),
                  pl.BlockSpec((d, tn), lambda j: (0, j)),
                  pl.BlockSpec((1, tn), lambda j: (0, j))],
        out_specs=pl.BlockSpec((r, tn), lambda j: (0, j)),
        compiler_params=_cparams(("parallel",)),
        name="ada",
    )(cvecs, w_ada, b_ada.reshape(1, n))


def _inproj_kernel(types_ref, tab_ref, x_ref, g_ref, sc_ref, sh_ref, w_ref, cos_ref, sin_ref, qkg_ref,
                   o_ref, h_scr, *, heads_per_tile):
    del tab_ref
    j = pl.program_id(1)

    @pl.when(j == 0)
    def _():
        h_scr[...] = _modulate(x_ref[...], g_ref[...], sc_ref[...], sh_ref[...]).astype(BF16)

    acc = jnp.dot(h_scr[...], w_ref[...], preferred_element_type=F32)

    def rope(y):
        return y * cos_ref[...] + pltpu.roll(y, HEAD_DIM // 2, 1) * sin_ref[...]

    for hh in range(heads_per_tile):
        t = types_ref[j * heads_per_tile + hh]
        sl = slice(hh * HEAD_DIM, (hh + 1) * HEAD_DIM)
        y = acc[:, sl]
        has_norm = (t & (T_QNORM | T_KNORM)) != 0
        has_rope = (t & T_ROPE) != 0
        f = jnp.where((t & T_SCALE_B) != 0, ATT_SCALE * LOG2E,
                      jnp.where((t & T_SCALE_A) != 0, A_SCALE * LOG2E, 1.0)).astype(F32)

        @pl.when(t == 0)
        def _():
            o_ref[:, sl] = y.astype(o_ref.dtype)

        @pl.when(jnp.logical_and(t != 0, jnp.logical_not(jnp.logical_or(has_norm, has_rope))))
        def _():
            o_ref[:, sl] = (y * f).astype(o_ref.dtype)

        @pl.when(jnp.logical_and(has_rope, jnp.logical_not(has_norm)))
        def _():
            o_ref[:, sl] = (rope(y) * f).astype(o_ref.dtype)

        @pl.when(has_norm)
        def _():
            ms = jnp.mean(y * y, axis=-1, keepdims=True)
            gain = jnp.where((t & T_QNORM) != 0, qkg_ref[0:1, :], qkg_ref[1:2, :])
            yn = y * lax.rsqrt(ms + EPS) * gain
            o_ref[:, sl] = (jnp.where(has_rope, rope(yn), yn) * f).astype(o_ref.dtype)


def _inproj(x2d, rows_per_batch, g, scale, shift, w_bf, cos_tab, sin_tab, qkg, types, tabsel, tm):
    m, d = x2d.shape
    n = w_bf.shape[1]
    tn = 512
    hpt = tn // HEAD_DIM
    tiles_per_batch = rows_per_batch // tm
    tab_rows = cos_tab.shape[1] // tm
    kern = functools.partial(_inproj_kernel, heads_per_tile=hpt)
    grid_spec = pltpu.PrefetchScalarGridSpec(
        num_scalar_prefetch=2,
        grid=(m // tm, n // tn),
        in_specs=[
            pl.BlockSpec((tm, d), lambda i, j, ty, tb: (i, 0)),
            pl.BlockSpec((1, d), lambda i, j, ty, tb: (0, 0)),
            pl.BlockSpec((None, 1, d), lambda i, j, ty, tb: (i // tiles_per_batch, 0, 0)),
            pl.BlockSpec((None, 1, d), lambda i, j, ty, tb: (i // tiles_per_batch, 0, 0)),
            pl.BlockSpec((d, tn), lambda i, j, ty, tb: (0, j)),
            pl.BlockSpec((None, tm, LANES), lambda i, j, ty, tb: (tb[j], i % tab_rows, 0)),
            pl.BlockSpec((None, tm, LANES), lambda i, j, ty, tb: (tb[j], i % tab_rows, 0)),
            pl.BlockSpec((2, HEAD_DIM), lambda i, j, ty, tb: (0, 0)),
        ],
        out_specs=pl.BlockSpec((tm, tn), lambda i, j, ty, tb: (i, j)),
        scratch_shapes=[pltpu.VMEM((tm, d), BF16)],
    )
    return pl.pallas_call(
        kern,
        out_shape=jax.ShapeDtypeStruct((m, n), BF16),
        grid_spec=grid_spec,
        compiler_params=_cparams(("parallel", "arbitrary")),
        name="inproj",
    )(types, tabsel, x2d, g.reshape(1, d), scale, shift, w_bf, cos_tab, sin_tab, qkg)


def _flash_kernel(*refs, mode, tq, has_ctx, nk):
    refs = list(refs)
    q_ref, k_ref, v_ref = refs[:3]
    pos = 3
    if has_ctx:
        kc_ref, vc_ref = refs[3:5]
        pos = 5
    if mode == "diff":
        lam_ref, subg_ref = refs[pos:pos + 2]
        pos += 2
    o_ref, qs, m_s, l_s, acc_s = refs[pos:pos + 5]
    ki = pl.program_id(3)

    @pl.when(ki == 0)
    def _():
        q = q_ref[...]
        if mode == "diff":
            lane = lax.broadcasted_iota(jnp.int32, (1, LANES), 1)
            zero = jnp.zeros_like(q)
            first = (lane & (A_SUB // 2)) == 0
            qs[0:tq, :] = jnp.where(first, q, zero)
            qs[tq:2 * tq, :] = jnp.where(first, zero, q)
        else:
            for gi in range(4):
                qs[gi * tq:(gi + 1) * tq, :] = q[:, gi * HEAD_DIM:(gi + 1) * HEAD_DIM]
        m_s[...] = jnp.full_like(m_s, NEG_INF)
        l_s[...] = jnp.zeros_like(l_s)
        acc_s[...] = jnp.zeros_like(acc_s)

    def step(k, v):
        s = lax.dot_general(qs[...], k, (((1,), (1,)), ((), ())), preferred_element_type=F32)
        slabs = [s[:, c * LANES:(c + 1) * LANES] for c in range(k.shape[0] // LANES)]
        mx = functools.reduce(jnp.maximum, slabs)
        m_prev = m_s[...]
        m_new = jnp.maximum(m_prev, jnp.max(mx, axis=-1, keepdims=True))
        alpha = jnp.exp2(m_prev - m_new)
        ps = [jnp.exp2(sl - m_new) for sl in slabs]
        l_s[...] = alpha * l_s[...] + functools.reduce(jnp.add, ps)
        p = jnp.concatenate([x.astype(BF16) for x in ps], axis=1)
        acc_s[...] = alpha * acc_s[...] + jnp.dot(p, v, preferred_element_type=F32)
        m_s[...] = m_new

    step(k_ref[...], v_ref[...])

    @pl.when(ki == nk - 1)
    def _():
        if has_ctx:
            step(kc_ref[...], vc_ref[...])
        o = acc_s[...] / jnp.sum(l_s[...], axis=-1, keepdims=True)
        if mode == "diff":
            lam1 = jnp.sum(lam_ref[0:1, :] * lam_ref[1:2, :], axis=-1, keepdims=True)
            lam2 = jnp.sum(lam_ref[2:3, :] * lam_ref[3:4, :], axis=-1, keepdims=True)
            lam = jnp.exp(lam1) - jnp.exp(lam2) + LAMBDA_INIT_L0
            od = o[0:tq, :] - lam * o[tq:2 * tq, :]
            ms = jnp.mean(od * od, axis=-1, keepdims=True)
            on = od * lax.rsqrt(ms + EPS) * subg_ref[...]
            o_ref[...] = (on * (1.0 - LAMBDA_INIT_L0)).astype(o_ref.dtype)
        else:
            for gi in range(4):
                o_ref[:, gi * HEAD_DIM:(gi + 1) * HEAD_DIM] = o[gi * tq:(gi + 1) * tq, :].astype(o_ref.dtype)


def _flash(mode, q_arr, kv_arr, ctx_arr, batch, tq, tk, qcol, kcol, vcol, lam=None, subg=None):
    mq = q_arr.shape[0]
    sq = mq // batch
    sk = kv_arr.shape[0] // batch
    nq, nk = sq // tq, sk // tk
    has_ctx = ctx_arr is not None
    if mode == "diff":
        nh, qw, reps = 8, HEAD_DIM, 2
        q_spec = pl.BlockSpec((tq, qw), lambda b, h, qi, ki: (b * nq + qi, qcol + h))
        o_spec = pl.BlockSpec((tq, qw), lambda b, h, qi, ki: (b * nq + qi, h))
    else:
        nh, qw, reps = 2, 4 * HEAD_DIM, 4
        q_spec = pl.BlockSpec((tq, qw), lambda b, h, qi, ki: (b * nq + qi, qcol // 4 + h))
        o_spec = pl.BlockSpec((tq, qw), lambda b, h, qi, ki: (b * nq + qi, h))
    in_specs = [q_spec,
                pl.BlockSpec((tk, HEAD_DIM), lambda b, h, qi, ki: (b * nk + ki, kcol + h)),
                pl.BlockSpec((tk, HEAD_DIM), lambda b, h, qi, ki: (b * nk + ki, vcol + h))]
    args = [q_arr, kv_arr, kv_arr]
    if has_ctx:
        nc = ctx_arr.shape[0] // batch
        in_specs += [pl.BlockSpec((nc, HEAD_DIM), lambda b, h, qi, ki: (b, kcol + h)),
                     pl.BlockSpec((nc, HEAD_DIM), lambda b, h, qi, ki: (b, vcol + h))]
        args += [ctx_arr, ctx_arr]
    if mode == "diff":
        in_specs += [pl.BlockSpec((4, A_SUB), lambda b, h, qi, ki: (0, 0)),
                     pl.BlockSpec((1, HEAD_DIM), lambda b, h, qi, ki: (0, 0))]
        args += [lam, subg]
    r = reps * tq
    kern = functools.partial(_flash_kernel, mode=mode, tq=tq, has_ctx=has_ctx, nk=nk)
    return pl.pallas_call(
        kern,
        out_shape=jax.ShapeDtypeStruct((mq, 8 * HEAD_DIM), BF16),
        grid=(batch, nh, nq, nk),
        in_specs=in_specs,
        out_specs=o_spec,
        scratch_shapes=[pltpu.VMEM((r, HEAD_DIM), BF16), pltpu.VMEM((r, LANES), F32),
                        pltpu.VMEM((r, LANES), F32), pltpu.VMEM((r, HEAD_DIM), F32)],
        compiler_params=_cparams(("parallel", "parallel", "parallel", "arbitrary")),
        name="flash_" + mode,
    )(*args)


def _banded_kernel(start_ref, var_ref, q_ref, k_ref, v_ref, kc_ref, vc_ref, bias_ref, sink_ref, o_ref,
                   *, mode, tq, band):
    del var_ref
    hg = pl.program_id(1)
    j = pl.program_id(2)
    start = pl.multiple_of(start_ref[j], LANES)
    kb = k_ref[pl.ds(start, band), :]
    vb = v_ref[pl.ds(start, band), :]
    kc = kc_ref[...]
    vc = vc_ref[...]
    q = q_ref[...]
    dn = (((1,), (1,)), ((), ()))
    heads = [slice(gi * HEAD_DIM, (gi + 1) * HEAD_DIM) for gi in range(4)]
    if mode == "window":
        qs = jnp.concatenate([q[:, hd] for hd in heads], axis=0)
        s_loc = lax.dot_general(qs, kb, dn, preferred_element_type=F32)
        s_ctx = lax.dot_general(qs, kc, dn, preferred_element_type=F32)
        bias = jnp.concatenate([bias_ref[...]] * 4, axis=0)
    else:
        s_loc = jnp.concatenate([lax.dot_general(q[:, hd], kb[:, hd], dn, preferred_element_type=F32)
                                 for hd in heads], axis=0)
        s_ctx = jnp.concatenate([lax.dot_general(q[:, hd], kc[:, hd], dn, preferred_element_type=F32)
                                 for hd in heads], axis=0)
        bias = bias_ref[...]
    s_loc = s_loc + bias
    n_loc = band // LANES
    slabs = ([s_loc[:, c * LANES:(c + 1) * LANES] for c in range(n_loc)]
             + [s_ctx[:, c * LANES:(c + 1) * LANES] for c in range(s_ctx.shape[1] // LANES)])
    mx = functools.reduce(jnp.maximum, slabs)
    m = jnp.broadcast_to(jnp.max(mx, axis=-1, keepdims=True), mx.shape)
    if mode == "window":
        sink = jnp.concatenate([jnp.full((tq, LANES), sink_ref[hg * 4 + gi] * LOG2E, F32) for gi in range(4)], axis=0)
        m = jnp.maximum(m, sink)
    ps = [jnp.exp2(sl - m) for sl in slabs]
    l = jnp.sum(functools.reduce(jnp.add, ps), axis=-1, keepdims=True)
    if mode == "window":
        l = l + jnp.exp2(sink - m)[:, 0:1]
    p_loc = jnp.concatenate([x.astype(BF16) for x in ps[:n_loc]], axis=1)
    p_ctx = jnp.concatenate([x.astype(BF16) for x in ps[n_loc:]], axis=1)
    if mode == "window":
        o = jnp.dot(p_loc, vb, preferred_element_type=F32) + jnp.dot(p_ctx, vc, preferred_element_type=F32)
        o = (o / l).astype(o_ref.dtype)
        for gi in range(4):
            o_ref[:, heads[gi]] = o[gi * tq:(gi + 1) * tq, :]
    else:
        for gi in range(4):
            rows = slice(gi * tq, (gi + 1) * tq)
            o = (jnp.dot(p_loc[rows, :], vb[:, heads[gi]], preferred_element_type=F32)
                 + jnp.dot(p_ctx[rows, :], vc[:, heads[gi]], preferred_element_type=F32))
            o_ref[:, heads[gi]] = (o / l[rows, :]).astype(o_ref.dtype)


def _banded(mode, proj, proj_ctx, batch, starts, variants, bias, sinks, qcol, kcol, vcol, kc_col, vc_col):
    m = proj.shape[0]
    s = m // batch
    tq = 128
    nq = s // tq
    nc = proj_ctx.shape[0] // batch
    band = bias.shape[-1]
    qw = 4 * HEAD_DIM
    if mode == "window":
        kw, div = HEAD_DIM, 1
        bias_spec = pl.BlockSpec((None, tq, band), lambda b, h, j, st, va: (va[j], 0, 0))
    else:
        kw, div = qw, 4
        bias_spec = pl.BlockSpec((None, None, 4 * tq, band), lambda b, h, j, st, va: (h, va[j], 0, 0))
    grid_spec = pltpu.PrefetchScalarGridSpec(
        num_scalar_prefetch=2,
        grid=(batch, 2, nq),
        in_specs=[
            pl.BlockSpec((tq, qw), lambda b, h, j, st, va: (b * nq + j, qcol // 4 + h)),
            pl.BlockSpec((s, kw), lambda b, h, j, st, va: (b, kcol // div + h)),
            pl.BlockSpec((s, kw), lambda b, h, j, st, va: (b, vcol // div + h)),
            pl.BlockSpec((nc, kw), lambda b, h, j, st, va: (b, kc_col // div + h)),
            pl.BlockSpec((nc, kw), lambda b, h, j, st, va: (b, vc_col // div + h)),
            bias_spec,
            pl.BlockSpec(memory_space=pltpu.SMEM),
        ],
        out_specs=pl.BlockSpec((tq, qw), lambda b, h, j, st, va: (b * nq + j, h)),
    )
    kern = functools.partial(_banded_kernel, mode=mode, tq=tq, band=band)
    return pl.pallas_call(
        kern,
        out_shape=jax.ShapeDtypeStruct((m, 8 * HEAD_DIM), BF16),
        grid_spec=grid_spec,
        compiler_params=_cparams(("parallel", "parallel", "arbitrary")),
        name="banded_" + mode,
    )(starts, variants, proj, proj, proj, proj_ctx, proj_ctx, bias, sinks)


def _outproj_kernel(oa_ref, ob_ref, wa_ref, wb_ref, x_ref, gt_ref, o_ref):
    acc = jnp.dot(oa_ref[...], wa_ref[...], preferred_element_type=F32)
    acc = acc + jnp.dot(ob_ref[...], wb_ref[...], preferred_element_type=F32)
    o_ref[...] = x_ref[...] + gt_ref[...] * acc


def _outproj(oa, ob, w_bf, x2d, gate, rows_per_batch, tm):
    m, d = x2d.shape
    half = oa.shape[1]
    tiles_per_batch = rows_per_batch // tm
    return pl.pallas_call(
        _outproj_kernel,
        out_shape=jax.ShapeDtypeStruct((m, d), F32),
        grid=(m // tm,),
        in_specs=[pl.BlockSpec((tm, half), lambda i: (i, 0)),
                  pl.BlockSpec((tm, half), lambda i: (i, 0)),
                  pl.BlockSpec((half, d), lambda i: (0, 0)),
                  pl.BlockSpec((half, d), lambda i: (1, 0)),
                  pl.BlockSpec((tm, d), lambda i: (i, 0)),
                  pl.BlockSpec((None, 1, d), lambda i: (i // tiles_per_batch, 0, 0))],
        out_specs=pl.BlockSpec((tm, d), lambda i: (i, 0)),
        compiler_params=_cparams(("parallel",)),
        name="outproj",
    )(oa, ob, w_bf, w_bf, x2d, gate)


def _swiglu_step(h, wg, wu, wd):
    g = jnp.dot(h, wg, preferred_element_type=F32)
    u = jnp.dot(h, wu, preferred_element_type=F32)
    a = (g * (1.0 / (1.0 + jnp.exp(-g))) * u).astype(BF16)
    return jnp.dot(a, wd, preferred_element_type=F32)


def _ffn_kernel(x_ref, g_ref, sc_ref, sh_ref, gt_ref, wg_ref, wu_ref, wd_ref, o_ref, h_scr, acc_scr, *, nf):
    j = pl.program_id(1)

    @pl.when(j == 0)
    def _():
        h_scr[...] = _modulate(x_ref[...], g_ref[...], sc_ref[...], sh_ref[...]).astype(BF16)
        acc_scr[...] = jnp.zeros_like(acc_scr)

    acc_scr[...] += _swiglu_step(h_scr[...], wg_ref[...], wu_ref[...], wd_ref[...])

    @pl.when(j == nf - 1)
    def _():
        o_ref[...] = x_ref[...] + gt_ref[...] * acc_scr[...]


def _ffn(x2d, rows_per_batch, g, scale, shift, gate, wg, wu, wd, tm):
    m, d = x2d.shape
    f = wg.shape[1]
    tf = 512
    nf = f // tf
    tiles_per_batch = rows_per_batch // tm
    vec = pl.BlockSpec((None, 1, d), lambda i, j: (i // tiles_per_batch, 0, 0))
    return pl.pallas_call(
        functools.partial(_ffn_kernel, nf=nf),
        out_shape=jax.ShapeDtypeStruct((m, d), F32),
        grid=(m // tm, nf),
        in_specs=[pl.BlockSpec((tm, d), lambda i, j: (i, 0)),
                  pl.BlockSpec((1, d), lambda i, j: (0, 0)),
                  vec, vec, vec,
                  pl.BlockSpec((d, tf), lambda i, j: (0, j)),
                  pl.BlockSpec((d, tf), lambda i, j: (0, j)),
                  pl.BlockSpec((tf, d), lambda i, j: (j, 0))],
        out_specs=pl.BlockSpec((tm, d), lambda i, j: (i, 0)),
        scratch_shapes=[pltpu.VMEM((tm, d), BF16), pltpu.VMEM((tm, d), F32)],
        compiler_params=_cparams(("parallel", "arbitrary")),
        name="ffn",
    )(x2d, g.reshape(1, d), scale, shift, gate, wg, wu, wd)


def _router_kernel(x_ref, g_ref, sc_ref, sh_ref, wr_ref, h_ref, ri_ref, rg_ref, *, n_exp, n_tiles):
    h = _modulate(x_ref[...], g_ref[...], sc_ref[...], sh_ref[...])
    h_ref[...] = jnp.where(pl.program_id(0) < n_tiles, h, 0.0)
    logits = jnp.dot(h, wr_ref[...], preferred_element_type=F32, precision=lax.Precision.HIGHEST)
    lane = lax.broadcasted_iota(jnp.int32, logits.shape, 1)
    logits = jnp.where(lane < n_exp, logits, -jnp.inf)
    lane_f = lane.astype(F32)
    m1 = jnp.max(logits, axis=-1, keepdims=True)
    i1 = jnp.min(jnp.where(logits == m1, lane_f, float(LANES)), axis=-1, keepdims=True)
    rest = jnp.where(lane_f == i1, -jnp.inf, logits)
    m2 = jnp.max(rest, axis=-1, keepdims=True)
    i2 = jnp.min(jnp.where(rest == m2, lane_f, float(LANES)), axis=-1, keepdims=True)
    e = jnp.exp(m2 - m1)
    g1 = 1.0 / (1.0 + e)
    g2 = e / (1.0 + e)
    ri_ref[...] = jnp.where(lane == 0, i1, jnp.where(lane == 1, i2, 0.0)).astype(jnp.int32)
    rg_ref[...] = jnp.where(lane == 0, g1, jnp.where(lane == 1, g2, 0.0))


def _router(x2d, rows_per_batch, g, scale, shift, wr_pad, n_exp, tm):
    m, d = x2d.shape
    tiles_per_batch = rows_per_batch // tm
    n_tiles = m // tm
    last = n_tiles - 1
    vec = pl.BlockSpec((None, 1, d), lambda i: (jnp.minimum(i, last) // tiles_per_batch, 0, 0))
    return pl.pallas_call(
        functools.partial(_router_kernel, n_exp=n_exp, n_tiles=n_tiles),
        out_shape=(jax.ShapeDtypeStruct((m + tm, d), F32),
                   jax.ShapeDtypeStruct((m + tm, LANES), jnp.int32),
                   jax.ShapeDtypeStruct((m + tm, LANES), F32)),
        grid=(n_tiles + 1,),
        in_specs=[pl.BlockSpec((tm, d), lambda i: (jnp.minimum(i, last), 0)),
                  pl.BlockSpec((1, d), lambda i: (0, 0)),
                  vec, vec,
                  pl.BlockSpec((d, LANES), lambda i: (0, 0))],
        out_specs=(pl.BlockSpec((tm, d), lambda i: (i, 0)),
                   pl.BlockSpec((tm, LANES), lambda i: (i, 0)),
                   pl.BlockSpec((tm, LANES), lambda i: (i, 0))),
        compiler_params=_cparams(("parallel",)),
        name="router",
    )(x2d, g.reshape(1, d), scale, shift, wr_pad)


def _moe_kernel(be_ref, nu_ref, tok_ref, h_hbm, wg_ref, wu_ref, wd_ref, o_ref, xbuf, xb_scr, acc_scr, sem,
                *, nf, tm, nblk, per_step):
    del be_ref
    i = pl.program_id(0)
    j = pl.program_id(1)
    used = i < nu_ref[0]
    n_copies = nf * per_step

    def row_copy(blk, slot, r):
        tok = tok_ref[blk * tm + jnp.minimum(r, tm - 1)]
        return pltpu.make_async_copy(h_hbm.at[pl.ds(tok, 1)], xbuf.at[slot, pl.ds(r, 1)], sem)

    def wait_all(slot):
        def body(r, carry):
            row_copy(0, slot, r).wait()
            return carry
        lax.fori_loop(0, n_copies, body, 0)

    def start_next():
        nxt = jnp.minimum(i + 1, nblk - 1)
        for u in range(per_step):
            row_copy(nxt, (i + 1) % 2, j * per_step + u).start()

    @pl.when(j == 0)
    def _():
        @pl.when(i == 0)
        def _():
            def body(r, carry):
                row_copy(0, 0, r).start()
                return carry
            lax.fori_loop(0, n_copies, body, 0)

        wait_all(i % 2)

        @pl.when(used)
        def _():
            xb_scr[...] = xbuf[i % 2, 0:tm, :].astype(BF16)
            acc_scr[...] = jnp.zeros_like(acc_scr)

    @pl.when(used)
    def _():
        start_next()
        acc_scr[...] += _swiglu_step(xb_scr[...], wg_ref[...], wu_ref[...], wd_ref[...])

    @pl.when(jnp.logical_not(used))
    def _():
        start_next()

    @pl.when(j == nf - 1)
    def _():
        @pl.when(used)
        def _():
            o_ref[...] = acc_scr[...]

        @pl.when(jnp.logical_not(used))
        def _():
            o_ref[...] = jnp.zeros_like(o_ref)

        @pl.when(i == nblk - 1)
        def _():
            wait_all((i + 1) % 2)


def _moe(slot_tok, h_pad, block_expert, n_used, wg, wu, wd, tm):
    cap = slot_tok.shape[0]
    d = h_pad.shape[1]
    f = wg.shape[2]
    tf = 512
    nf = f // tf
    nblk = cap // tm
    per_step = -(-tm // nf)
    slot_rows = -(-nf * per_step // 8) * 8

    def col_j(i, j, nu):
        return jnp.where(i < nu[0], j, nf - 1)

    grid_spec = pltpu.PrefetchScalarGridSpec(
        num_scalar_prefetch=3,
        grid=(nblk, nf),
        in_specs=[pl.BlockSpec(memory_space=pl.ANY),
                  pl.BlockSpec((None, d, tf), lambda i, j, be, nu, tok: (be[i], 0, col_j(i, j, nu))),
                  pl.BlockSpec((None, d, tf), lambda i, j, be, nu, tok: (be[i], 0, col_j(i, j, nu))),
                  pl.BlockSpec((None, tf, d), lambda i, j, be, nu, tok: (be[i], col_j(i, j, nu), 0))],
        out_specs=pl.BlockSpec((tm, d), lambda i, j, be, nu, tok: (i, 0)),
        scratch_shapes=[pltpu.VMEM((2, slot_rows, d), F32), pltpu.VMEM((tm, d), BF16), pltpu.VMEM((tm, d), F32),
                        pltpu.SemaphoreType.DMA],
    )
    return pl.pallas_call(
        functools.partial(_moe_kernel, nf=nf, tm=tm, nblk=nblk, per_step=per_step),
        out_shape=jax.ShapeDtypeStruct((cap, d), F32),
        grid_spec=grid_spec,
        compiler_params=_cparams(("arbitrary", "arbitrary")),
        name="moe",
    )(block_expert, n_used, slot_tok, h_pad, wg, wu, wd)


def _combine_kernel(slot_ref, x_ref, gt_ref, rg_ref, fg_ref, y_hbm, o_ref, buf, sem, *, tc):
    base = pl.program_id(0) * tc

    def issue(r, carry):
        for k in range(TOP_K):
            s = slot_ref[TOP_K * (base + r) + k]
            pltpu.make_async_copy(y_hbm.at[pl.ds(s, 1)], buf.at[k, pl.ds(r, 1)], sem).start()
        return carry

    lax.fori_loop(0, tc, issue, 0)

    def drain(r, carry):
        for k in range(TOP_K):
            pltpu.make_async_copy(y_hbm.at[pl.ds(0, 1)], buf.at[k, pl.ds(r, 1)], sem).wait()
        return carry

    lax.fori_loop(0, tc, drain, 0)
    rg = rg_ref[...]
    y = rg[:, 0:1] * buf[0] + rg[:, 1:2] * buf[1]
    xo = x_ref[...] + gt_ref[...] * y
    ms = jnp.mean(xo * xo, axis=-1, keepdims=True)
    o_ref[...] = xo * lax.rsqrt(ms + EPS) * fg_ref[...]


def _combine(slot, x2d, rows_per_batch, gate, route_g, final_g, y, tc):
    m, d = x2d.shape
    tiles_per_batch = rows_per_batch // tc
    grid_spec = pltpu.PrefetchScalarGridSpec(
        num_scalar_prefetch=1,
        grid=(m // tc,),
        in_specs=[pl.BlockSpec((tc, d), lambda i, sl: (i, 0)),
                  pl.BlockSpec((None, 1, d), lambda i, sl: (i // tiles_per_batch, 0, 0)),
                  pl.BlockSpec((tc, LANES), lambda i, sl: (i, 0)),
                  pl.BlockSpec((1, d), lambda i, sl: (0, 0)),
                  pl.BlockSpec(memory_space=pl.ANY)],
        out_specs=pl.BlockSpec((tc, d), lambda i, sl: (i, 0)),
        scratch_shapes=[pltpu.VMEM((TOP_K, tc, d), F32), pltpu.SemaphoreType.DMA],
    )
    return pl.pallas_call(
        functools.partial(_combine_kernel, tc=tc),
        out_shape=jax.ShapeDtypeStruct((m, d), F32),
        grid_spec=grid_spec,
        compiler_params=_cparams(("arbitrary",)),
        name="combine",
    )(slot, x2d, gate, route_g, final_g.reshape(1, d), y)


def _rope_tables(n_tok, dim):
    t = jnp.arange(n_tok)
    pos = jnp.stack([t // GRID_W, t % GRID_W], axis=-1).astype(F32)
    n_pairs = dim // 4
    freq = ROPE_BASE ** (-jnp.arange(n_pairs, dtype=F32) / n_pairs)
    ang = (pos[:, :, None] * freq).reshape(n_tok, 2 * n_pairs)
    cos = jnp.tile(jnp.cos(ang), (1, HEAD_DIM // dim))
    sin = jnp.tile(jnp.sin(ang), (1, HEAD_DIM // dim))
    return jnp.concatenate([cos, cos], axis=1), jnp.concatenate([-sin, sin], axis=1)


def _deinterleave(w, comps):
    d, n = w.shape
    pairs = HEAD_DIM // (2 * comps)
    return w.reshape(d, n // HEAD_DIM, comps, pairs, 2).transpose(0, 1, 4, 2, 3).reshape(d, n)


def _dedupe(patterns):
    keys, variant = {}, []
    for p in patterns:
        variant.append(keys.setdefault(p.tobytes(), len(keys)))
    first = [variant.index(v) for v in range(len(keys))]
    return np.asarray(variant, np.int32), first


def _window_tables(s):
    tq = 128
    nb = s // tq
    band = 3 * tq
    starts = np.clip(np.arange(nb) - 1, 0, nb - 3) * tq
    qpos = np.arange(nb)[:, None] * tq + np.arange(tq)[None, :]
    kpos = starts[:, None] + np.arange(band)[None, :]
    ok = np.abs(kpos[:, None, :] - qpos[:, :, None]) <= C_WINDOW
    variant, first = _dedupe(list(ok))
    bias = np.where(ok[first], 0.0, NEG_INF).astype(np.float32)
    return starts.astype(np.int32), variant, jnp.asarray(bias)


def _na_tables(s, rpb):
    tq = 128
    rows = s // GRID_W
    wr = min(NA_ROWS, rows)
    nb = s // tq
    band_blocks = 5
    band = band_blocks * tq
    starts = np.clip(np.arange(nb) - 2, 0, nb - band_blocks) * tq
    n_h = rpb.shape[0]
    n_dc = 2 * NA_COLS - 1
    period = GRID_W + n_dc - 1
    wrap = jnp.zeros((n_h, 2 * NA_ROWS - 1, period), F32)
    wrap = wrap.at[..., :NA_COLS].set(rpb[..., NA_COLS - 1:].astype(F32))
    wrap = wrap.at[..., period - (NA_COLS - 1):].set(rpb[..., :NA_COLS - 1].astype(F32))
    toep = jnp.tile(wrap, (1, 1, GRID_W))[..., :GRID_W * (period - 1)]
    toep = toep.reshape(n_h, 2 * NA_ROWS - 1, GRID_W, period - 1)[..., :GRID_W]
    qc = np.arange(GRID_W)[:, None]
    kc = np.arange(GRID_W)[None, :]
    cstart = np.clip(qc - NA_COLS // 2, 0, GRID_W - NA_COLS)
    col_ok = (kc >= cstart) & (kc < cstart + NA_COLS)
    toep = jnp.where(jnp.asarray(col_ok), toep * LOG2E, NEG_INF)
    masked = jnp.full((n_h, GRID_W, GRID_W), NEG_INF, F32)
    pats = []
    for j in range(nb):
        pat = np.full((tq // GRID_W, band // GRID_W), -1, np.int32)
        for a in range(tq // GRID_W):
            qr = j * (tq // GRID_W) + a
            rs = min(max(qr - wr // 2, 0), rows - wr)
            for w in range(band // GRID_W):
                kr = starts[j] // GRID_W + w
                if rs <= kr < rs + wr:
                    pat[a, w] = kr - qr + NA_ROWS - 1
        pats.append(pat)
    variant, first = _dedupe(pats)
    tiles = []
    for j in first:
        rows_ = [jnp.concatenate([toep[:, dr] if dr >= 0 else masked for dr in pats[j][a]], axis=-1)
                 for a in range(tq // GRID_W)]
        tiles.append(jnp.concatenate(rows_, axis=-2))
    bias = jnp.stack(tiles, axis=1)
    bias = bias.reshape(n_h // 4, 4, len(first), tq, band).transpose(0, 2, 1, 3, 4)
    return starts.astype(np.int32), variant, bias.reshape(n_h // 4, len(first), 4 * tq, band)


def kernel(x, c, ctx, c_ctx, l0_norm1_g, l0_norm2_g, l0_w_ada, l0_b_ada, l0_w_in, l0_w_out, l0_lam_q1, l0_lam_k1, l0_lam_q2, l0_lam_k2, l0_subln_g, l0_q_norm_g, l0_k_norm_g, l0_ffn_w_gate, l0_ffn_w_up, l0_ffn_w_down, l1_norm1_g, l1_norm2_g, l1_w_ada, l1_b_ada, l1_w_in, l1_w_out, l1_sinks, l1_rpb, l1_router, l1_exp_w_gate, l1_exp_w_up, l1_exp_w_down, final_norm_g):
    b, s, d = x.shape
    n_ctx = ctx.shape[1]
    n_exp = l1_router.shape[1]
    n = b * s
    assert d == 2048 and s % 512 == 0 and n_ctx == 256 and s // GRID_W >= 10

    x2d = x.reshape(n, d)
    xc2d = ctx.reshape(b * n_ctx, d)

    cvecs = jnp.zeros((8, d), F32).at[:b].set(c).at[b].set(c_ctx)
    mod0 = _ada(cvecs, l0_w_ada, l0_b_ada)
    mod1 = _ada(cvecs, l1_w_ada, l1_b_ada)

    def lat(mod, k):
        return mod[:b, k * d:(k + 1) * d].reshape(b, 1, d)

    def cx(mod, k):
        return jnp.broadcast_to(mod[b, k * d:(k + 1) * d].reshape(1, 1, d), (b, 1, d))

    cos_a, sin_a = _rope_tables(s, A_SUB)
    cos_b, sin_b = _rope_tables(s, HEAD_DIM)
    cos_tab = jnp.stack([cos_a, cos_b])
    sin_tab = jnp.stack([sin_a, sin_b])
    tm = 512
    hq = 8 * HEAD_DIM

    qa = T_ROPE | T_SCALE_A
    qb = T_ROPE | T_QNORM | T_SCALE_B
    kb = T_ROPE | T_KNORM
    types0 = np.array([qa] * 8 + [qb] * 8 + [T_ROPE] * 8 + [0] * 8 + [kb] * 2 + [0] * 2, np.int32)
    tabsel0 = np.array([0, 0, 1, 1, 0, 0, 0, 0, 1], np.int32)
    qkg0 = _deinterleave(jnp.stack([l0_q_norm_g, l0_k_norm_g]).astype(F32), 1)
    w0 = l0_w_in.astype(BF16)
    w_in0 = jnp.concatenate([_deinterleave(w0[:, :hq], 2), _deinterleave(w0[:, hq:2 * hq], 1),
                             _deinterleave(w0[:, 2 * hq:3 * hq], 2), w0[:, 3 * hq:4 * hq],
                             _deinterleave(w0[:, 4 * hq:4 * hq + 2 * HEAD_DIM], 1),
                             w0[:, 4 * hq + 2 * HEAD_DIM:]], axis=1)
    p0 = _inproj(x2d, s, l0_norm1_g, lat(mod0, 1), lat(mod0, 0), w_in0, cos_tab, sin_tab, qkg0,
                 jnp.asarray(types0), jnp.asarray(tabsel0), tm)
    p0c = _inproj(xc2d, n_ctx, l0_norm1_g, cx(mod0, 1), cx(mod0, 0), w_in0, cos_tab, sin_tab, qkg0,
                  jnp.asarray(types0 & ~T_ROPE), jnp.asarray(tabsel0), n_ctx)
    lam = jnp.stack([l0_lam_q1, l0_lam_k1, l0_lam_q2, l0_lam_k2]).astype(F32)
    subg = l0_subln_g.reshape(1, HEAD_DIM).astype(F32)
    tk = min(2048, s)
    oa = _flash("diff", p0, p0, p0c, b, 512, tk, 0, 16, 24, lam, subg)
    ob = _flash("gqa", p0, p0, p0c, b, 256, tk, 8, 32, 34)
    oac = _flash("diff", p0c, p0c, None, b, n_ctx, n_ctx, 0, 16, 24, lam, subg)
    obc = _flash("gqa", p0c, p0c, None, b, n_ctx, n_ctx, 8, 32, 34)
    w_out0 = l0_w_out.astype(BF16)
    x1 = _outproj(oa, ob, w_out0, x2d, lat(mod0, 2), s, tm)
    xc1 = _outproj(oac, obc, w_out0, xc2d, cx(mod0, 2), n_ctx, n_ctx)
    wg0, wu0, wd0 = l0_ffn_w_gate.astype(BF16), l0_ffn_w_up.astype(BF16), l0_ffn_w_down.astype(BF16)
    x2 = _ffn(x1, s, l0_norm2_g, lat(mod0, 4), lat(mod0, 3), lat(mod0, 5), wg0, wu0, wd0, tm)
    xc2 = _ffn(xc1, n_ctx, l0_norm2_g, cx(mod0, 4), cx(mod0, 3), cx(mod0, 5), wg0, wu0, wd0, n_ctx)

    types1 = np.array([T_ROPE | T_SCALE_B] * 8 + [T_SCALE_B] * 8 + [T_ROPE] * 2 + [0] * 2 + [0] * 16, np.int32)
    tabsel1 = np.array([1] * 9, np.int32)
    w1 = l1_w_in.astype(BF16)
    w_in1 = jnp.concatenate([_deinterleave(w1[:, :hq], 1), w1[:, hq:2 * hq],
                             _deinterleave(w1[:, 2 * hq:2 * hq + 2 * HEAD_DIM], 1),
                             w1[:, 2 * hq + 2 * HEAD_DIM:]], axis=1)
    qkg1 = jnp.ones((2, HEAD_DIM), F32)
    p1 = _inproj(x2, s, l1_norm1_g, lat(mod1, 1), lat(mod1, 0), w_in1, cos_tab, sin_tab, qkg1,
                 jnp.asarray(types1), jnp.asarray(tabsel1), tm)
    l1_q = 16 * HEAD_DIM
    n_kv_tiles = (w_in1.shape[1] - l1_q) // 512
    p1c = _inproj(xc2, n_ctx, l1_norm1_g, cx(mod1, 1), cx(mod1, 0), w_in1[:, l1_q:], cos_tab, sin_tab, qkg1,
                  jnp.zeros((n_kv_tiles * 4,), jnp.int32), jnp.zeros((n_kv_tiles,), jnp.int32), n_ctx)
    w_st, w_var, w_bias = _window_tables(s)
    n_st, n_var, n_bias = _na_tables(s, l1_rpb)
    sinks = l1_sinks.astype(F32)
    oc = _banded("window", p1, p1c, b, jnp.asarray(w_st), jnp.asarray(w_var), w_bias, sinks, 0, 16, 18, 0, 2)
    od = _banded("na", p1, p1c, b, jnp.asarray(n_st), jnp.asarray(n_var), n_bias, sinks, 8, 20, 28, 4, 12)
    x3 = _outproj(oc, od, l1_w_out.astype(BF16), x2, lat(mod1, 2), s, tm)

    wr_pad = jnp.zeros((d, LANES), F32).at[:, :n_exp].set(l1_router.astype(F32))
    h2, route_i, route_g = _router(x3, s, l1_norm2_g, lat(mod1, 4), lat(mod1, 3), wr_pad, n_exp, tm)
    route_g = route_g[:n]
    e_flat = route_i[:n, :TOP_K].reshape(-1)
    onehot = (e_flat[:, None] == jnp.arange(n_exp, dtype=jnp.int32)[None, :]).astype(jnp.int32)
    csum = jnp.cumsum(onehot, axis=0)
    rank = jnp.sum((csum - onehot) * onehot, axis=1)
    counts = csum[-1]
    padded = (counts + tm - 1) // tm * tm
    pad_end = jnp.cumsum(padded)
    pad_start = pad_end - padded
    slot = (pad_start[e_flat] + rank).astype(jnp.int32)
    n_blocks = (n * TOP_K + n_exp * (tm - 1) + tm - 1) // tm
    cap = n_blocks * tm
    tok_flat = jnp.repeat(jnp.arange(n, dtype=jnp.int32), TOP_K)
    slot_tok = jnp.full((cap,), n, jnp.int32).at[slot].set(tok_flat)
    block_expert = jnp.minimum(jnp.searchsorted(pad_end, jnp.arange(n_blocks, dtype=jnp.int32) * tm, side="right"),
                               n_exp - 1).astype(jnp.int32)
    n_used = (pad_end[-1] // tm).astype(jnp.int32).reshape(1)

    y = _moe(slot_tok, h2, block_expert, n_used, l1_exp_w_gate.astype(BF16), l1_exp_w_up.astype(BF16),
             l1_exp_w_down.astype(BF16), tm)
    out = _combine(slot, x3, s, lat(mod1, 5), route_g, final_norm_g, y, 256)
    return out.reshape(b, s, d)
```

```python
import functools
import math

import numpy as np
import jax
import jax.numpy as jnp
from jax import lax
from jax.experimental import pallas as pl
from jax.experimental.pallas import tpu as pltpu

F32 = jnp.float32
BF16 = jnp.bfloat16

HEAD_DIM = 128
LANES = 128
GRID_W = 64
ROPE_BASE = 10000.0
EPS = 1e-6
NEG_INF = -1e30
A_SUB = HEAD_DIM // 2
A_SCALE = A_SUB ** -0.5
ATT_SCALE = HEAD_DIM ** -0.5
LAMBDA_INIT_L0 = 0.8 - 0.6 * math.exp(-0.3 * 0)
LOG2E = 1.4426950408889634
C_WINDOW = 128
NA_ROWS = 8
NA_COLS = 16
TOP_K = 2
VMEM_LIMIT = 56 * 1024 * 1024
T_ROPE, T_QNORM, T_KNORM, T_SCALE_B, T_SCALE_A = 1, 2, 4, 8, 16


def _cparams(sem):
    return pltpu.CompilerParams(dimension_semantics=sem, vmem_limit_bytes=VMEM_LIMIT)


def _modulate(xf, g, scale, shift):
    ms = jnp.mean(xf * xf, axis=-1, keepdims=True)
    y = xf * lax.rsqrt(ms + EPS) * g
    return y * (1.0 + scale) + shift


def _ada_kernel(c_ref, w_ref, b_ref, o_ref):
    c = c_ref[...]
    a = c * (1.0 / (1.0 + jnp.exp(-c)))
    o_ref[...] = jnp.dot(a.astype(BF16), w_ref[...].astype(BF16), preferred_element_type=F32) + b_ref[...]


def _ada(cvecs, w_ada, b_ada):
    r, d = cvecs.shape
    n = w_ada.shape[1]
    tn = 1024
    return pl.pallas_call(
        _ada_kernel,
        out_shape=jax.ShapeDtypeStruct((r, n), F32),
        grid=(n // tn,),
        in_specs=[pl.BlockSpec((r, d), lambda j: (0, 0)),
                  pl.BlockSpec((d, tn), lambda j: (0, j)),
                  pl.BlockSpec((1, tn), lambda j: (0, j))],
        out_specs=pl.BlockSpec((r, tn), lambda j: (0, j)),
        compiler_params=_cparams(("parallel",)),
        name="ada",
    )(cvecs, w_ada, b_ada.reshape(1, n))


def _inproj_kernel(types_ref, tab_ref, x_ref, g_ref, sc_ref, sh_ref, w_ref, cos_ref, sin_ref, qkg_ref,
                   o_ref, h_scr, *, heads_per_tile):
    del tab_ref
    j = pl.program_id(1)

    @pl.when(j == 0)
    def _():
        h_scr[...] = _modulate(x_ref[...], g_ref[...], sc_ref[...], sh_ref[...]).astype(BF16)

    acc = jnp.dot(h_scr[...], w_ref[...], preferred_element_type=F32)

    def rope(y):
        return y * cos_ref[...] + pltpu.roll(y, HEAD_DIM // 2, 1) * sin_ref[...]

    for hh in range(heads_per_tile):
        t = types_ref[j * heads_per_tile + hh]
        sl = slice(hh * HEAD_DIM, (hh + 1) * HEAD_DIM)
        y = acc[:, sl]
        has_norm = (t & (T_QNORM | T_KNORM)) != 0
        has_rope = (t & T_ROPE) != 0
        f = jnp.where((t & T_SCALE_B) != 0, ATT_SCALE * LOG2E,
                      jnp.where((t & T_SCALE_A) != 0, A_SCALE * LOG2E, 1.0)).astype(F32)

        @pl.when(t == 0)
        def _():
            o_ref[:, sl] = y.astype(o_ref.dtype)

        @pl.when(jnp.logical_and(t != 0, jnp.logical_not(jnp.logical_or(has_norm, has_rope))))
        def _():
            o_ref[:, sl] = (y * f).astype(o_ref.dtype)

        @pl.when(jnp.logical_and(has_rope, jnp.logical_not(has_norm)))
        def _():
            o_ref[:, sl] = (rope(y) * f).astype(o_ref.dtype)

        @pl.when(has_norm)
        def _():
            ms = jnp.mean(y * y, axis=-1, keepdims=True)
            gain = jnp.where((t & T_QNORM) != 0, qkg_ref[0:1, :], qkg_ref[1:2, :])
            yn = y * lax.rsqrt(ms + EPS) * gain
            o_ref[:, sl] = (jnp.where(has_rope, rope(yn), yn) * f).astype(o_ref.dtype)


def _inproj(x2d, rows_per_batch, g, scale, shift, w_bf, cos_tab, sin_tab, qkg, types, tabsel, tm):
    m, d = x2d.shape
    n = w_bf.shape[1]
    tn = 512
    hpt = tn // HEAD_DIM
    tiles_per_batch = rows_per_batch // tm
    tab_rows = cos_tab.shape[1] // tm
    kern = functools.partial(_inproj_kernel, heads_per_tile=hpt)
    grid_spec = pltpu.PrefetchScalarGridSpec(
        num_scalar_prefetch=2,
        grid=(m // tm, n // tn),
        in_specs=[
            pl.BlockSpec((tm, d), lambda i, j, ty, tb: (i, 0)),
            pl.BlockSpec((1, d), lambda i, j, ty, tb: (0, 0)),
            pl.BlockSpec((None, 1, d), lambda i, j, ty, tb: (i // tiles_per_batch, 0, 0)),
            pl.BlockSpec((None, 1, d), lambda i, j, ty, tb: (i // tiles_per_batch, 0, 0)),
            pl.BlockSpec((d, tn), lambda i, j, ty, tb: (0, j)),
            pl.BlockSpec((None, tm, LANES), lambda i, j, ty, tb: (tb[j], i % tab_rows, 0)),
            pl.BlockSpec((None, tm, LANES), lambda i, j, ty, tb: (tb[j], i % tab_rows, 0)),
            pl.BlockSpec((2, HEAD_DIM), lambda i, j, ty, tb: (0, 0)),
        ],
        out_specs=pl.BlockSpec((tm, tn), lambda i, j, ty, tb: (i, j)),
        scratch_shapes=[pltpu.VMEM((tm, d), BF16)],
    )
    return pl.pallas_call(
        kern,
        out_shape=jax.ShapeDtypeStruct((m, n), BF16),
        grid_spec=grid_spec,
        compiler_params=_cparams(("parallel", "arbitrary")),
        name="inproj",
    )(types, tabsel, x2d, g.reshape(1, d), scale, shift, w_bf, cos_tab, sin_tab, qkg)


def _flash_kernel(*refs, mode, tq, has_ctx, nk):
    refs = list(refs)
    q_ref, k_ref, v_ref = refs[:3]
    pos = 3
    if has_ctx:
        kc_ref, vc_ref = refs[3:5]
        pos = 5
    if mode == "diff":
        lam_ref, subg_ref = refs[pos:pos + 2]
        pos += 2
    o_ref, qs, m_s, l_s, acc_s = refs[pos:pos + 5]
    ki = pl.program_id(3)

    @pl.when(ki == 0)
    def _():
        q = q_ref[...]
        if mode == "diff":
            lane = lax.broadcasted_iota(jnp.int32, (1, LANES), 1)
            zero = jnp.zeros_like(q)
            first = (lane & (A_SUB // 2)) == 0
            qs[0:tq, :] = jnp.where(first, q, zero)
            qs[tq:2 * tq, :] = jnp.where(first, zero, q)
        else:
            for gi in range(4):
                qs[gi * tq:(gi + 1) * tq, :] = q[:, gi * HEAD_DIM:(gi + 1) * HEAD_DIM]
        m_s[...] = jnp.full_like(m_s, NEG_INF)
        l_s[...] = jnp.zeros_like(l_s)
        acc_s[...] = jnp.zeros_like(acc_s)

    def step(k, v):
        s = lax.dot_general(qs[...], k, (((1,), (1,)), ((), ())), preferred_element_type=F32)
        slabs = [s[:, c * LANES:(c + 1) * LANES] for c in range(k.shape[0] // LANES)]
        mx = functools.reduce(jnp.maximum, slabs)
        m_prev = m_s[...]
        m_new = jnp.maximum(m_prev, jnp.max(mx, axis=-1, keepdims=True))
        alpha = jnp.exp2(m_prev - m_new)
        ps = [jnp.exp2(sl - m_new) for sl in slabs]
        l_s[...] = alpha * l_s[...] + functools.reduce(jnp.add, ps)
        p = jnp.concatenate([x.astype(BF16) for x in ps], axis=1)
        acc_s[...] = alpha * acc_s[...] + jnp.dot(p, v, preferred_element_type=F32)
        m_s[...] = m_new

    step(k_ref[...], v_ref[...])

    @pl.when(ki == nk - 1)
    def _():
        if has_ctx:
            step(kc_ref[...], vc_ref[...])
        o = acc_s[...] / jnp.sum(l_s[...], axis=-1, keepdims=True)
        if mode == "diff":
            lam1 = jnp.sum(lam_ref[0:1, :] * lam_ref[1:2, :], axis=-1, keepdims=True)
            lam2 = jnp.sum(lam_ref[2:3, :] * lam_ref[3:4, :], axis=-1, keepdims=True)
            lam = jnp.exp(lam1) - jnp.exp(lam2) + LAMBDA_INIT_L0
            od = o[0:tq, :] - lam * o[tq:2 * tq, :]
            ms = jnp.mean(od * od, axis=-1, keepdims=True)
            on = od * lax.rsqrt(ms + EPS) * subg_ref[...]
            o_ref[...] = (on * (1.0 - LAMBDA_INIT_L0)).astype(o_ref.dtype)
        else:
            for gi in range(4):
                o_ref[:, gi * HEAD_DIM:(gi + 1) * HEAD_DIM] = o[gi * tq:(gi + 1) * tq, :].astype(o_ref.dtype)


def _flash(mode, q_arr, kv_arr, ctx_arr, batch, tq, tk, qcol, kcol, vcol, lam=None, subg=None):
    mq = q_arr.shape[0]
    sq = mq // batch
    sk = kv_arr.shape[0] // batch
    nq, nk = sq // tq, sk // tk
    has_ctx = ctx_arr is not None
    if mode == "diff":
        nh, qw, reps = 8, HEAD_DIM, 2
        q_spec = pl.BlockSpec((tq, qw), lambda b, h, qi, ki: (b * nq + qi, qcol + h))
        o_spec = pl.BlockSpec((tq, qw), lambda b, h, qi, ki: (b * nq + qi, h))
    else:
        nh, qw, reps = 2, 4 * HEAD_DIM, 4
        q_spec = pl.BlockSpec((tq, qw), lambda b, h, qi, ki: (b * nq + qi, qcol // 4 + h))
        o_spec = pl.BlockSpec((tq, qw), lambda b, h, qi, ki: (b * nq + qi, h))
    in_specs = [q_spec,
                pl.BlockSpec((tk, HEAD_DIM), lambda b, h, qi, ki: (b * nk + ki, kcol + h)),
                pl.BlockSpec((tk, HEAD_DIM), lambda b, h, qi, ki: (b * nk + ki, vcol + h))]
    args = [q_arr, kv_arr, kv_arr]
    if has_ctx:
        nc = ctx_arr.shape[0] // batch
        in_specs += [pl.BlockSpec((nc, HEAD_DIM), lambda b, h, qi, ki: (b, kcol + h)),
                     pl.BlockSpec((nc, HEAD_DIM), lambda b, h, qi, ki: (b, vcol + h))]
        args += [ctx_arr, ctx_arr]
    if mode == "diff":
        in_specs += [pl.BlockSpec((4, A_SUB), lambda b, h, qi, ki: (0, 0)),
                     pl.BlockSpec((1, HEAD_DIM), lambda b, h, qi, ki: (0, 0))]
        args += [lam, subg]
    r = reps * tq
    kern = functools.partial(_flash_kernel, mode=mode, tq=tq, has_ctx=has_ctx, nk=nk)
    return pl.pallas_call(
        kern,
        out_shape=jax.ShapeDtypeStruct((mq, 8 * HEAD_DIM), BF16),
        grid=(batch, nh, nq, nk),
        in_specs=in_specs,
        out_specs=o_spec,
        scratch_shapes=[pltpu.VMEM((r, HEAD_DIM), BF16), pltpu.VMEM((r, LANES), F32),
                        pltpu.VMEM((r, LANES), F32), pltpu.VMEM((r, HEAD_DIM), F32)],
        compiler_params=_cparams(("parallel", "parallel", "parallel", "arbitrary")),
        name="flash_" + mode,
    )(*args)


def _banded_kernel(start_ref, var_ref, q_ref, k_ref, v_ref, kc_ref, vc_ref, bias_ref, sink_ref, o_ref,
                   *, mode, tq, band):
    del var_ref
    hg = pl.program_id(1)
    j = pl.program_id(2)
    start = pl.multiple_of(start_ref[j], LANES)
    kb = k_ref[pl.ds(start, band), :]
    vb = v_ref[pl.ds(start, band), :]
    kc = kc_ref[...]
    vc = vc_ref[...]
    q = q_ref[...]
    dn = (((1,), (1,)), ((), ()))
    heads = [slice(gi * HEAD_DIM, (gi + 1) * HEAD_DIM) for gi in range(4)]
    if mode == "window":
        qs = jnp.concatenate([q[:, hd] for hd in heads], axis=0)
        s_loc = lax.dot_general(qs, kb, dn, preferred_element_type=F32)
        s_ctx = lax.dot_general(qs, kc, dn, preferred_element_type=F32)
        bias = jnp.concatenate([bias_ref[...]] * 4, axis=0)
    else:
        s_loc = jnp.concatenate([lax.dot_general(q[:, hd], kb[:, hd], dn, preferred_element_type=F32)
                                 for hd in heads], axis=0)
        s_ctx = jnp.concatenate([lax.dot_general(q[:, hd], kc[:, hd], dn, preferred_element_type=F32)
                                 for hd in heads], axis=0)
        bias = bias_ref[...]
    s_loc = s_loc + bias
    n_loc = band // LANES
    slabs = ([s_loc[:, c * LANES:(c + 1) * LANES] for c in range(n_loc)]
             + [s_ctx[:, c * LANES:(c + 1) * LANES] for c in range(s_ctx.shape[1] // LANES)])
    mx = functools.reduce(jnp.maximum, slabs)
    m = jnp.broadcast_to(jnp.max(mx, axis=-1, keepdims=True), mx.shape)
    if mode == "window":
        sink = jnp.concatenate([jnp.full((tq, LANES), sink_ref[hg * 4 + gi] * LOG2E, F32) for gi in range(4)], axis=0)
        m = jnp.maximum(m, sink)
    ps = [jnp.exp2(sl - m) for sl in slabs]
    l = jnp.sum(functools.reduce(jnp.add, ps), axis=-1, keepdims=True)
    if mode == "window":
        l = l + jnp.exp2(sink - m)[:, 0:1]
    p_loc = jnp.concatenate([x.astype(BF16) for x in ps[:n_loc]], axis=1)
    p_ctx = jnp.concatenate([x.astype(BF16) for x in ps[n_loc:]], axis=1)
    if mode == "window":
        o = jnp.dot(p_loc, vb, preferred_element_type=F32) + jnp.dot(p_ctx, vc, preferred_element_type=F32)
        o = (o / l).astype(o_ref.dtype)
        for gi in range(4):
            o_ref[:, heads[gi]] = o[gi * tq:(gi + 1) * tq, :]
    else:
        for gi in range(4):
            rows = slice(gi * tq, (gi + 1) * tq)
            o = (jnp.dot(p_loc[rows, :], vb[:, heads[gi]], preferred_element_type=F32)
                 + jnp.dot(p_ctx[rows, :], vc[:, heads[gi]], preferred_element_type=F32))
            o_ref[:, heads[gi]] = (o / l[rows, :]).astype(o_ref.dtype)


def _banded(mode, proj, proj_ctx, batch, starts, variants, bias, sinks, qcol, kcol, vcol, kc_col, vc_col):
    m = proj.shape[0]
    s = m // batch
    tq = 128
    nq = s // tq
    nc = proj_ctx.shape[0] // batch
    band = bias.shape[-1]
    qw = 4 * HEAD_DIM
    if mode == "window":
        kw, div = HEAD_DIM, 1
        bias_spec = pl.BlockSpec((None, tq, band), lambda b, h, j, st, va: (va[j], 0, 0))
    else:
        kw, div = qw, 4
        bias_spec = pl.BlockSpec((None, None, 4 * tq, band), lambda b, h, j, st, va: (h, va[j], 0, 0))
    grid_spec = pltpu.PrefetchScalarGridSpec(
        num_scalar_prefetch=2,
        grid=(batch, 2, nq),
        in_specs=[
            pl.BlockSpec((tq, qw), lambda b, h, j, st, va: (b * nq + j, qcol // 4 + h)),
            pl.BlockSpec((s, kw), lambda b, h, j, st, va: (b, kcol // div + h)),
            pl.BlockSpec((s, kw), lambda b, h, j, st, va: (b, vcol // div + h)),
            pl.BlockSpec((nc, kw), lambda b, h, j, st, va: (b, kc_col // div + h)),
            pl.BlockSpec((nc, kw), lambda b, h, j, st, va: (b, vc_col // div + h)),
            bias_spec,
            pl.BlockSpec(memory_space=pltpu.SMEM),
        ],
        out_specs=pl.BlockSpec((tq, qw), lambda b, h, j, st, va: (b * nq + j, h)),
    )
    kern = functools.partial(_banded_kernel, mode=mode, tq=tq, band=band)
    return pl.pallas_call(
        kern,
        out_shape=jax.ShapeDtypeStruct((m, 8 * HEAD_DIM), BF16),
        grid_spec=grid_spec,
        compiler_params=_cparams(("parallel", "parallel", "arbitrary")),
        name="banded_" + mode,
    )(starts, variants, proj, proj, proj, proj_ctx, proj_ctx, bias, sinks)


def _outproj_kernel(oa_ref, ob_ref, wa_ref, wb_ref, x_ref, gt_ref, o_ref):
    acc = jnp.dot(oa_ref[...], wa_ref[...], preferred_element_type=F32)
    acc = acc + jnp.dot(ob_ref[...], wb_ref[...], preferred_element_type=F32)
    o_ref[...] = x_ref[...] + gt_ref[...] * acc


def _outproj(oa, ob, w_bf, x2d, gate, rows_per_batch, tm):
    m, d = x2d.shape
    half = oa.shape[1]
    tiles_per_batch = rows_per_batch // tm
    return pl.pallas_call(
        _outproj_kernel,
        out_shape=jax.ShapeDtypeStruct((m, d), F32),
        grid=(m // tm,),
        in_specs=[pl.BlockSpec((tm, half), lambda i: (i, 0)),
                  pl.BlockSpec((tm, half), lambda i: (i, 0)),
                  pl.BlockSpec((half, d), lambda i: (0, 0)),
                  pl.BlockSpec((half, d), lambda i: (1, 0)),
                  pl.BlockSpec((tm, d), lambda i: (i, 0)),
                  pl.BlockSpec((None, 1, d), lambda i: (i // tiles_per_batch, 0, 0))],
        out_specs=pl.BlockSpec((tm, d), lambda i: (i, 0)),
        compiler_params=_cparams(("parallel",)),
        name="outproj",
    )(oa, ob, w_bf, w_bf, x2d, gate)


def _swiglu_step(h, wg, wu, wd):
    g = jnp.dot(h, wg, preferred_element_type=F32)
    u = jnp.dot(h, wu, preferred_element_type=F32)
    a = (g * (1.0 / (1.0 + jnp.exp(-g))) * u).astype(BF16)
    return jnp.dot(a, wd, preferred_element_type=F32)


def _ffn_kernel(x_ref, g_ref, sc_ref, sh_ref, gt_ref, wg_ref, wu_ref, wd_ref, o_ref, h_scr, acc_scr, *, nf):
    j = pl.program_id(1)

    @pl.when(j == 0)
    def _():
        h_scr[...] = _modulate(x_ref[...], g_ref[...], sc_ref[...], sh_ref[...]).astype(BF16)
        acc_scr[...] = jnp.zeros_like(acc_scr)

    acc_scr[...] += _swiglu_step(h_scr[...], wg_ref[...], wu_ref[...], wd_ref[...])

    @pl.when(j == nf - 1)
    def _():
        o_ref[...] = x_ref[...] + gt_ref[...] * acc_scr[...]


def _ffn(x2d, rows_per_batch, g, scale, shift, gate, wg, wu, wd, tm):
    m, d = x2d.shape
    f = wg.shape[1]
    tf = 512
    nf = f // tf
    tiles_per_batch = rows_per_batch // tm
    vec = pl.BlockSpec((None, 1, d), lambda i, j: (i // tiles_per_batch, 0, 0))
    return pl.pallas_call(
        functools.partial(_ffn_kernel, nf=nf),
        out_shape=jax.ShapeDtypeStruct((m, d), F32),
        grid=(m // tm, nf),
        in_specs=[pl.BlockSpec((tm, d), lambda i, j: (i, 0)),
                  pl.BlockSpec((1, d), lambda i, j: (0, 0)),
                  vec, vec, vec,
                  pl.BlockSpec((d, tf), lambda i, j: (0, j)),
                  pl.BlockSpec((d, tf), lambda i, j: (0, j)),
                  pl.BlockSpec((tf, d), lambda i, j: (j, 0))],
        out_specs=pl.BlockSpec((tm, d), lambda i, j: (i, 0)),
        scratch_shapes=[pltpu.VMEM((tm, d), BF16), pltpu.VMEM((tm, d), F32)],
        compiler_params=_cparams(("parallel", "arbitrary")),
        name="ffn",
    )(x2d, g.reshape(1, d), scale, shift, gate, wg, wu, wd)


def _router_kernel(x_ref, g_ref, sc_ref, sh_ref, wr_ref, h_ref, ri_ref, rg_ref, *, n_exp, n_tiles):
    h = _modulate(x_ref[...], g_ref[...], sc_ref[...], sh_ref[...])
    h_ref[...] = jnp.where(pl.program_id(0) < n_tiles, h, 0.0)
    logits = jnp.dot(h, wr_ref[...], preferred_element_type=F32, precision=lax.Precision.HIGHEST)
    lane = lax.broadcasted_iota(jnp.int32, logits.shape, 1)
    logits = jnp.where(lane < n_exp, logits, -jnp.inf)
    lane_f = lane.astype(F32)
    m1 = jnp.max(logits, axis=-1, keepdims=True)
    i1 = jnp.min(jnp.where(logits == m1, lane_f, float(LANES)), axis=-1, keepdims=True)
    rest = jnp.where(lane_f == i1, -jnp.inf, logits)
    m2 = jnp.max(rest, axis=-1, keepdims=True)
    i2 = jnp.min(jnp.where(rest == m2, lane_f, float(LANES)), axis=-1, keepdims=True)
    e = jnp.exp(m2 - m1)
    g1 = 1.0 / (1.0 + e)
    g2 = e / (1.0 + e)
    ri_ref[...] = jnp.where(lane == 0, i1, jnp.where(lane == 1, i2, 0.0)).astype(jnp.int32)
    rg_ref[...] = jnp.where(lane == 0, g1, jnp.where(lane == 1, g2, 0.0))


def _router(x2d, rows_per_batch, g, scale, shift, wr_pad, n_exp, tm):
    m, d = x2d.shape
    tiles_per_batch = rows_per_batch // tm
    n_tiles = m // tm
    last = n_tiles - 1
    vec = pl.BlockSpec((None, 1, d), lambda i: (jnp.minimum(i, last) // tiles_per_batch, 0, 0))
    return pl.pallas_call(
        functools.partial(_router_kernel, n_exp=n_exp, n_tiles=n_tiles),
        out_shape=(jax.ShapeDtypeStruct((m + tm, d), F32),
                   jax.ShapeDtypeStruct((m + tm, LANES), jnp.int32),
                   jax.ShapeDtypeStruct((m + tm, LANES), F32)),
        grid=(n_tiles + 1,),
        in_specs=[pl.BlockSpec((tm, d), lambda i: (jnp.minimum(i, last), 0)),
                  pl.BlockSpec((1, d), lambda i: (0, 0)),
                  vec, vec,
                  pl.BlockSpec((d, LANES), lambda i: (0, 0))],
        out_specs=(pl.BlockSpec((tm, d), lambda i: (i, 0)),
                   pl.BlockSpec((tm, LANES), lambda i: (i, 0)),
                   pl.BlockSpec((tm, LANES), lambda i: (i, 0))),
        compiler_params=_cparams(("parallel",)),
        name="router",
    )(x2d, g.reshape(1, d), scale, shift, wr_pad)


def _moe_kernel(be_ref, nv_ref, tok_ref, h_hbm, wg_ref, wu_ref, wd_ref, o_ref, xbuf, xb_scr, sem,
                *, nf, tm, nblk, per_step):
    del be_ref
    i = pl.program_id(0)
    j = pl.program_id(1)
    n_valid = nv_ref[i]
    half = tm // 2
    n_copies = nf * per_step

    def row_copy(blk, r):
        tok = tok_ref[blk * tm + jnp.minimum(r, tm - 1)]
        return pltpu.make_async_copy(h_hbm.at[pl.ds(tok, 1)], xbuf.at[pl.ds(r, 1)], sem)

    def wait_all():
        pltpu.make_async_copy(h_hbm.at[pl.ds(0, n_copies)], xbuf.at[pl.ds(0, n_copies)], sem).wait()

    def start_next():
        nxt = jnp.minimum(i + 1, nblk - 1)
        for u in range(per_step):
            row_copy(nxt, j * per_step + u).start()

    @pl.when(j == 0)
    def _():
        @pl.when(i == 0)
        def _():
            def body(r, carry):
                row_copy(0, r).start()
                return carry
            lax.fori_loop(0, n_copies, body, 0)

        wait_all()
        xb_scr[...] = xbuf[0:tm, :].astype(BF16)
        o_ref[...] = jnp.zeros_like(o_ref)

    def accumulate(rows):
        start_next()
        o_ref[rows, :] += _swiglu_step(xb_scr[rows, :], wg_ref[...].astype(BF16), wu_ref[...].astype(BF16),
                                       wd_ref[...].astype(BF16))

    @pl.when(n_valid > half)
    def _():
        accumulate(slice(0, tm))

    @pl.when(jnp.logical_and(n_valid > 0, n_valid <= half))
    def _():
        accumulate(slice(0, half))

    @pl.when(n_valid == 0)
    def _():
        start_next()

    @pl.when(jnp.logical_and(i == nblk - 1, j == nf - 1))
    def _():
        wait_all()


def _moe(slot_tok, h_pad, block_expert, n_valid, wg, wu, wd, tm):
    cap = slot_tok.shape[0]
    d = h_pad.shape[1]
    f = wg.shape[2]
    tf = 256
    nf = f // tf
    nblk = cap // tm
    per_step = -(-tm // nf)
    while (nf * per_step) % 8:
        per_step += 1
    buf_rows = nf * per_step

    def col_j(i, j, nv):
        return jnp.where(nv[i] > 0, j, nf - 1)

    grid_spec = pltpu.PrefetchScalarGridSpec(
        num_scalar_prefetch=3,
        grid=(nblk, nf),
        in_specs=[pl.BlockSpec(memory_space=pl.ANY),
                  pl.BlockSpec((None, d, tf), lambda i, j, be, nv, tok: (be[i], 0, col_j(i, j, nv))),
                  pl.BlockSpec((None, d, tf), lambda i, j, be, nv, tok: (be[i], 0, col_j(i, j, nv))),
                  pl.BlockSpec((None, tf, d), lambda i, j, be, nv, tok: (be[i], col_j(i, j, nv), 0))],
        out_specs=pl.BlockSpec((tm, d), lambda i, j, be, nv, tok: (i, 0)),
        scratch_shapes=[pltpu.VMEM((buf_rows, d), F32), pltpu.VMEM((tm, d), BF16), pltpu.SemaphoreType.DMA],
    )
    return pl.pallas_call(
        functools.partial(_moe_kernel, nf=nf, tm=tm, nblk=nblk, per_step=per_step),
        out_shape=jax.ShapeDtypeStruct((cap, d), F32),
        grid_spec=grid_spec,
        compiler_params=_cparams(("arbitrary", "arbitrary")),
        name="moe",
    )(block_expert, n_valid, slot_tok, h_pad, wg, wu, wd)


def _combine_kernel(slot_ref, x_ref, gt_ref, rg_ref, fg_ref, y_hbm, o_ref, buf, sem, *, tc):
    base = pl.program_id(0) * tc

    def issue(r, carry):
        for k in range(TOP_K):
            s = slot_ref[TOP_K * (base + r) + k]
            pltpu.make_async_copy(y_hbm.at[pl.ds(s, 1)], buf.at[pl.ds(k * tc + r, 1)], sem).start()
        return carry

    lax.fori_loop(0, tc, issue, 0, unroll=8)
    pltpu.make_async_copy(y_hbm.at[pl.ds(0, TOP_K * tc)], buf, sem).wait()
    rg = rg_ref[...]
    y = rg[:, 0:1] * buf[0:tc, :] + rg[:, 1:2] * buf[tc:2 * tc, :]
    xo = x_ref[...] + gt_ref[...] * y
    ms = jnp.mean(xo * xo, axis=-1, keepdims=True)
    o_ref[...] = xo * lax.rsqrt(ms + EPS) * fg_ref[...]


def _combine(slot, x2d, rows_per_batch, gate, route_g, final_g, y, tc):
    m, d = x2d.shape
    tiles_per_batch = rows_per_batch // tc
    grid_spec = pltpu.PrefetchScalarGridSpec(
        num_scalar_prefetch=1,
        grid=(m // tc,),
        in_specs=[pl.BlockSpec((tc, d), lambda i, sl: (i, 0)),
                  pl.BlockSpec((None, 1, d), lambda i, sl: (i // tiles_per_batch, 0, 0)),
                  pl.BlockSpec((tc, LANES), lambda i, sl: (i, 0)),
                  pl.BlockSpec((1, d), lambda i, sl: (0, 0)),
                  pl.BlockSpec(memory_space=pl.ANY)],
        out_specs=pl.BlockSpec((tc, d), lambda i, sl: (i, 0)),
        scratch_shapes=[pltpu.VMEM((TOP_K * tc, d), F32), pltpu.SemaphoreType.DMA],
    )
    return pl.pallas_call(
        functools.partial(_combine_kernel, tc=tc),
        out_shape=jax.ShapeDtypeStruct((m, d), F32),
        grid_spec=grid_spec,
        compiler_params=_cparams(("arbitrary",)),
        name="combine",
    )(slot, x2d, gate, route_g, final_g.reshape(1, d), y)


def _rope_tables(n_tok, dim):
    t = jnp.arange(n_tok)
    pos = jnp.stack([t // GRID_W, t % GRID_W], axis=-1).astype(F32)
    n_pairs = dim // 4
    freq = ROPE_BASE ** (-jnp.arange(n_pairs, dtype=F32) / n_pairs)
    ang = (pos[:, :, None] * freq).reshape(n_tok, 2 * n_pairs)
    cos = jnp.tile(jnp.cos(ang), (1, HEAD_DIM // dim))
    sin = jnp.tile(jnp.sin(ang), (1, HEAD_DIM // dim))
    return jnp.concatenate([cos, cos], axis=1), jnp.concatenate([-sin, sin], axis=1)


def _deinterleave(w, comps):
    d, n = w.shape
    pairs = HEAD_DIM // (2 * comps)
    return w.reshape(d, n // HEAD_DIM, comps, pairs, 2).transpose(0, 1, 4, 2, 3).reshape(d, n)


def _dedupe(patterns):
    keys, variant = {}, []
    for p in patterns:
        variant.append(keys.setdefault(p.tobytes(), len(keys)))
    first = [variant.index(v) for v in range(len(keys))]
    return np.asarray(variant, np.int32), first


def _window_tables(s):
    tq = 128
    nb = s // tq
    band = 3 * tq
    starts = np.clip(np.arange(nb) - 1, 0, nb - 3) * tq
    qpos = np.arange(nb)[:, None] * tq + np.arange(tq)[None, :]
    kpos = starts[:, None] + np.arange(band)[None, :]
    ok = np.abs(kpos[:, None, :] - qpos[:, :, None]) <= C_WINDOW
    variant, first = _dedupe(list(ok))
    bias = np.where(ok[first], 0.0, NEG_INF).astype(np.float32)
    return starts.astype(np.int32), variant, jnp.asarray(bias)


def _na_tables(s, rpb):
    tq = 128
    rows = s // GRID_W
    wr = min(NA_ROWS, rows)
    nb = s // tq
    band_blocks = 5
    band = band_blocks * tq
    starts = np.clip(np.arange(nb) - 2, 0, nb - band_blocks) * tq
    n_h = rpb.shape[0]
    n_dc = 2 * NA_COLS - 1
    period = GRID_W + n_dc - 1
    wrap = jnp.zeros((n_h, 2 * NA_ROWS - 1, period), F32)
    wrap = wrap.at[..., :NA_COLS].set(rpb[..., NA_COLS - 1:].astype(F32))
    wrap = wrap.at[..., period - (NA_COLS - 1):].set(rpb[..., :NA_COLS - 1].astype(F32))
    toep = jnp.tile(wrap, (1, 1, GRID_W))[..., :GRID_W * (period - 1)]
    toep = toep.reshape(n_h, 2 * NA_ROWS - 1, GRID_W, period - 1)[..., :GRID_W]
    qc = np.arange(GRID_W)[:, None]
    kc = np.arange(GRID_W)[None, :]
    cstart = np.clip(qc - NA_COLS // 2, 0, GRID_W - NA_COLS)
    col_ok = (kc >= cstart) & (kc < cstart + NA_COLS)
    toep = jnp.where(jnp.asarray(col_ok), toep * LOG2E, NEG_INF)
    masked = jnp.full((n_h, GRID_W, GRID_W), NEG_INF, F32)
    pats = []
    for j in range(nb):
        pat = np.full((tq // GRID_W, band // GRID_W), -1, np.int32)
        for a in range(tq // GRID_W):
            qr = j * (tq // GRID_W) + a
            rs = min(max(qr - wr // 2, 0), rows - wr)
            for w in range(band // GRID_W):
                kr = starts[j] // GRID_W + w
                if rs <= kr < rs + wr:
                    pat[a, w] = kr - qr + NA_ROWS - 1
        pats.append(pat)
    variant, first = _dedupe(pats)
    tiles = []
    for j in first:
        rows_ = [jnp.concatenate([toep[:, dr] if dr >= 0 else masked for dr in pats[j][a]], axis=-1)
                 for a in range(tq // GRID_W)]
        tiles.append(jnp.concatenate(rows_, axis=-2))
    bias = jnp.stack(tiles, axis=1)
    bias = bias.reshape(n_h // 4, 4, len(first), tq, band).transpose(0, 2, 1, 3, 4)
    return starts.astype(np.int32), variant, bias.reshape(n_h // 4, len(first), 4 * tq, band)


def kernel(x, c, ctx, c_ctx, l0_norm1_g, l0_norm2_g, l0_w_ada, l0_b_ada, l0_w_in, l0_w_out, l0_lam_q1, l0_lam_k1, l0_lam_q2, l0_lam_k2, l0_subln_g, l0_q_norm_g, l0_k_norm_g, l0_ffn_w_gate, l0_ffn_w_up, l0_ffn_w_down, l1_norm1_g, l1_norm2_g, l1_w_ada, l1_b_ada, l1_w_in, l1_w_out, l1_sinks, l1_rpb, l1_router, l1_exp_w_gate, l1_exp_w_up, l1_exp_w_down, final_norm_g):
    b, s, d = x.shape
    n_ctx = ctx.shape[1]
    n_exp = l1_router.shape[1]
    n = b * s
    assert d == 2048 and s % 512 == 0 and n_ctx == 256 and s // GRID_W >= 10

    x2d = x.reshape(n, d)
    xc2d = ctx.reshape(b * n_ctx, d)

    cvecs = jnp.zeros((8, d), F32).at[:b].set(c).at[b].set(c_ctx)
    mod0 = _ada(cvecs, l0_w_ada, l0_b_ada)
    mod1 = _ada(cvecs, l1_w_ada, l1_b_ada)

    def lat(mod, k):
        return mod[:b, k * d:(k + 1) * d].reshape(b, 1, d)

    def cx(mod, k):
        return jnp.broadcast_to(mod[b, k * d:(k + 1) * d].reshape(1, 1, d), (b, 1, d))

    cos_a, sin_a = _rope_tables(s, A_SUB)
    cos_b, sin_b = _rope_tables(s, HEAD_DIM)
    cos_tab = jnp.stack([cos_a, cos_b])
    sin_tab = jnp.stack([sin_a, sin_b])
    tm = 512
    hq = 8 * HEAD_DIM

    qa = T_ROPE | T_SCALE_A
    qb = T_ROPE | T_QNORM | T_SCALE_B
    kb = T_ROPE | T_KNORM
    types0 = np.array([qa] * 8 + [qb] * 8 + [T_ROPE] * 8 + [0] * 8 + [kb] * 2 + [0] * 2, np.int32)
    tabsel0 = np.array([0, 0, 1, 1, 0, 0, 0, 0, 1], np.int32)
    qkg0 = _deinterleave(jnp.stack([l0_q_norm_g, l0_k_norm_g]).astype(F32), 1)
    w0 = l0_w_in.astype(BF16)
    w_in0 = jnp.concatenate([_deinterleave(w0[:, :hq], 2), _deinterleave(w0[:, hq:2 * hq], 1),
                             _deinterleave(w0[:, 2 * hq:3 * hq], 2), w0[:, 3 * hq:4 * hq],
                             _deinterleave(w0[:, 4 * hq:4 * hq + 2 * HEAD_DIM], 1),
                             w0[:, 4 * hq + 2 * HEAD_DIM:]], axis=1)
    p0 = _inproj(x2d, s, l0_norm1_g, lat(mod0, 1), lat(mod0, 0), w_in0, cos_tab, sin_tab, qkg0,
                 jnp.asarray(types0), jnp.asarray(tabsel0), tm)
    p0c = _inproj(xc2d, n_ctx, l0_norm1_g, cx(mod0, 1), cx(mod0, 0), w_in0, cos_tab, sin_tab, qkg0,
                  jnp.asarray(types0 & ~T_ROPE), jnp.asarray(tabsel0), n_ctx)
    lam = jnp.stack([l0_lam_q1, l0_lam_k1, l0_lam_q2, l0_lam_k2]).astype(F32)
    subg = l0_subln_g.reshape(1, HEAD_DIM).astype(F32)
    tk = min(2048, s)
    oa = _flash("diff", p0, p0, p0c, b, 512, tk, 0, 16, 24, lam, subg)
    ob = _flash("gqa", p0, p0, p0c, b, 256, tk, 8, 32, 34)
    oac = _flash("diff", p0c, p0c, None, b, n_ctx, n_ctx, 0, 16, 24, lam, subg)
    obc = _flash("gqa", p0c, p0c, None, b, n_ctx, n_ctx, 8, 32, 34)
    w_out0 = l0_w_out.astype(BF16)
    x1 = _outproj(oa, ob, w_out0, x2d, lat(mod0, 2), s, tm)
    xc1 = _outproj(oac, obc, w_out0, xc2d, cx(mod0, 2), n_ctx, n_ctx)
    wg0, wu0, wd0 = l0_ffn_w_gate.astype(BF16), l0_ffn_w_up.astype(BF16), l0_ffn_w_down.astype(BF16)
    x2 = _ffn(x1, s, l0_norm2_g, lat(mod0, 4), lat(mod0, 3), lat(mod0, 5), wg0, wu0, wd0, tm)
    xc2 = _ffn(xc1, n_ctx, l0_norm2_g, cx(mod0, 4), cx(mod0, 3), cx(mod0, 5), wg0, wu0, wd0, n_ctx)

    types1 = np.array([T_ROPE | T_SCALE_B] * 8 + [T_SCALE_B] * 8 + [T_ROPE] * 2 + [0] * 2 + [0] * 16, np.int32)
    tabsel1 = np.array([1] * 9, np.int32)
    w1 = l1_w_in.astype(BF16)
    w_in1 = jnp.concatenate([_deinterleave(w1[:, :hq], 1), w1[:, hq:2 * hq],
                             _deinterleave(w1[:, 2 * hq:2 * hq + 2 * HEAD_DIM], 1),
                             w1[:, 2 * hq + 2 * HEAD_DIM:]], axis=1)
    qkg1 = jnp.ones((2, HEAD_DIM), F32)
    p1 = _inproj(x2, s, l1_norm1_g, lat(mod1, 1), lat(mod1, 0), w_in1, cos_tab, sin_tab, qkg1,
                 jnp.asarray(types1), jnp.asarray(tabsel1), tm)
    l1_q = 16 * HEAD_DIM
    n_kv_tiles = (w_in1.shape[1] - l1_q) // 512
    p1c = _inproj(xc2, n_ctx, l1_norm1_g, cx(mod1, 1), cx(mod1, 0), w_in1[:, l1_q:], cos_tab, sin_tab, qkg1,
                  jnp.zeros((n_kv_tiles * 4,), jnp.int32), jnp.zeros((n_kv_tiles,), jnp.int32), n_ctx)
    w_st, w_var, w_bias = _window_tables(s)
    n_st, n_var, n_bias = _na_tables(s, l1_rpb)
    sinks = l1_sinks.astype(F32)
    oc = _banded("window", p1, p1c, b, jnp.asarray(w_st), jnp.asarray(w_var), w_bias, sinks, 0, 16, 18, 0, 2)
    od = _banded("na", p1, p1c, b, jnp.asarray(n_st), jnp.asarray(n_var), n_bias, sinks, 8, 20, 28, 4, 12)
    x3 = _outproj(oc, od, l1_w_out.astype(BF16), x2, lat(mod1, 2), s, tm)

    wr_pad = jnp.zeros((d, LANES), F32).at[:, :n_exp].set(l1_router.astype(F32))
    h2, route_i, route_g = _router(x3, s, l1_norm2_g, lat(mod1, 4), lat(mod1, 3), wr_pad, n_exp, tm)
    route_g = route_g[:n]
    e_flat = route_i[:n, :TOP_K].reshape(-1)
    onehot = (e_flat[:, None] == jnp.arange(n_exp, dtype=jnp.int32)[None, :]).astype(jnp.int32)
    csum = jnp.cumsum(onehot, axis=0)
    rank = jnp.sum((csum - onehot) * onehot, axis=1)
    counts = csum[-1]
    tmo = 2 * tm
    padded = (counts + tmo - 1) // tmo * tmo
    pad_end = jnp.cumsum(padded)
    pad_start = pad_end - padded
    slot = (pad_start[e_flat] + rank).astype(jnp.int32)
    n_blocks = (n * TOP_K + n_exp * (tmo - 1) + tmo - 1) // tmo
    cap = n_blocks * tmo
    tok_flat = jnp.repeat(jnp.arange(n, dtype=jnp.int32), TOP_K)
    slot_tok = jnp.full((cap,), n, jnp.int32).at[slot].set(tok_flat)
    block_row0 = jnp.arange(n_blocks, dtype=jnp.int32) * tmo
    block_expert = jnp.minimum(jnp.searchsorted(pad_end, block_row0, side="right"), n_exp - 1).astype(jnp.int32)
    n_valid = jnp.clip((pad_start + counts)[block_expert] - block_row0, 0, tmo).astype(jnp.int32)

    y = _moe(slot_tok, h2, block_expert, n_valid, l1_exp_w_gate, l1_exp_w_up, l1_exp_w_down, tmo)
    out = _combine(slot, x3, s, lat(mod1, 5), route_g, final_norm_g, y, 256)
    return out.reshape(b, s, d)
```

```python
import functools
import math

import numpy as np
import jax
import jax.numpy as jnp
from jax import lax
from jax.experimental import pallas as pl
from jax.experimental.pallas import tpu as pltpu

F32 = jnp.float32
BF16 = jnp.bfloat16

HEAD_DIM = 128
LANES = 128
GRID_W = 64
ROPE_BASE = 10000.0
EPS = 1e-6
NEG_INF = -1e30
A_SUB = HEAD_DIM // 2
A_SCALE = A_SUB ** -0.5
ATT_SCALE = HEAD_DIM ** -0.5
LAMBDA_INIT_L0 = 0.8 - 0.6 * math.exp(-0.3 * 0)
LOG2E = 1.4426950408889634
C_WINDOW = 128
NA_ROWS = 8
NA_COLS = 16
TOP_K = 2
VMEM_LIMIT = 56 * 1024 * 1024
T_ROPE, T_QNORM, T_KNORM, T_SCALE_B, T_SCALE_A = 1, 2, 4, 8, 16


def _cparams(sem):
    return pltpu.CompilerParams(dimension_semantics=sem, vmem_limit_bytes=VMEM_LIMIT)


def _modulate(xf, g, scale, shift):
    ms = jnp.mean(xf * xf, axis=-1, keepdims=True)
    y = xf * lax.rsqrt(ms + EPS) * g
    return y * (1.0 + scale) + shift


def _ada_kernel(c_ref, w_ref, b_ref, o_ref):
    c = c_ref[...]
    a = c * (1.0 / (1.0 + jnp.exp(-c)))
    o_ref[...] = jnp.dot(a.astype(BF16), w_ref[...].astype(BF16), preferred_element_type=F32) + b_ref[...]


def _ada(cvecs, w_ada, b_ada):
    r, d = cvecs.shape
    n = w_ada.shape[1]
    tn = 1024
    return pl.pallas_call(
        _ada_kernel,
        out_shape=jax.ShapeDtypeStruct((r, n), F32),
        grid=(n // tn,),
        in_specs=[pl.BlockSpec((r, d), lambda j: (0, 0)),
                  pl.BlockSpec((d, tn), lambda j: (0, j)),
                  pl.BlockSpec((1, tn), lambda j: (0, j))],
        out_specs=pl.BlockSpec((r, tn), lambda j: (0, j)),
        compiler_params=_cparams(("parallel",)),
        name="ada",
    )(cvecs, w_ada, b_ada.reshape(1, n))


def _inproj_kernel(x_ref, g_ref, sc_ref, sh_ref, w_ref, cos_ref, sin_ref, qkg_ref, o_ref, h_scr, *, types, tabs):
    h_scr[...] = _modulate(x_ref[...], g_ref[...], sc_ref[...], sh_ref[...]).astype(BF16)
    chunk = 4 * HEAD_DIM
    for c in range(len(types) // 4):
        acc = jnp.dot(h_scr[...], w_ref[:, c * chunk:(c + 1) * chunk], preferred_element_type=F32)
        for hh in range(4):
            t = types[4 * c + hh]
            y = acc[:, hh * HEAD_DIM:(hh + 1) * HEAD_DIM]
            if t & (T_QNORM | T_KNORM):
                ms = jnp.mean(y * y, axis=-1, keepdims=True)
                row = 0 if t & T_QNORM else 1
                y = y * lax.rsqrt(ms + EPS) * qkg_ref[row:row + 1, :]
            if t & T_ROPE:
                tb = tabs[4 * c + hh]
                y = y * cos_ref[tb] + pltpu.roll(y, HEAD_DIM // 2, 1) * sin_ref[tb]
            if t & T_SCALE_B:
                y = y * (ATT_SCALE * LOG2E)
            elif t & T_SCALE_A:
                y = y * (A_SCALE * LOG2E)
            col = (4 * c + hh) * HEAD_DIM
            o_ref[:, col:col + HEAD_DIM] = y.astype(o_ref.dtype)


def _inproj(x2d, rows_per_batch, g, scale, shift, w_bf, cos_tab, sin_tab, qkg, types, tabs, tm):
    m, d = x2d.shape
    n = w_bf.shape[1]
    tiles_per_batch = rows_per_batch // tm
    tab_rows = cos_tab.shape[1] // tm
    kern = functools.partial(_inproj_kernel, types=tuple(int(t) for t in types), tabs=tuple(int(t) for t in tabs))
    return pl.pallas_call(
        kern,
        out_shape=jax.ShapeDtypeStruct((m, n), BF16),
        grid=(m // tm,),
        in_specs=[
            pl.BlockSpec((tm, d), lambda i: (i, 0)),
            pl.BlockSpec((1, d), lambda i: (0, 0)),
            pl.BlockSpec((None, 1, d), lambda i: (i // tiles_per_batch, 0, 0)),
            pl.BlockSpec((None, 1, d), lambda i: (i // tiles_per_batch, 0, 0)),
            pl.BlockSpec((d, n), lambda i: (0, 0), pipeline_mode=pl.Buffered(1)),
            pl.BlockSpec((2, tm, LANES), lambda i: (0, i % tab_rows, 0)),
            pl.BlockSpec((2, tm, LANES), lambda i: (0, i % tab_rows, 0)),
            pl.BlockSpec((2, HEAD_DIM), lambda i: (0, 0)),
        ],
        out_specs=pl.BlockSpec((tm, n), lambda i: (i, 0)),
        scratch_shapes=[pltpu.VMEM((tm, d), BF16)],
        compiler_params=_cparams(("parallel",)),
        name="inproj",
    )(x2d, g.reshape(1, d), scale, shift, w_bf, cos_tab, sin_tab, qkg)


def _flash_kernel(*refs, mode, tq, has_ctx, tk):
    refs = list(refs)
    q_ref, k_ref, v_ref = refs[:3]
    pos = 3
    if has_ctx:
        kc_ref, vc_ref = refs[3:5]
        pos = 5
    if mode == "diff":
        lam_ref, subg_ref = refs[pos:pos + 2]
        pos += 2
    o_ref, qs = refs[pos:pos + 2]

    q = q_ref[...]
    if mode == "diff":
        lane = lax.broadcasted_iota(jnp.int32, (1, LANES), 1)
        zero = jnp.zeros_like(q)
        first = (lane & (A_SUB // 2)) == 0
        qs[0:tq, :] = jnp.where(first, q, zero)
        qs[tq:2 * tq, :] = jnp.where(first, zero, q)
    else:
        for gi in range(4):
            qs[gi * tq:(gi + 1) * tq, :] = q[:, gi * HEAD_DIM:(gi + 1) * HEAD_DIM]

    def step(carry, k, v):
        s = lax.dot_general(qs[...], k, (((1,), (1,)), ((), ())), preferred_element_type=F32)
        slabs = [s[:, c * LANES:(c + 1) * LANES] for c in range(k.shape[0] // LANES)]
        mx = functools.reduce(jnp.maximum, slabs)
        row_max = jnp.broadcast_to(jnp.max(mx, axis=-1, keepdims=True), mx.shape)
        m_new = row_max if carry is None else jnp.maximum(carry[0], row_max)
        ps = [jnp.exp2(sl - m_new) for sl in slabs]
        p = jnp.concatenate([x.astype(BF16) for x in ps], axis=1)
        l_new = functools.reduce(jnp.add, ps)
        acc_new = jnp.dot(p, v, preferred_element_type=F32)
        if carry is not None:
            alpha = jnp.exp2(carry[0] - m_new)
            l_new = alpha * carry[1] + l_new
            acc_new = alpha * carry[2] + acc_new
        return m_new, l_new, acc_new

    carry = None
    for c in range(k_ref.shape[0] // tk):
        carry = step(carry, k_ref[c * tk:(c + 1) * tk, :], v_ref[c * tk:(c + 1) * tk, :])
    if has_ctx:
        carry = step(carry, kc_ref[...], vc_ref[...])
    _, l_fin, acc_fin = carry
    o = acc_fin / jnp.sum(l_fin, axis=-1, keepdims=True)
    if mode == "diff":
        lam1 = jnp.sum(lam_ref[0:1, :] * lam_ref[1:2, :], axis=-1, keepdims=True)
        lam2 = jnp.sum(lam_ref[2:3, :] * lam_ref[3:4, :], axis=-1, keepdims=True)
        lam = jnp.exp(lam1) - jnp.exp(lam2) + LAMBDA_INIT_L0
        od = o[0:tq, :] - lam * o[tq:2 * tq, :]
        ms = jnp.mean(od * od, axis=-1, keepdims=True)
        on = od * lax.rsqrt(ms + EPS) * subg_ref[...]
        o_ref[...] = (on * (1.0 - LAMBDA_INIT_L0)).astype(o_ref.dtype)
    else:
        for gi in range(4):
            o_ref[:, gi * HEAD_DIM:(gi + 1) * HEAD_DIM] = o[gi * tq:(gi + 1) * tq, :].astype(o_ref.dtype)


def _flash(mode, q_arr, kv_arr, ctx_arr, batch, tq, tk, qcol, kcol, vcol, lam=None, subg=None):
    mq = q_arr.shape[0]
    sq = mq // batch
    sk = kv_arr.shape[0] // batch
    nq = sq // tq
    has_ctx = ctx_arr is not None
    if mode == "diff":
        nh, qw, reps = 8, HEAD_DIM, 2
        q_spec = pl.BlockSpec((tq, qw), lambda b, h, qi: (b * nq + qi, qcol + h))
        o_spec = pl.BlockSpec((tq, qw), lambda b, h, qi: (b * nq + qi, h))
    else:
        nh, qw, reps = 2, 4 * HEAD_DIM, 4
        q_spec = pl.BlockSpec((tq, qw), lambda b, h, qi: (b * nq + qi, qcol // 4 + h))
        o_spec = pl.BlockSpec((tq, qw), lambda b, h, qi: (b * nq + qi, h))
    in_specs = [q_spec,
                pl.BlockSpec((sk, HEAD_DIM), lambda b, h, qi: (b, kcol + h)),
                pl.BlockSpec((sk, HEAD_DIM), lambda b, h, qi: (b, vcol + h))]
    args = [q_arr, kv_arr, kv_arr]
    if has_ctx:
        nc = ctx_arr.shape[0] // batch
        in_specs += [pl.BlockSpec((nc, HEAD_DIM), lambda b, h, qi: (b, kcol + h)),
                     pl.BlockSpec((nc, HEAD_DIM), lambda b, h, qi: (b, vcol + h))]
        args += [ctx_arr, ctx_arr]
    if mode == "diff":
        in_specs += [pl.BlockSpec((4, A_SUB), lambda b, h, qi: (0, 0)),
                     pl.BlockSpec((1, HEAD_DIM), lambda b, h, qi: (0, 0))]
        args += [lam, subg]
    kern = functools.partial(_flash_kernel, mode=mode, tq=tq, has_ctx=has_ctx, tk=tk)
    return pl.pallas_call(
        kern,
        out_shape=jax.ShapeDtypeStruct((mq, 8 * HEAD_DIM), BF16),
        grid=(batch, nh, nq),
        in_specs=in_specs,
        out_specs=o_spec,
        scratch_shapes=[pltpu.VMEM((reps * tq, HEAD_DIM), BF16)],
        compiler_params=_cparams(("parallel", "parallel", "parallel")),
        name="flash_" + mode,
    )(*args)


def _banded_kernel(start_ref, var_ref, q_ref, k_ref, v_ref, kc_ref, vc_ref, bias_ref, sink_ref, o_ref,
                   *, mode, tq, band):
    del var_ref
    hg = pl.program_id(1)
    j = pl.program_id(2)
    start = pl.multiple_of(start_ref[j], LANES)
    kb = k_ref[pl.ds(start, band), :]
    vb = v_ref[pl.ds(start, band), :]
    kc = kc_ref[...]
    vc = vc_ref[...]
    q = q_ref[...]
    dn = (((1,), (1,)), ((), ()))
    heads = [slice(gi * HEAD_DIM, (gi + 1) * HEAD_DIM) for gi in range(4)]
    if mode == "window":
        qs = jnp.concatenate([q[:, hd] for hd in heads], axis=0)
        s_loc = lax.dot_general(qs, kb, dn, preferred_element_type=F32)
        s_ctx = lax.dot_general(qs, kc, dn, preferred_element_type=F32)
        bias = jnp.concatenate([bias_ref[...]] * 4, axis=0)
    else:
        s_loc = jnp.concatenate([lax.dot_general(q[:, hd], kb[:, hd], dn, preferred_element_type=F32)
                                 for hd in heads], axis=0)
        s_ctx = jnp.concatenate([lax.dot_general(q[:, hd], kc[:, hd], dn, preferred_element_type=F32)
                                 for hd in heads], axis=0)
        bias = bias_ref[...]
    s_loc = s_loc + bias
    n_loc = band // LANES
    slabs = ([s_loc[:, c * LANES:(c + 1) * LANES] for c in range(n_loc)]
             + [s_ctx[:, c * LANES:(c + 1) * LANES] for c in range(s_ctx.shape[1] // LANES)])
    mx = functools.reduce(jnp.maximum, slabs)
    m = jnp.broadcast_to(jnp.max(mx, axis=-1, keepdims=True), mx.shape)
    if mode == "window":
        sink = jnp.concatenate([jnp.full((tq, LANES), sink_ref[hg * 4 + gi] * LOG2E, F32) for gi in range(4)], axis=0)
        m = jnp.maximum(m, sink)
    ps = [jnp.exp2(sl - m) for sl in slabs]
    l = jnp.sum(functools.reduce(jnp.add, ps), axis=-1, keepdims=True)
    if mode == "window":
        l = l + jnp.exp2(sink - m)[:, 0:1]
    p_loc = jnp.concatenate([x.astype(BF16) for x in ps[:n_loc]], axis=1)
    p_ctx = jnp.concatenate([x.astype(BF16) for x in ps[n_loc:]], axis=1)
    if mode == "window":
        o = jnp.dot(p_loc, vb, preferred_element_type=F32) + jnp.dot(p_ctx, vc, preferred_element_type=F32)
        o = (o / l).astype(o_ref.dtype)
        for gi in range(4):
            o_ref[:, heads[gi]] = o[gi * tq:(gi + 1) * tq, :]
    else:
        for gi in range(4):
            rows = slice(gi * tq, (gi + 1) * tq)
            o = (jnp.dot(p_loc[rows, :], vb[:, heads[gi]], preferred_element_type=F32)
                 + jnp.dot(p_ctx[rows, :], vc[:, heads[gi]], preferred_element_type=F32))
            o_ref[:, heads[gi]] = (o / l[rows, :]).astype(o_ref.dtype)


def _banded(mode, proj, proj_ctx, batch, starts, variants, bias, sinks, qcol, kcol, vcol, kc_col, vc_col):
    m = proj.shape[0]
    s = m // batch
    tq = 128
    nq = s // tq
    nc = proj_ctx.shape[0] // batch
    band = bias.shape[-1]
    qw = 4 * HEAD_DIM
    if mode == "window":
        kw, div = HEAD_DIM, 1
        bias_spec = pl.BlockSpec((None, tq, band), lambda b, h, j, st, va: (va[j], 0, 0))
    else:
        kw, div = qw, 4
        bias_spec = pl.BlockSpec((None, None, 4 * tq, band), lambda b, h, j, st, va: (h, va[j], 0, 0))
    grid_spec = pltpu.PrefetchScalarGridSpec(
        num_scalar_prefetch=2,
        grid=(batch, 2, nq),
        in_specs=[
            pl.BlockSpec((tq, qw), lambda b, h, j, st, va: (b * nq + j, qcol // 4 + h)),
            pl.BlockSpec((s, kw), lambda b, h, j, st, va: (b, kcol // div + h)),
            pl.BlockSpec((s, kw), lambda b, h, j, st, va: (b, vcol // div + h)),
            pl.BlockSpec((nc, kw), lambda b, h, j, st, va: (b, kc_col // div + h)),
            pl.BlockSpec((nc, kw), lambda b, h, j, st, va: (b, vc_col // div + h)),
            bias_spec,
            pl.BlockSpec(memory_space=pltpu.SMEM),
        ],
        out_specs=pl.BlockSpec((tq, qw), lambda b, h, j, st, va: (b * nq + j, h)),
    )
    kern = functools.partial(_banded_kernel, mode=mode, tq=tq, band=band)
    return pl.pallas_call(
        kern,
        out_shape=jax.ShapeDtypeStruct((m, 8 * HEAD_DIM), BF16),
        grid_spec=grid_spec,
        compiler_params=_cparams(("parallel", "parallel", "arbitrary")),
        name="banded_" + mode,
    )(starts, variants, proj, proj, proj, proj_ctx, proj_ctx, bias, sinks)


def _outproj_kernel(oa_ref, ob_ref, wa_ref, wb_ref, x_ref, gt_ref, o_ref):
    acc = jnp.dot(oa_ref[...], wa_ref[...], preferred_element_type=F32)
    acc = acc + jnp.dot(ob_ref[...], wb_ref[...], preferred_element_type=F32)
    o_ref[...] = x_ref[...] + gt_ref[...] * acc


def _outproj(oa, ob, w_bf, x2d, gate, rows_per_batch, tm):
    m, d = x2d.shape
    half = oa.shape[1]
    tiles_per_batch = rows_per_batch // tm
    return pl.pallas_call(
        _outproj_kernel,
        out_shape=jax.ShapeDtypeStruct((m, d), F32),
        grid=(m // tm,),
        in_specs=[pl.BlockSpec((tm, half), lambda i: (i, 0)),
                  pl.BlockSpec((tm, half), lambda i: (i, 0)),
                  pl.BlockSpec((half, d), lambda i: (0, 0)),
                  pl.BlockSpec((half, d), lambda i: (1, 0)),
                  pl.BlockSpec((tm, d), lambda i: (i, 0)),
                  pl.BlockSpec((None, 1, d), lambda i: (i // tiles_per_batch, 0, 0))],
        out_specs=pl.BlockSpec((tm, d), lambda i: (i, 0)),
        compiler_params=_cparams(("parallel",)),
        name="outproj",
    )(oa, ob, w_bf, w_bf, x2d, gate)


def _swiglu_step(h, wg, wu, wd):
    g = jnp.dot(h, wg, preferred_element_type=F32)
    u = jnp.dot(h, wu, preferred_element_type=F32)
    a = (g * (1.0 / (1.0 + jnp.exp(-g))) * u).astype(BF16)
    return jnp.dot(a, wd, preferred_element_type=F32)


def _ffn_kernel(x_ref, g_ref, sc_ref, sh_ref, gt_ref, wg_ref, wu_ref, wd_ref, o_ref, h_scr, acc_scr, *, nf):
    j = pl.program_id(1)

    @pl.when(j == 0)
    def _():
        h_scr[...] = _modulate(x_ref[...], g_ref[...], sc_ref[...], sh_ref[...]).astype(BF16)
        acc_scr[...] = jnp.zeros_like(acc_scr)

    acc_scr[...] += _swiglu_step(h_scr[...], wg_ref[...], wu_ref[...], wd_ref[...])

    @pl.when(j == nf - 1)
    def _():
        o_ref[...] = x_ref[...] + gt_ref[...] * acc_scr[...]


def _ffn(x2d, rows_per_batch, g, scale, shift, gate, wg, wu, wd, tm):
    m, d = x2d.shape
    f = wg.shape[1]
    tf = 512
    nf = f // tf
    tiles_per_batch = rows_per_batch // tm
    vec = pl.BlockSpec((None, 1, d), lambda i, j: (i // tiles_per_batch, 0, 0))
    return pl.pallas_call(
        functools.partial(_ffn_kernel, nf=nf),
        out_shape=jax.ShapeDtypeStruct((m, d), F32),
        grid=(m // tm, nf),
        in_specs=[pl.BlockSpec((tm, d), lambda i, j: (i, 0)),
                  pl.BlockSpec((1, d), lambda i, j: (0, 0)),
                  vec, vec, vec,
                  pl.BlockSpec((d, tf), lambda i, j: (0, j)),
                  pl.BlockSpec((d, tf), lambda i, j: (0, j)),
                  pl.BlockSpec((tf, d), lambda i, j: (j, 0))],
        out_specs=pl.BlockSpec((tm, d), lambda i, j: (i, 0)),
        scratch_shapes=[pltpu.VMEM((tm, d), BF16), pltpu.VMEM((tm, d), F32)],
        compiler_params=_cparams(("parallel", "arbitrary")),
        name="ffn",
    )(x2d, g.reshape(1, d), scale, shift, gate, wg, wu, wd)


def _router_kernel(x_ref, g_ref, sc_ref, sh_ref, wr_ref, h_ref, ri_ref, rg_ref, *, n_exp, n_tiles):
    h = _modulate(x_ref[...], g_ref[...], sc_ref[...], sh_ref[...])
    h_ref[...] = jnp.where(pl.program_id(0) < n_tiles, h, 0.0)
    logits = jnp.dot(h, wr_ref[...], preferred_element_type=F32, precision=lax.Precision.HIGHEST)
    lane = lax.broadcasted_iota(jnp.int32, logits.shape, 1)
    logits = jnp.where(lane < n_exp, logits, -jnp.inf)
    lane_f = lane.astype(F32)
    m1 = jnp.max(logits, axis=-1, keepdims=True)
    i1 = jnp.min(jnp.where(logits == m1, lane_f, float(LANES)), axis=-1, keepdims=True)
    rest = jnp.where(lane_f == i1, -jnp.inf, logits)
    m2 = jnp.max(rest, axis=-1, keepdims=True)
    i2 = jnp.min(jnp.where(rest == m2, lane_f, float(LANES)), axis=-1, keepdims=True)
    e = jnp.exp(m2 - m1)
    g1 = 1.0 / (1.0 + e)
    g2 = e / (1.0 + e)
    ri_ref[...] = jnp.where(lane == 0, i1, jnp.where(lane == 1, i2, 0.0)).astype(jnp.int32)
    rg_ref[...] = jnp.where(lane == 0, g1, jnp.where(lane == 1, g2, 0.0))


def _router(x2d, rows_per_batch, g, scale, shift, wr_pad, n_exp, tm):
    m, d = x2d.shape
    tiles_per_batch = rows_per_batch // tm
    n_tiles = m // tm
    last = n_tiles - 1
    vec = pl.BlockSpec((None, 1, d), lambda i: (jnp.minimum(i, last) // tiles_per_batch, 0, 0))
    return pl.pallas_call(
        functools.partial(_router_kernel, n_exp=n_exp, n_tiles=n_tiles),
        out_shape=(jax.ShapeDtypeStruct((m + tm, d), F32),
                   jax.ShapeDtypeStruct((m + tm, LANES), jnp.int32),
                   jax.ShapeDtypeStruct((m + tm, LANES), F32)),
        grid=(n_tiles + 1,),
        in_specs=[pl.BlockSpec((tm, d), lambda i: (jnp.minimum(i, last), 0)),
                  pl.BlockSpec((1, d), lambda i: (0, 0)),
                  vec, vec,
                  pl.BlockSpec((d, LANES), lambda i: (0, 0))],
        out_specs=(pl.BlockSpec((tm, d), lambda i: (i, 0)),
                   pl.BlockSpec((tm, LANES), lambda i: (i, 0)),
                   pl.BlockSpec((tm, LANES), lambda i: (i, 0))),
        compiler_params=_cparams(("parallel",)),
        name="router",
    )(x2d, g.reshape(1, d), scale, shift, wr_pad)


def _moe_kernel(be_ref, nv_ref, tok_ref, h_hbm, wg_ref, wu_ref, wd_ref, o_ref, xbuf, xb_scr, sem,
                *, nf, tm, nblk, per_step):
    del be_ref
    i = pl.program_id(0)
    j = pl.program_id(1)
    n_valid = nv_ref[i]
    half = tm // 2
    n_copies = nf * per_step

    def start_row(blk, r):
        tok = tok_ref[blk * tm + jnp.minimum(r, tm - 1)]
        pltpu.async_copy(h_hbm.at[pl.ds(tok, 1)], xbuf.at[pl.ds(r, 1)], sem, priority=1)

    def wait_all():
        pltpu.make_async_copy(h_hbm.at[pl.ds(0, n_copies)], xbuf.at[pl.ds(0, n_copies)], sem).wait()

    def start_next():
        nxt = jnp.minimum(i + 1, nblk - 1)
        for u in range(per_step):
            start_row(nxt, j * per_step + u)

    @pl.when(j == 0)
    def _():
        @pl.when(i == 0)
        def _():
            def body(r, carry):
                start_row(0, r)
                return carry
            lax.fori_loop(0, n_copies, body, 0)

        wait_all()
        xb_scr[...] = xbuf[0:tm, :].astype(BF16)
        o_ref[...] = jnp.zeros_like(o_ref)

    def accumulate(rows):
        start_next()
        o_ref[rows, :] += _swiglu_step(xb_scr[rows, :], wg_ref[...].astype(BF16), wu_ref[...].astype(BF16),
                                       wd_ref[...].astype(BF16))

    @pl.when(n_valid > half)
    def _():
        accumulate(slice(0, tm))

    @pl.when(jnp.logical_and(n_valid > 0, n_valid <= half))
    def _():
        accumulate(slice(0, half))

    @pl.when(n_valid == 0)
    def _():
        start_next()

    @pl.when(jnp.logical_and(i == nblk - 1, j == nf - 1))
    def _():
        wait_all()


def _moe(slot_tok, h_pad, block_expert, n_valid, wg, wu, wd, tm):
    cap = slot_tok.shape[0]
    d = h_pad.shape[1]
    f = wg.shape[2]
    tf = 256
    nf = f // tf
    nblk = cap // tm
    per_step = -(-tm // nf)
    while (nf * per_step) % 8:
        per_step += 1
    buf_rows = nf * per_step

    def col_j(i, j, nv):
        return jnp.where(nv[i] > 0, j, nf - 1)

    grid_spec = pltpu.PrefetchScalarGridSpec(
        num_scalar_prefetch=3,
        grid=(nblk, nf),
        in_specs=[pl.BlockSpec(memory_space=pl.ANY),
                  pl.BlockSpec((None, d, tf), lambda i, j, be, nv, tok: (be[i], 0, col_j(i, j, nv))),
                  pl.BlockSpec((None, d, tf), lambda i, j, be, nv, tok: (be[i], 0, col_j(i, j, nv))),
                  pl.BlockSpec((None, tf, d), lambda i, j, be, nv, tok: (be[i], col_j(i, j, nv), 0))],
        out_specs=pl.BlockSpec((tm, d), lambda i, j, be, nv, tok: (i, 0)),
        scratch_shapes=[pltpu.VMEM((buf_rows, d), F32), pltpu.VMEM((tm, d), BF16), pltpu.SemaphoreType.DMA],
    )
    return pl.pallas_call(
        functools.partial(_moe_kernel, nf=nf, tm=tm, nblk=nblk, per_step=per_step),
        out_shape=jax.ShapeDtypeStruct((cap, d), F32),
        grid_spec=grid_spec,
        compiler_params=_cparams(("arbitrary", "arbitrary")),
        name="moe",
    )(block_expert, n_valid, slot_tok, h_pad, wg, wu, wd)


def _combine_kernel(slot_ref, x_ref, gt_ref, rg_ref, fg_ref, y_hbm, o_ref, buf, sem, *, tc):
    base = pl.program_id(0) * tc

    def issue(r, carry):
        for k in range(TOP_K):
            s = slot_ref[TOP_K * (base + r) + k]
            pltpu.make_async_copy(y_hbm.at[pl.ds(s, 1)], buf.at[pl.ds(k * tc + r, 1)], sem).start()
        return carry

    lax.fori_loop(0, tc, issue, 0, unroll=8)
    pltpu.make_async_copy(y_hbm.at[pl.ds(0, TOP_K * tc)], buf, sem).wait()
    rg = rg_ref[...]
    y = rg[:, 0:1] * buf[0:tc, :] + rg[:, 1:2] * buf[tc:2 * tc, :]
    xo = x_ref[...] + gt_ref[...] * y
    ms = jnp.mean(xo * xo, axis=-1, keepdims=True)
    o_ref[...] = xo * lax.rsqrt(ms + EPS) * fg_ref[...]


def _combine(slot, x2d, rows_per_batch, gate, route_g, final_g, y, tc):
    m, d = x2d.shape
    tiles_per_batch = rows_per_batch // tc
    grid_spec = pltpu.PrefetchScalarGridSpec(
        num_scalar_prefetch=1,
        grid=(m // tc,),
        in_specs=[pl.BlockSpec((tc, d), lambda i, sl: (i, 0)),
                  pl.BlockSpec((None, 1, d), lambda i, sl: (i // tiles_per_batch, 0, 0)),
                  pl.BlockSpec((tc, LANES), lambda i, sl: (i, 0)),
                  pl.BlockSpec((1, d), lambda i, sl: (0, 0)),
                  pl.BlockSpec(memory_space=pl.ANY)],
        out_specs=pl.BlockSpec((tc, d), lambda i, sl: (i, 0)),
        scratch_shapes=[pltpu.VMEM((TOP_K * tc, d), F32), pltpu.SemaphoreType.DMA],
    )
    return pl.pallas_call(
        functools.partial(_combine_kernel, tc=tc),
        out_shape=jax.ShapeDtypeStruct((m, d), F32),
        grid_spec=grid_spec,
        compiler_params=_cparams(("arbitrary",)),
        name="combine",
    )(slot, x2d, gate, route_g, final_g.reshape(1, d), y)


def _rope_tables(n_tok, dim):
    t = jnp.arange(n_tok)
    pos = jnp.stack([t // GRID_W, t % GRID_W], axis=-1).astype(F32)
    n_pairs = dim // 4
    freq = ROPE_BASE ** (-jnp.arange(n_pairs, dtype=F32) / n_pairs)
    ang = (pos[:, :, None] * freq).reshape(n_tok, 2 * n_pairs)
    cos = jnp.tile(jnp.cos(ang), (1, HEAD_DIM // dim))
    sin = jnp.tile(jnp.sin(ang), (1, HEAD_DIM // dim))
    return jnp.concatenate([cos, cos], axis=1), jnp.concatenate([-sin, sin], axis=1)


def _deinterleave(w, comps):
    d, n = w.shape
    pairs = HEAD_DIM // (2 * comps)
    return w.reshape(d, n // HEAD_DIM, comps, pairs, 2).transpose(0, 1, 4, 2, 3).reshape(d, n)


def _dedupe(patterns):
    keys, variant = {}, []
    for p in patterns:
        variant.append(keys.setdefault(p.tobytes(), len(keys)))
    first = [variant.index(v) for v in range(len(keys))]
    return np.asarray(variant, np.int32), first


def _window_tables(s):
    tq = 128
    nb = s // tq
    band = 3 * tq
    starts = np.clip(np.arange(nb) - 1, 0, nb - 3) * tq
    qpos = np.arange(nb)[:, None] * tq + np.arange(tq)[None, :]
    kpos = starts[:, None] + np.arange(band)[None, :]
    ok = np.abs(kpos[:, None, :] - qpos[:, :, None]) <= C_WINDOW
    variant, first = _dedupe(list(ok))
    bias = np.where(ok[first], 0.0, NEG_INF).astype(np.float32)
    return starts.astype(np.int32), variant, jnp.asarray(bias)


def _na_tables(s, rpb):
    tq = 128
    rows = s // GRID_W
    wr = min(NA_ROWS, rows)
    nb = s // tq
    band_blocks = 5
    band = band_blocks * tq
    starts = np.clip(np.arange(nb) - 2, 0, nb - band_blocks) * tq
    n_h = rpb.shape[0]
    n_dc = 2 * NA_COLS - 1
    period = GRID_W + n_dc - 1
    wrap = jnp.zeros((n_h, 2 * NA_ROWS - 1, period), F32)
    wrap = wrap.at[..., :NA_COLS].set(rpb[..., NA_COLS - 1:].astype(F32))
    wrap = wrap.at[..., period - (NA_COLS - 1):].set(rpb[..., :NA_COLS - 1].astype(F32))
    toep = jnp.tile(wrap, (1, 1, GRID_W))[..., :GRID_W * (period - 1)]
    toep = toep.reshape(n_h, 2 * NA_ROWS - 1, GRID_W, period - 1)[..., :GRID_W]
    qc = np.arange(GRID_W)[:, None]
    kc = np.arange(GRID_W)[None, :]
    cstart = np.clip(qc - NA_COLS // 2, 0, GRID_W - NA_COLS)
    col_ok = (kc >= cstart) & (kc < cstart + NA_COLS)
    toep = jnp.where(jnp.asarray(col_ok), toep * LOG2E, NEG_INF)
    masked = jnp.full((n_h, GRID_W, GRID_W), NEG_INF, F32)
    pats = []
    for j in range(nb):
        pat = np.full((tq // GRID_W, band // GRID_W), -1, np.int32)
        for a in range(tq // GRID_W):
            qr = j * (tq // GRID_W) + a
            rs = min(max(qr - wr // 2, 0), rows - wr)
            for w in range(band // GRID_W):
                kr = starts[j] // GRID_W + w
                if rs <= kr < rs + wr:
                    pat[a, w] = kr - qr + NA_ROWS - 1
        pats.append(pat)
    variant, first = _dedupe(pats)
    tiles = []
    for j in first:
        rows_ = [jnp.concatenate([toep[:, dr] if dr >= 0 else masked for dr in pats[j][a]], axis=-1)
                 for a in range(tq // GRID_W)]
        tiles.append(jnp.concatenate(rows_, axis=-2))
    bias = jnp.stack(tiles, axis=1)
    bias = bias.reshape(n_h // 4, 4, len(first), tq, band).transpose(0, 2, 1, 3, 4)
    return starts.astype(np.int32), variant, bias.reshape(n_h // 4, len(first), 4 * tq, band)


def kernel(x, c, ctx, c_ctx, l0_norm1_g, l0_norm2_g, l0_w_ada, l0_b_ada, l0_w_in, l0_w_out, l0_lam_q1, l0_lam_k1, l0_lam_q2, l0_lam_k2, l0_subln_g, l0_q_norm_g, l0_k_norm_g, l0_ffn_w_gate, l0_ffn_w_up, l0_ffn_w_down, l1_norm1_g, l1_norm2_g, l1_w_ada, l1_b_ada, l1_w_in, l1_w_out, l1_sinks, l1_rpb, l1_router, l1_exp_w_gate, l1_exp_w_up, l1_exp_w_down, final_norm_g):
    b, s, d = x.shape
    n_ctx = ctx.shape[1]
    n_exp = l1_router.shape[1]
    n = b * s
    assert d == 2048 and s % 512 == 0 and n_ctx == 256 and s // GRID_W >= 10

    x2d = x.reshape(n, d)
    xc2d = ctx.reshape(b * n_ctx, d)

    cvecs = jnp.zeros((8, d), F32).at[:b].set(c).at[b].set(c_ctx)
    mod0 = _ada(cvecs, l0_w_ada, l0_b_ada)
    mod1 = _ada(cvecs, l1_w_ada, l1_b_ada)

    def lat(mod, k):
        return mod[:b, k * d:(k + 1) * d].reshape(b, 1, d)

    def cx(mod, k):
        return jnp.broadcast_to(mod[b, k * d:(k + 1) * d].reshape(1, 1, d), (b, 1, d))

    cos_a, sin_a = _rope_tables(s, A_SUB)
    cos_b, sin_b = _rope_tables(s, HEAD_DIM)
    cos_tab = jnp.stack([cos_a, cos_b])
    sin_tab = jnp.stack([sin_a, sin_b])
    tm = 512
    hq = 8 * HEAD_DIM

    qa = T_ROPE | T_SCALE_A
    qb = T_ROPE | T_QNORM | T_SCALE_B
    kb = T_ROPE | T_KNORM
    types0 = [qa] * 8 + [qb] * 8 + [T_ROPE] * 8 + [0] * 8 + [kb] * 2 + [0] * 2
    tabs0 = [0] * 8 + [1] * 8 + [0] * 8 + [0] * 8 + [1] * 2 + [0] * 2
    qkg0 = _deinterleave(jnp.stack([l0_q_norm_g, l0_k_norm_g]).astype(F32), 1)
    w0 = l0_w_in.astype(BF16)
    w_in0 = jnp.concatenate([_deinterleave(w0[:, :hq], 2), _deinterleave(w0[:, hq:2 * hq], 1),
                             _deinterleave(w0[:, 2 * hq:3 * hq], 2), w0[:, 3 * hq:4 * hq],
                             _deinterleave(w0[:, 4 * hq:4 * hq + 2 * HEAD_DIM], 1),
                             w0[:, 4 * hq + 2 * HEAD_DIM:]], axis=1)
    p0 = _inproj(x2d, s, l0_norm1_g, lat(mod0, 1), lat(mod0, 0), w_in0, cos_tab, sin_tab, qkg0,
                 types0, tabs0, tm)
    p0c = _inproj(xc2d, n_ctx, l0_norm1_g, cx(mod0, 1), cx(mod0, 0), w_in0, cos_tab, sin_tab, qkg0,
                  [t & ~T_ROPE for t in types0], tabs0, n_ctx)
    lam = jnp.stack([l0_lam_q1, l0_lam_k1, l0_lam_q2, l0_lam_k2]).astype(F32)
    subg = l0_subln_g.reshape(1, HEAD_DIM).astype(F32)
    tk = min(2048, s)
    oa = _flash("diff", p0, p0, p0c, b, 512, tk, 0, 16, 24, lam, subg)
    ob = _flash("gqa", p0, p0, p0c, b, 256, tk, 8, 32, 34)
    oac = _flash("diff", p0c, p0c, None, b, n_ctx, n_ctx, 0, 16, 24, lam, subg)
    obc = _flash("gqa", p0c, p0c, None, b, n_ctx, n_ctx, 8, 32, 34)
    w_out0 = l0_w_out.astype(BF16)
    x1 = _outproj(oa, ob, w_out0, x2d, lat(mod0, 2), s, tm)
    xc1 = _outproj(oac, obc, w_out0, xc2d, cx(mod0, 2), n_ctx, n_ctx)
    wg0, wu0, wd0 = l0_ffn_w_gate.astype(BF16), l0_ffn_w_up.astype(BF16), l0_ffn_w_down.astype(BF16)
    x2 = _ffn(x1, s, l0_norm2_g, lat(mod0, 4), lat(mod0, 3), lat(mod0, 5), wg0, wu0, wd0, tm)
    xc2 = _ffn(xc1, n_ctx, l0_norm2_g, cx(mod0, 4), cx(mod0, 3), cx(mod0, 5), wg0, wu0, wd0, n_ctx)

    types1 = [T_ROPE | T_SCALE_B] * 8 + [T_SCALE_B] * 8 + [T_ROPE] * 2 + [0] * 2 + [0] * 16
    tabs1 = [1] * len(types1)
    w1 = l1_w_in.astype(BF16)
    w_in1 = jnp.concatenate([_deinterleave(w1[:, :hq], 1), w1[:, hq:2 * hq],
                             _deinterleave(w1[:, 2 * hq:2 * hq + 2 * HEAD_DIM], 1),
                             w1[:, 2 * hq + 2 * HEAD_DIM:]], axis=1)
    qkg1 = jnp.ones((2, HEAD_DIM), F32)
    p1 = _inproj(x2, s, l1_norm1_g, lat(mod1, 1), lat(mod1, 0), w_in1, cos_tab, sin_tab, qkg1,
                 types1, tabs1, tm)
    l1_q = 16 * HEAD_DIM
    n_kv_heads = (w_in1.shape[1] - l1_q) // HEAD_DIM
    p1c = _inproj(xc2, n_ctx, l1_norm1_g, cx(mod1, 1), cx(mod1, 0), w_in1[:, l1_q:], cos_tab, sin_tab, qkg1,
                  [0] * n_kv_heads, [0] * n_kv_heads, n_ctx)
    w_st, w_var, w_bias = _window_tables(s)
    n_st, n_var, n_bias = _na_tables(s, l1_rpb)
    sinks = l1_sinks.astype(F32)
    oc = _banded("window", p1, p1c, b, jnp.asarray(w_st), jnp.asarray(w_var), w_bias, sinks, 0, 16, 18, 0, 2)
    od = _banded("na", p1, p1c, b, jnp.asarray(n_st), jnp.asarray(n_var), n_bias, sinks, 8, 20, 28, 4, 12)
    x3 = _outproj(oc, od, l1_w_out.astype(BF16), x2, lat(mod1, 2), s, tm)

    wr_pad = jnp.zeros((d, LANES), F32).at[:, :n_exp].set(l1_router.astype(F32))
    h2, route_i, route_g = _router(x3, s, l1_norm2_g, lat(mod1, 4), lat(mod1, 3), wr_pad, n_exp, tm)
    route_g = route_g[:n]
    e_flat = route_i[:n, :TOP_K].reshape(-1)
    onehot = (e_flat[:, None] == jnp.arange(n_exp, dtype=jnp.int32)[None, :]).astype(jnp.int32)
    csum = jnp.cumsum(onehot, axis=0)
    rank = jnp.sum((csum - onehot) * onehot, axis=1)
    counts = csum[-1]
    tmo = 2 * tm
    padded = (counts + tmo - 1) // tmo * tmo
    pad_end = jnp.cumsum(padded)
    pad_start = pad_end - padded
    slot = (pad_start[e_flat] + rank).astype(jnp.int32)
    n_blocks = (n * TOP_K + n_exp * (tmo - 1) + tmo - 1) // tmo
    cap = n_blocks * tmo
    tok_flat = jnp.repeat(jnp.arange(n, dtype=jnp.int32), TOP_K)
    slot_tok = jnp.full((cap,), n, jnp.int32).at[slot].set(tok_flat)
    block_row0 = jnp.arange(n_blocks, dtype=jnp.int32) * tmo
    block_expert = jnp.minimum(jnp.searchsorted(pad_end, block_row0, side="right"), n_exp - 1).astype(jnp.int32)
    n_valid = jnp.clip((pad_start + counts)[block_expert] - block_row0, 0, tmo).astype(jnp.int32)

    y = _moe(slot_tok, h2, block_expert, n_valid, l1_exp_w_gate, l1_exp_w_up, l1_exp_w_down, tmo)
    out = _combine(slot, x3, s, lat(mod1, 5), route_g, final_norm_g, y, 256)
    return out.reshape(b, s, d)
```

```python
import functools
import math

import numpy as np
import jax
import jax.numpy as jnp
from jax import lax
from jax.experimental import pallas as pl
from jax.experimental.pallas import tpu as pltpu

F32 = jnp.float32
BF16 = jnp.bfloat16

HEAD_DIM = 128
LANES = 128
GRID_W = 64
ROPE_BASE = 10000.0
EPS = 1e-6
NEG_INF = -1e30
A_SUB = HEAD_DIM // 2
A_SCALE = A_SUB ** -0.5
ATT_SCALE = HEAD_DIM ** -0.5
LAMBDA_INIT_L0 = 0.8 - 0.6 * math.exp(-0.3 * 0)
LOG2E = 1.4426950408889634
C_WINDOW = 128
NA_ROWS = 8
NA_COLS = 16
TOP_K = 2
VMEM_LIMIT = 56 * 1024 * 1024
T_ROPE, T_QNORM, T_KNORM, T_SCALE_B, T_SCALE_A = 1, 2, 4, 8, 16


def _cparams(sem):
    return pltpu.CompilerParams(dimension_semantics=sem, vmem_limit_bytes=VMEM_LIMIT)


def _modulate(xf, g, scale, shift):
    ms = jnp.mean(xf * xf, axis=-1, keepdims=True)
    y = xf * lax.rsqrt(ms + EPS) * g
    return y * (1.0 + scale) + shift


def _ada_kernel(c_ref, w_ref, b_ref, o_ref):
    c = c_ref[...]
    a = c * (1.0 / (1.0 + jnp.exp(-c)))
    o_ref[...] = jnp.dot(a.astype(BF16), w_ref[...].astype(BF16), preferred_element_type=F32) + b_ref[...]


def _ada(cvecs, w_ada, b_ada):
    r, d = cvecs.shape
    n = w_ada.shape[1]
    tn = 1024
    return pl.pallas_call(
        _ada_kernel,
        out_shape=jax.ShapeDtypeStruct((r, n), F32),
        grid=(n // tn,),
        in_specs=[pl.BlockSpec((r, d), lambda j: (0, 0)),
                  pl.BlockSpec((d, tn), lambda j: (0, j)),
                  pl.BlockSpec((1, tn), lambda j: (0, j))],
        out_specs=pl.BlockSpec((r, tn), lambda j: (0, j)),
        compiler_params=_cparams(("parallel",)),
        name="ada",
    )(cvecs, w_ada, b_ada.reshape(1, n))


def _inproj_kernel(x_ref, g_ref, sc_ref, sh_ref, w_ref, cos_ref, sin_ref, qkg_ref, o_ref, h_scr, *, types, tabs):
    h_scr[...] = _modulate(x_ref[...], g_ref[...], sc_ref[...], sh_ref[...]).astype(BF16)
    chunk = 4 * HEAD_DIM
    for c in range(len(types) // 4):
        acc = jnp.dot(h_scr[...], w_ref[:, c * chunk:(c + 1) * chunk], preferred_element_type=F32)
        for hh in range(4):
            t = types[4 * c + hh]
            y = acc[:, hh * HEAD_DIM:(hh + 1) * HEAD_DIM]
            if t & (T_QNORM | T_KNORM):
                ms = jnp.mean(y * y, axis=-1, keepdims=True)
                row = 0 if t & T_QNORM else 1
                y = y * lax.rsqrt(ms + EPS) * qkg_ref[row:row + 1, :]
            if t & T_ROPE:
                tb = tabs[4 * c + hh]
                y = y * cos_ref[tb] + pltpu.roll(y, HEAD_DIM // 2, 1) * sin_ref[tb]
            if t & T_SCALE_B:
                y = y * (ATT_SCALE * LOG2E)
            elif t & T_SCALE_A:
                y = y * (A_SCALE * LOG2E)
            col = (4 * c + hh) * HEAD_DIM
            o_ref[:, col:col + HEAD_DIM] = y.astype(o_ref.dtype)


def _inproj(x2d, rows_per_batch, g, scale, shift, w_bf, cos_tab, sin_tab, qkg, types, tabs, tm):
    m, d = x2d.shape
    n = w_bf.shape[1]
    tiles_per_batch = rows_per_batch // tm
    tab_rows = cos_tab.shape[1] // tm
    kern = functools.partial(_inproj_kernel, types=tuple(int(t) for t in types), tabs=tuple(int(t) for t in tabs))
    return pl.pallas_call(
        kern,
        out_shape=jax.ShapeDtypeStruct((m, n), BF16),
        grid=(m // tm,),
        in_specs=[
            pl.BlockSpec((tm, d), lambda i: (i, 0)),
            pl.BlockSpec((1, d), lambda i: (0, 0)),
            pl.BlockSpec((None, 1, d), lambda i: (i // tiles_per_batch, 0, 0)),
            pl.BlockSpec((None, 1, d), lambda i: (i // tiles_per_batch, 0, 0)),
            pl.BlockSpec((d, n), lambda i: (0, 0), pipeline_mode=pl.Buffered(1)),
            pl.BlockSpec((2, tm, LANES), lambda i: (0, i % tab_rows, 0)),
            pl.BlockSpec((2, tm, LANES), lambda i: (0, i % tab_rows, 0)),
            pl.BlockSpec((2, HEAD_DIM), lambda i: (0, 0)),
        ],
        out_specs=pl.BlockSpec((tm, n), lambda i: (i, 0)),
        scratch_shapes=[pltpu.VMEM((tm, d), BF16)],
        compiler_params=_cparams(("parallel",)),
        name="inproj",
    )(x2d, g.reshape(1, d), scale, shift, w_bf, cos_tab, sin_tab, qkg)


def _flash_kernel(*refs, mode, tq, has_ctx, tk):
    refs = list(refs)
    q_ref, k_ref, v_ref = refs[:3]
    pos = 3
    if has_ctx:
        kc_ref, vc_ref = refs[3:5]
        pos = 5
    if mode == "diff":
        lam_ref, subg_ref = refs[pos:pos + 2]
        pos += 2
    o_ref, qs = refs[pos:pos + 2]

    q = q_ref[...]
    if mode == "diff":
        lane = lax.broadcasted_iota(jnp.int32, (1, LANES), 1)
        zero = jnp.zeros_like(q)
        first = (lane & (A_SUB // 2)) == 0
        qs[0:tq, :] = jnp.where(first, q, zero)
        qs[tq:2 * tq, :] = jnp.where(first, zero, q)
    else:
        for gi in range(4):
            qs[gi * tq:(gi + 1) * tq, :] = q[:, gi * HEAD_DIM:(gi + 1) * HEAD_DIM]

    def step(carry, k, v):
        s = lax.dot_general(qs[...], k, (((1,), (1,)), ((), ())), preferred_element_type=F32)
        slabs = [s[:, c * LANES:(c + 1) * LANES] for c in range(k.shape[0] // LANES)]
        mx = functools.reduce(jnp.maximum, slabs)
        row_max = jnp.broadcast_to(jnp.max(mx, axis=-1, keepdims=True), mx.shape)
        m_new = row_max if carry is None else jnp.maximum(carry[0], row_max)
        ps = [jnp.exp2(sl - m_new) for sl in slabs]
        p = jnp.concatenate([x.astype(BF16) for x in ps], axis=1)
        l_new = functools.reduce(jnp.add, ps)
        acc_new = jnp.dot(p, v, preferred_element_type=F32)
        if carry is not None:
            alpha = jnp.exp2(carry[0] - m_new)
            l_new = alpha * carry[1] + l_new
            acc_new = alpha * carry[2] + acc_new
        return m_new, l_new, acc_new

    carry = None
    for c in range(k_ref.shape[0] // tk):
        carry = step(carry, k_ref[c * tk:(c + 1) * tk, :], v_ref[c * tk:(c + 1) * tk, :])
    if has_ctx:
        carry = step(carry, kc_ref[...], vc_ref[...])
    _, l_fin, acc_fin = carry
    o = acc_fin / jnp.sum(l_fin, axis=-1, keepdims=True)
    if mode == "diff":
        lam1 = jnp.sum(lam_ref[0:1, :] * lam_ref[1:2, :], axis=-1, keepdims=True)
        lam2 = jnp.sum(lam_ref[2:3, :] * lam_ref[3:4, :], axis=-1, keepdims=True)
        lam = jnp.exp(lam1) - jnp.exp(lam2) + LAMBDA_INIT_L0
        od = o[0:tq, :] - lam * o[tq:2 * tq, :]
        ms = jnp.mean(od * od, axis=-1, keepdims=True)
        on = od * lax.rsqrt(ms + EPS) * subg_ref[...]
        o_ref[...] = (on * (1.0 - LAMBDA_INIT_L0)).astype(o_ref.dtype)
    else:
        for gi in range(4):
            o_ref[:, gi * HEAD_DIM:(gi + 1) * HEAD_DIM] = o[gi * tq:(gi + 1) * tq, :].astype(o_ref.dtype)


def _flash(mode, q_arr, kv_arr, ctx_arr, batch, tq, tk, qcol, kcol, vcol, lam=None, subg=None):
    mq = q_arr.shape[0]
    sq = mq // batch
    sk = kv_arr.shape[0] // batch
    nq = sq // tq
    has_ctx = ctx_arr is not None
    if mode == "diff":
        nh, qw, reps = 8, HEAD_DIM, 2
        q_spec = pl.BlockSpec((tq, qw), lambda b, h, qi: (b * nq + qi, qcol + h))
        o_spec = pl.BlockSpec((tq, qw), lambda b, h, qi: (b * nq + qi, h))
    else:
        nh, qw, reps = 2, 4 * HEAD_DIM, 4
        q_spec = pl.BlockSpec((tq, qw), lambda b, h, qi: (b * nq + qi, qcol // 4 + h))
        o_spec = pl.BlockSpec((tq, qw), lambda b, h, qi: (b * nq + qi, h))
    in_specs = [q_spec,
                pl.BlockSpec((sk, HEAD_DIM), lambda b, h, qi: (b, kcol + h)),
                pl.BlockSpec((sk, HEAD_DIM), lambda b, h, qi: (b, vcol + h))]
    args = [q_arr, kv_arr, kv_arr]
    if has_ctx:
        nc = ctx_arr.shape[0] // batch
        in_specs += [pl.BlockSpec((nc, HEAD_DIM), lambda b, h, qi: (b, kcol + h)),
                     pl.BlockSpec((nc, HEAD_DIM), lambda b, h, qi: (b, vcol + h))]
        args += [ctx_arr, ctx_arr]
    if mode == "diff":
        in_specs += [pl.BlockSpec((4, A_SUB), lambda b, h, qi: (0, 0)),
                     pl.BlockSpec((1, HEAD_DIM), lambda b, h, qi: (0, 0))]
        args += [lam, subg]
    kern = functools.partial(_flash_kernel, mode=mode, tq=tq, has_ctx=has_ctx, tk=tk)
    return pl.pallas_call(
        kern,
        out_shape=jax.ShapeDtypeStruct((mq, 8 * HEAD_DIM), BF16),
        grid=(batch, nh, nq),
        in_specs=in_specs,
        out_specs=o_spec,
        scratch_shapes=[pltpu.VMEM((reps * tq, HEAD_DIM), BF16)],
        compiler_params=_cparams(("parallel", "parallel", "parallel")),
        name="flash_" + mode,
    )(*args)


def _banded_kernel(start_ref, var_ref, q_ref, k_ref, v_ref, kc_ref, vc_ref, bias_ref, sink_ref, o_ref,
                   *, mode, tq, band):
    del var_ref
    hg = pl.program_id(1)
    j = pl.program_id(2)
    start = pl.multiple_of(start_ref[j], LANES)
    kb = k_ref[pl.ds(start, band), :]
    vb = v_ref[pl.ds(start, band), :]
    kc = kc_ref[...]
    vc = vc_ref[...]
    q = q_ref[...]
    dn = (((1,), (1,)), ((), ()))
    heads = [slice(gi * HEAD_DIM, (gi + 1) * HEAD_DIM) for gi in range(4)]
    if mode == "window":
        qs = jnp.concatenate([q[:, hd] for hd in heads], axis=0)
        s_loc = lax.dot_general(qs, kb, dn, preferred_element_type=F32)
        s_ctx = lax.dot_general(qs, kc, dn, preferred_element_type=F32)
        bias = jnp.concatenate([bias_ref[...]] * 4, axis=0)
    else:
        s_loc = jnp.concatenate([lax.dot_general(q[:, hd], kb[:, hd], dn, preferred_element_type=F32)
                                 for hd in heads], axis=0)
        s_ctx = jnp.concatenate([lax.dot_general(q[:, hd], kc[:, hd], dn, preferred_element_type=F32)
                                 for hd in heads], axis=0)
        bias = bias_ref[...]
    s_loc = s_loc + bias
    n_loc = band // LANES
    slabs = ([s_loc[:, c * LANES:(c + 1) * LANES] for c in range(n_loc)]
             + [s_ctx[:, c * LANES:(c + 1) * LANES] for c in range(s_ctx.shape[1] // LANES)])
    mx = functools.reduce(jnp.maximum, slabs)
    m = jnp.broadcast_to(jnp.max(mx, axis=-1, keepdims=True), mx.shape)
    if mode == "window":
        sink = jnp.concatenate([jnp.full((tq, LANES), sink_ref[hg * 4 + gi] * LOG2E, F32) for gi in range(4)], axis=0)
        m = jnp.maximum(m, sink)
    ps = [jnp.exp2(sl - m) for sl in slabs]
    l = jnp.sum(functools.reduce(jnp.add, ps), axis=-1, keepdims=True)
    if mode == "window":
        l = l + jnp.exp2(sink - m)[:, 0:1]
    p_loc = jnp.concatenate([x.astype(BF16) for x in ps[:n_loc]], axis=1)
    p_ctx = jnp.concatenate([x.astype(BF16) for x in ps[n_loc:]], axis=1)
    if mode == "window":
        o = jnp.dot(p_loc, vb, preferred_element_type=F32) + jnp.dot(p_ctx, vc, preferred_element_type=F32)
        o = (o / l).astype(o_ref.dtype)
        for gi in range(4):
            o_ref[:, heads[gi]] = o[gi * tq:(gi + 1) * tq, :]
    else:
        for gi in range(4):
            rows = slice(gi * tq, (gi + 1) * tq)
            o = (jnp.dot(p_loc[rows, :], vb[:, heads[gi]], preferred_element_type=F32)
                 + jnp.dot(p_ctx[rows, :], vc[:, heads[gi]], preferred_element_type=F32))
            o_ref[:, heads[gi]] = (o / l[rows, :]).astype(o_ref.dtype)


def _banded(mode, proj, proj_ctx, batch, starts, variants, bias, sinks, qcol, kcol, vcol, kc_col, vc_col):
    m = proj.shape[0]
    s = m // batch
    tq = 128
    nq = s // tq
    nc = proj_ctx.shape[0] // batch
    band = bias.shape[-1]
    qw = 4 * HEAD_DIM
    if mode == "window":
        kw, div = HEAD_DIM, 1
        bias_spec = pl.BlockSpec((None, tq, band), lambda b, h, j, st, va: (va[j], 0, 0))
    else:
        kw, div = qw, 4
        bias_spec = pl.BlockSpec((None, None, 4 * tq, band), lambda b, h, j, st, va: (h, va[j], 0, 0))
    grid_spec = pltpu.PrefetchScalarGridSpec(
        num_scalar_prefetch=2,
        grid=(batch, 2, nq),
        in_specs=[
            pl.BlockSpec((tq, qw), lambda b, h, j, st, va: (b * nq + j, qcol // 4 + h)),
            pl.BlockSpec((s, kw), lambda b, h, j, st, va: (b, kcol // div + h)),
            pl.BlockSpec((s, kw), lambda b, h, j, st, va: (b, vcol // div + h)),
            pl.BlockSpec((nc, kw), lambda b, h, j, st, va: (b, kc_col // div + h)),
            pl.BlockSpec((nc, kw), lambda b, h, j, st, va: (b, vc_col // div + h)),
            bias_spec,
            pl.BlockSpec(memory_space=pltpu.SMEM),
        ],
        out_specs=pl.BlockSpec((tq, qw), lambda b, h, j, st, va: (b * nq + j, h)),
    )
    kern = functools.partial(_banded_kernel, mode=mode, tq=tq, band=band)
    return pl.pallas_call(
        kern,
        out_shape=jax.ShapeDtypeStruct((m, 8 * HEAD_DIM), BF16),
        grid_spec=grid_spec,
        compiler_params=_cparams(("parallel", "parallel", "arbitrary")),
        name="banded_" + mode,
    )(starts, variants, proj, proj, proj, proj_ctx, proj_ctx, bias, sinks)


def _outproj_kernel(oa_ref, ob_ref, wa_ref, wb_ref, x_ref, gt_ref, o_ref):
    acc = jnp.dot(oa_ref[...], wa_ref[...], preferred_element_type=F32)
    acc = acc + jnp.dot(ob_ref[...], wb_ref[...], preferred_element_type=F32)
    o_ref[...] = x_ref[...] + gt_ref[...] * acc


def _outproj(oa, ob, w_bf, x2d, gate, rows_per_batch, tm):
    m, d = x2d.shape
    half = oa.shape[1]
    tiles_per_batch = rows_per_batch // tm
    return pl.pallas_call(
        _outproj_kernel,
        out_shape=jax.ShapeDtypeStruct((m, d), F32),
        grid=(m // tm,),
        in_specs=[pl.BlockSpec((tm, half), lambda i: (i, 0)),
                  pl.BlockSpec((tm, half), lambda i: (i, 0)),
                  pl.BlockSpec((half, d), lambda i: (0, 0)),
                  pl.BlockSpec((half, d), lambda i: (1, 0)),
                  pl.BlockSpec((tm, d), lambda i: (i, 0)),
                  pl.BlockSpec((None, 1, d), lambda i: (i // tiles_per_batch, 0, 0))],
        out_specs=pl.BlockSpec((tm, d), lambda i: (i, 0)),
        compiler_params=_cparams(("parallel",)),
        name="outproj",
    )(oa, ob, w_bf, w_bf, x2d, gate)


def _swiglu_step(h, wg, wu, wd):
    g = jnp.dot(h, wg, preferred_element_type=F32)
    u = jnp.dot(h, wu, preferred_element_type=F32)
    a = (g * (1.0 / (1.0 + jnp.exp(-g))) * u).astype(BF16)
    return jnp.dot(a, wd, preferred_element_type=F32)


def _ffn_kernel(x_ref, g_ref, sc_ref, sh_ref, gt_ref, wg_ref, wu_ref, wd_ref, o_ref, h_scr, acc_scr, *, nf):
    j = pl.program_id(1)

    @pl.when(j == 0)
    def _():
        h_scr[...] = _modulate(x_ref[...], g_ref[...], sc_ref[...], sh_ref[...]).astype(BF16)
        acc_scr[...] = jnp.zeros_like(acc_scr)

    acc_scr[...] += _swiglu_step(h_scr[...], wg_ref[...], wu_ref[...], wd_ref[...])

    @pl.when(j == nf - 1)
    def _():
        o_ref[...] = x_ref[...] + gt_ref[...] * acc_scr[...]


def _ffn(x2d, rows_per_batch, g, scale, shift, gate, wg, wu, wd, tm):
    m, d = x2d.shape
    f = wg.shape[1]
    tf = 512
    nf = f // tf
    tiles_per_batch = rows_per_batch // tm
    vec = pl.BlockSpec((None, 1, d), lambda i, j: (i // tiles_per_batch, 0, 0))
    return pl.pallas_call(
        functools.partial(_ffn_kernel, nf=nf),
        out_shape=jax.ShapeDtypeStruct((m, d), F32),
        grid=(m // tm, nf),
        in_specs=[pl.BlockSpec((tm, d), lambda i, j: (i, 0)),
                  pl.BlockSpec((1, d), lambda i, j: (0, 0)),
                  vec, vec, vec,
                  pl.BlockSpec((d, tf), lambda i, j: (0, j)),
                  pl.BlockSpec((d, tf), lambda i, j: (0, j)),
                  pl.BlockSpec((tf, d), lambda i, j: (j, 0))],
        out_specs=pl.BlockSpec((tm, d), lambda i, j: (i, 0)),
        scratch_shapes=[pltpu.VMEM((tm, d), BF16), pltpu.VMEM((tm, d), F32)],
        compiler_params=_cparams(("parallel", "arbitrary")),
        name="ffn",
    )(x2d, g.reshape(1, d), scale, shift, gate, wg, wu, wd)


def _router_kernel(x_ref, g_ref, sc_ref, sh_ref, wr_ref, h_ref, ri_ref, rg_ref, *, n_exp, n_tiles):
    h = _modulate(x_ref[...], g_ref[...], sc_ref[...], sh_ref[...])
    h_ref[...] = jnp.where(pl.program_id(0) < n_tiles, h, 0.0)
    logits = jnp.dot(h, wr_ref[...], preferred_element_type=F32, precision=lax.Precision.HIGHEST)
    lane = lax.broadcasted_iota(jnp.int32, logits.shape, 1)
    logits = jnp.where(lane < n_exp, logits, -jnp.inf)
    lane_f = lane.astype(F32)
    m1 = jnp.max(logits, axis=-1, keepdims=True)
    i1 = jnp.min(jnp.where(logits == m1, lane_f, float(LANES)), axis=-1, keepdims=True)
    rest = jnp.where(lane_f == i1, -jnp.inf, logits)
    m2 = jnp.max(rest, axis=-1, keepdims=True)
    i2 = jnp.min(jnp.where(rest == m2, lane_f, float(LANES)), axis=-1, keepdims=True)
    e = jnp.exp(m2 - m1)
    g1 = 1.0 / (1.0 + e)
    g2 = e / (1.0 + e)
    ri_ref[...] = jnp.where(lane == 0, i1, jnp.where(lane == 1, i2, 0.0)).astype(jnp.int32)
    rg_ref[...] = jnp.where(lane == 0, g1, jnp.where(lane == 1, g2, 0.0))


def _router(x2d, rows_per_batch, g, scale, shift, wr_pad, n_exp, tm):
    m, d = x2d.shape
    tiles_per_batch = rows_per_batch // tm
    n_tiles = m // tm
    last = n_tiles - 1
    vec = pl.BlockSpec((None, 1, d), lambda i: (jnp.minimum(i, last) // tiles_per_batch, 0, 0))
    return pl.pallas_call(
        functools.partial(_router_kernel, n_exp=n_exp, n_tiles=n_tiles),
        out_shape=(jax.ShapeDtypeStruct((m + tm, d), F32),
                   jax.ShapeDtypeStruct((m + tm, LANES), jnp.int32),
                   jax.ShapeDtypeStruct((m + tm, LANES), F32)),
        grid=(n_tiles + 1,),
        in_specs=[pl.BlockSpec((tm, d), lambda i: (jnp.minimum(i, last), 0)),
                  pl.BlockSpec((1, d), lambda i: (0, 0)),
                  vec, vec,
                  pl.BlockSpec((d, LANES), lambda i: (0, 0))],
        out_specs=(pl.BlockSpec((tm, d), lambda i: (i, 0)),
                   pl.BlockSpec((tm, LANES), lambda i: (i, 0)),
                   pl.BlockSpec((tm, LANES), lambda i: (i, 0))),
        compiler_params=_cparams(("parallel",)),
        name="router",
    )(x2d, g.reshape(1, d), scale, shift, wr_pad)


def _moe_kernel(be_ref, nv_ref, tok_ref, h_hbm, wg_ref, wu_ref, wd_ref, o_ref, xbuf, xb_scr, sem,
                *, nf, tm, nblk, per_step):
    del be_ref
    i = pl.program_id(0)
    j = pl.program_id(1)
    n_valid = nv_ref[i]
    half = tm // 2
    n_copies = nf * per_step

    def start_row(blk, r):
        tok = tok_ref[blk * tm + jnp.minimum(r, tm - 1)]
        pltpu.make_async_copy(h_hbm.at[pl.ds(tok, 1)], xbuf.at[pl.ds(r, 1)], sem).start()

    def wait_all():
        pltpu.make_async_copy(h_hbm.at[pl.ds(0, n_copies)], xbuf.at[pl.ds(0, n_copies)], sem).wait()

    def start_next():
        nxt = jnp.minimum(i + 1, nblk - 1)
        for u in range(per_step):
            start_row(nxt, j * per_step + u)

    @pl.when(j == 0)
    def _():
        @pl.when(i == 0)
        def _():
            def body(r, carry):
                start_row(0, r)
                return carry
            lax.fori_loop(0, n_copies, body, 0)

        @pl.when(jnp.logical_or(i == 0, nv_ref[jnp.maximum(i - 1, 0)] > 0))
        def _():
            wait_all()
            xb_scr[...] = xbuf[0:tm, :].astype(BF16)

        o_ref[...] = jnp.zeros_like(o_ref)

    def accumulate(rows):
        start_next()
        o_ref[rows, :] += _swiglu_step(xb_scr[rows, :], wg_ref[...].astype(BF16), wu_ref[...].astype(BF16),
                                       wd_ref[...].astype(BF16))

    @pl.when(n_valid > half)
    def _():
        accumulate(slice(0, tm))

    @pl.when(jnp.logical_and(n_valid > 0, n_valid <= half))
    def _():
        accumulate(slice(0, half))

    @pl.when(jnp.logical_and(jnp.logical_and(i == nblk - 1, j == nf - 1), n_valid > 0))
    def _():
        wait_all()


def _moe(slot_tok, h_pad, block_expert, n_valid, wg, wu, wd, tm):
    cap = slot_tok.shape[0]
    d = h_pad.shape[1]
    f = wg.shape[2]
    tf = 256
    nf = f // tf
    nblk = cap // tm
    per_step = -(-tm // nf)
    while (nf * per_step) % 8:
        per_step += 1
    buf_rows = nf * per_step

    def col_j(i, j, nv):
        return jnp.where(nv[i] > 0, j, nf - 1)

    grid_spec = pltpu.PrefetchScalarGridSpec(
        num_scalar_prefetch=3,
        grid=(nblk, nf),
        in_specs=[pl.BlockSpec(memory_space=pl.ANY),
                  pl.BlockSpec((None, d, tf), lambda i, j, be, nv, tok: (be[i], 0, col_j(i, j, nv))),
                  pl.BlockSpec((None, d, tf), lambda i, j, be, nv, tok: (be[i], 0, col_j(i, j, nv))),
                  pl.BlockSpec((None, tf, d), lambda i, j, be, nv, tok: (be[i], col_j(i, j, nv), 0))],
        out_specs=pl.BlockSpec((tm, d), lambda i, j, be, nv, tok: (i, 0)),
        scratch_shapes=[pltpu.VMEM((buf_rows, d), F32), pltpu.VMEM((tm, d), BF16), pltpu.SemaphoreType.DMA],
    )
    return pl.pallas_call(
        functools.partial(_moe_kernel, nf=nf, tm=tm, nblk=nblk, per_step=per_step),
        out_shape=jax.ShapeDtypeStruct((cap, d), F32),
        grid_spec=grid_spec,
        compiler_params=_cparams(("arbitrary", "arbitrary")),
        name="moe",
    )(block_expert, n_valid, slot_tok, h_pad, wg, wu, wd)


def _combine_kernel(slot_ref, x_ref, gt_ref, rg_ref, fg_ref, y_hbm, o_ref, buf, sem, *, tc, n_steps):
    i = pl.program_id(0)

    def start_tile(t, slot):
        def issue(r, carry):
            for k in range(TOP_K):
                s = slot_ref[TOP_K * (t * tc + r) + k]
                pltpu.make_async_copy(y_hbm.at[pl.ds(s, 1)], buf.at[slot, pl.ds(k * tc + r, 1)],
                                      sem.at[slot]).start()
            return carry
        lax.fori_loop(0, tc, issue, 0, unroll=8)

    @pl.when(i == 0)
    def _():
        start_tile(0, 0)

    @pl.when(i + 1 < n_steps)
    def _():
        start_tile(i + 1, (i + 1) % 2)

    cur = i % 2
    pltpu.make_async_copy(y_hbm.at[pl.ds(0, TOP_K * tc)], buf.at[cur], sem.at[cur]).wait()
    rg = rg_ref[...]
    y = rg[:, 0:1] * buf[cur, 0:tc, :] + rg[:, 1:2] * buf[cur, tc:2 * tc, :]
    xo = x_ref[...] + gt_ref[...] * y
    ms = jnp.mean(xo * xo, axis=-1, keepdims=True)
    o_ref[...] = xo * lax.rsqrt(ms + EPS) * fg_ref[...]


def _combine(slot, x2d, rows_per_batch, gate, route_g, final_g, y, tc):
    m, d = x2d.shape
    tiles_per_batch = rows_per_batch // tc
    grid_spec = pltpu.PrefetchScalarGridSpec(
        num_scalar_prefetch=1,
        grid=(m // tc,),
        in_specs=[pl.BlockSpec((tc, d), lambda i, sl: (i, 0)),
                  pl.BlockSpec((None, 1, d), lambda i, sl: (i // tiles_per_batch, 0, 0)),
                  pl.BlockSpec((tc, LANES), lambda i, sl: (i, 0)),
                  pl.BlockSpec((1, d), lambda i, sl: (0, 0)),
                  pl.BlockSpec(memory_space=pl.ANY)],
        out_specs=pl.BlockSpec((tc, d), lambda i, sl: (i, 0)),
        scratch_shapes=[pltpu.VMEM((2, TOP_K * tc, d), F32), pltpu.SemaphoreType.DMA((2,))],
    )
    return pl.pallas_call(
        functools.partial(_combine_kernel, tc=tc, n_steps=m // tc),
        out_shape=jax.ShapeDtypeStruct((m, d), F32),
        grid_spec=grid_spec,
        compiler_params=_cparams(("arbitrary",)),
        name="combine",
    )(slot, x2d, gate, route_g, final_g.reshape(1, d), y)


def _rope_tables(n_tok, dim):
    t = jnp.arange(n_tok)
    pos = jnp.stack([t // GRID_W, t % GRID_W], axis=-1).astype(F32)
    n_pairs = dim // 4
    freq = ROPE_BASE ** (-jnp.arange(n_pairs, dtype=F32) / n_pairs)
    ang = (pos[:, :, None] * freq).reshape(n_tok, 2 * n_pairs)
    cos = jnp.tile(jnp.cos(ang), (1, HEAD_DIM // dim))
    sin = jnp.tile(jnp.sin(ang), (1, HEAD_DIM // dim))
    return jnp.concatenate([cos, cos], axis=1), jnp.concatenate([-sin, sin], axis=1)


def _deinterleave(w, comps):
    d, n = w.shape
    pairs = HEAD_DIM // (2 * comps)
    return w.reshape(d, n // HEAD_DIM, comps, pairs, 2).transpose(0, 1, 4, 2, 3).reshape(d, n)


def _dedupe(patterns):
    keys, variant = {}, []
    for p in patterns:
        variant.append(keys.setdefault(p.tobytes(), len(keys)))
    first = [variant.index(v) for v in range(len(keys))]
    return np.asarray(variant, np.int32), first


def _window_tables(s):
    tq = 128
    nb = s // tq
    band = 3 * tq
    starts = np.clip(np.arange(nb) - 1, 0, nb - 3) * tq
    qpos = np.arange(nb)[:, None] * tq + np.arange(tq)[None, :]
    kpos = starts[:, None] + np.arange(band)[None, :]
    ok = np.abs(kpos[:, None, :] - qpos[:, :, None]) <= C_WINDOW
    variant, first = _dedupe(list(ok))
    bias = np.where(ok[first], 0.0, NEG_INF).astype(np.float32)
    return starts.astype(np.int32), variant, jnp.asarray(bias)


def _na_tables(s, rpb):
    tq = 128
    rows = s // GRID_W
    wr = min(NA_ROWS, rows)
    nb = s // tq
    band_blocks = 5
    band = band_blocks * tq
    starts = np.clip(np.arange(nb) - 2, 0, nb - band_blocks) * tq
    n_h = rpb.shape[0]
    n_dc = 2 * NA_COLS - 1
    period = GRID_W + n_dc - 1
    wrap = jnp.zeros((n_h, 2 * NA_ROWS - 1, period), F32)
    wrap = wrap.at[..., :NA_COLS].set(rpb[..., NA_COLS - 1:].astype(F32))
    wrap = wrap.at[..., period - (NA_COLS - 1):].set(rpb[..., :NA_COLS - 1].astype(F32))
    toep = jnp.tile(wrap, (1, 1, GRID_W))[..., :GRID_W * (period - 1)]
    toep = toep.reshape(n_h, 2 * NA_ROWS - 1, GRID_W, period - 1)[..., :GRID_W]
    qc = np.arange(GRID_W)[:, None]
    kc = np.arange(GRID_W)[None, :]
    cstart = np.clip(qc - NA_COLS // 2, 0, GRID_W - NA_COLS)
    col_ok = (kc >= cstart) & (kc < cstart + NA_COLS)
    toep = jnp.where(jnp.asarray(col_ok), toep * LOG2E, NEG_INF)
    masked = jnp.full((n_h, GRID_W, GRID_W), NEG_INF, F32)
    pats = []
    for j in range(nb):
        pat = np.full((tq // GRID_W, band // GRID_W), -1, np.int32)
        for a in range(tq // GRID_W):
            qr = j * (tq // GRID_W) + a
            rs = min(max(qr - wr // 2, 0), rows - wr)
            for w in range(band // GRID_W):
                kr = starts[j] // GRID_W + w
                if rs <= kr < rs + wr:
                    pat[a, w] = kr - qr + NA_ROWS - 1
        pats.append(pat)
    variant, first = _dedupe(pats)
    tiles = []
    for j in first:
        rows_ = [jnp.concatenate([toep[:, dr] if dr >= 0 else masked for dr in pats[j][a]], axis=-1)
                 for a in range(tq // GRID_W)]
        tiles.append(jnp.concatenate(rows_, axis=-2))
    bias = jnp.stack(tiles, axis=1)
    bias = bias.reshape(n_h // 4, 4, len(first), tq, band).transpose(0, 2, 1, 3, 4)
    return starts.astype(np.int32), variant, bias.reshape(n_h // 4, len(first), 4 * tq, band)


def kernel(x, c, ctx, c_ctx, l0_norm1_g, l0_norm2_g, l0_w_ada, l0_b_ada, l0_w_in, l0_w_out, l0_lam_q1, l0_lam_k1, l0_lam_q2, l0_lam_k2, l0_subln_g, l0_q_norm_g, l0_k_norm_g, l0_ffn_w_gate, l0_ffn_w_up, l0_ffn_w_down, l1_norm1_g, l1_norm2_g, l1_w_ada, l1_b_ada, l1_w_in, l1_w_out, l1_sinks, l1_rpb, l1_router, l1_exp_w_gate, l1_exp_w_up, l1_exp_w_down, final_norm_g):
    b, s, d = x.shape
    n_ctx = ctx.shape[1]
    n_exp = l1_router.shape[1]
    n = b * s
    assert d == 2048 and s % 512 == 0 and n_ctx == 256 and s // GRID_W >= 10

    x2d = x.reshape(n, d)
    xc2d = ctx.reshape(b * n_ctx, d)

    cvecs = jnp.zeros((8, d), F32).at[:b].set(c).at[b].set(c_ctx)
    mod0 = _ada(cvecs, l0_w_ada, l0_b_ada)
    mod1 = _ada(cvecs, l1_w_ada, l1_b_ada)

    def lat(mod, k):
        return mod[:b, k * d:(k + 1) * d].reshape(b, 1, d)

    def cx(mod, k):
        return jnp.broadcast_to(mod[b, k * d:(k + 1) * d].reshape(1, 1, d), (b, 1, d))

    cos_a, sin_a = _rope_tables(s, A_SUB)
    cos_b, sin_b = _rope_tables(s, HEAD_DIM)
    cos_tab = jnp.stack([cos_a, cos_b])
    sin_tab = jnp.stack([sin_a, sin_b])
    tm = 512
    hq = 8 * HEAD_DIM

    qa = T_ROPE | T_SCALE_A
    qb = T_ROPE | T_QNORM | T_SCALE_B
    kb = T_ROPE | T_KNORM
    types0 = [qa] * 8 + [qb] * 8 + [T_ROPE] * 8 + [0] * 8 + [kb] * 2 + [0] * 2
    tabs0 = [0] * 8 + [1] * 8 + [0] * 8 + [0] * 8 + [1] * 2 + [0] * 2
    qkg0 = _deinterleave(jnp.stack([l0_q_norm_g, l0_k_norm_g]).astype(F32), 1)
    w0 = l0_w_in.astype(BF16)
    w_in0 = jnp.concatenate([_deinterleave(w0[:, :hq], 2), _deinterleave(w0[:, hq:2 * hq], 1),
                             _deinterleave(w0[:, 2 * hq:3 * hq], 2), w0[:, 3 * hq:4 * hq],
                             _deinterleave(w0[:, 4 * hq:4 * hq + 2 * HEAD_DIM], 1),
                             w0[:, 4 * hq + 2 * HEAD_DIM:]], axis=1)
    p0 = _inproj(x2d, s, l0_norm1_g, lat(mod0, 1), lat(mod0, 0), w_in0, cos_tab, sin_tab, qkg0,
                 types0, tabs0, tm)
    rc = b * n_ctx
    tmc = tm if rc % tm == 0 else n_ctx
    p0c = _inproj(xc2d, rc, l0_norm1_g, cx(mod0, 1), cx(mod0, 0), w_in0, cos_tab, sin_tab, qkg0,
                  [t & ~T_ROPE for t in types0], tabs0, tmc)
    lam = jnp.stack([l0_lam_q1, l0_lam_k1, l0_lam_q2, l0_lam_k2]).astype(F32)
    subg = l0_subln_g.reshape(1, HEAD_DIM).astype(F32)
    tk = min(2048, s)
    oa = _flash("diff", p0, p0, p0c, b, 512, tk, 0, 16, 24, lam, subg)
    ob = _flash("gqa", p0, p0, p0c, b, 256, tk, 8, 32, 34)
    oac = _flash("diff", p0c, p0c, None, b, n_ctx, n_ctx, 0, 16, 24, lam, subg)
    obc = _flash("gqa", p0c, p0c, None, b, n_ctx, n_ctx, 8, 32, 34)
    w_out0 = l0_w_out.astype(BF16)
    x1 = _outproj(oa, ob, w_out0, x2d, lat(mod0, 2), s, tm)
    xc1 = _outproj(oac, obc, w_out0, xc2d, cx(mod0, 2), rc, tmc)
    wg0, wu0, wd0 = l0_ffn_w_gate.astype(BF16), l0_ffn_w_up.astype(BF16), l0_ffn_w_down.astype(BF16)
    x2 = _ffn(x1, s, l0_norm2_g, lat(mod0, 4), lat(mod0, 3), lat(mod0, 5), wg0, wu0, wd0, tm)
    xc2 = _ffn(xc1, rc, l0_norm2_g, cx(mod0, 4), cx(mod0, 3), cx(mod0, 5), wg0, wu0, wd0, tmc)

    types1 = [T_ROPE | T_SCALE_B] * 8 + [T_SCALE_B] * 8 + [T_ROPE] * 2 + [0] * 2 + [0] * 16
    tabs1 = [1] * len(types1)
    w1 = l1_w_in.astype(BF16)
    w_in1 = jnp.concatenate([_deinterleave(w1[:, :hq], 1), w1[:, hq:2 * hq],
                             _deinterleave(w1[:, 2 * hq:2 * hq + 2 * HEAD_DIM], 1),
                             w1[:, 2 * hq + 2 * HEAD_DIM:]], axis=1)
    qkg1 = jnp.ones((2, HEAD_DIM), F32)
    p1 = _inproj(x2, s, l1_norm1_g, lat(mod1, 1), lat(mod1, 0), w_in1, cos_tab, sin_tab, qkg1,
                 types1, tabs1, tm)
    l1_q = 16 * HEAD_DIM
    n_kv_heads = (w_in1.shape[1] - l1_q) // HEAD_DIM
    p1c = _inproj(xc2, rc, l1_norm1_g, cx(mod1, 1), cx(mod1, 0), w_in1[:, l1_q:], cos_tab, sin_tab, qkg1,
                  [0] * n_kv_heads, [0] * n_kv_heads, tmc)
    w_st, w_var, w_bias = _window_tables(s)
    n_st, n_var, n_bias = _na_tables(s, l1_rpb)
    sinks = l1_sinks.astype(F32)
    oc = _banded("window", p1, p1c, b, jnp.asarray(w_st), jnp.asarray(w_var), w_bias, sinks, 0, 16, 18, 0, 2)
    od = _banded("na", p1, p1c, b, jnp.asarray(n_st), jnp.asarray(n_var), n_bias, sinks, 8, 20, 28, 4, 12)
    x3 = _outproj(oc, od, l1_w_out.astype(BF16), x2, lat(mod1, 2), s, tm)

    wr_pad = jnp.zeros((d, LANES), F32).at[:, :n_exp].set(l1_router.astype(F32))
    h2, route_i, route_g = _router(x3, s, l1_norm2_g, lat(mod1, 4), lat(mod1, 3), wr_pad, n_exp, tm)
    route_g = route_g[:n]
    e_flat = route_i[:n, :TOP_K].reshape(-1)
    onehot = (e_flat[:, None] == jnp.arange(n_exp, dtype=jnp.int32)[None, :]).astype(jnp.int32)
    csum = jnp.cumsum(onehot, axis=0)
    rank = jnp.sum((csum - onehot) * onehot, axis=1)
    counts = csum[-1]
    tmo = 2 * tm
    padded = (counts + tmo - 1) // tmo * tmo
    pad_end = jnp.cumsum(padded)
    pad_start = pad_end - padded
    slot = (pad_start[e_flat] + rank).astype(jnp.int32)
    n_blocks = (n * TOP_K + n_exp * (tmo - 1) + tmo - 1) // tmo
    cap = n_blocks * tmo
    tok_flat = jnp.repeat(jnp.arange(n, dtype=jnp.int32), TOP_K)
    slot_tok = jnp.full((cap,), n, jnp.int32).at[slot].set(tok_flat)
    block_row0 = jnp.arange(n_blocks, dtype=jnp.int32) * tmo
    block_expert = jnp.minimum(jnp.searchsorted(pad_end, block_row0, side="right"), n_exp - 1).astype(jnp.int32)
    n_valid = jnp.clip((pad_start + counts)[block_expert] - block_row0, 0, tmo).astype(jnp.int32)

    y = _moe(slot_tok, h2, block_expert, n_valid, l1_exp_w_gate, l1_exp_w_up, l1_exp_w_down, tmo)
    out = _combine(slot, x3, s, lat(mod1, 5), route_g, final_norm_g, y, 256)
    return out.reshape(b, s, d)
```

```python
import functools
import math

import numpy as np
import jax
import jax.numpy as jnp
from jax import lax
from jax.experimental import pallas as pl
from jax.experimental.pallas import tpu as pltpu

F32 = jnp.float32
BF16 = jnp.bfloat16

HEAD_DIM = 128
LANES = 128
GRID_W = 64
ROPE_BASE = 10000.0
EPS = 1e-6
NEG_INF = -1e30
A_SUB = HEAD_DIM // 2
A_SCALE = A_SUB ** -0.5
ATT_SCALE = HEAD_DIM ** -0.5
LAMBDA_INIT_L0 = 0.8 - 0.6 * math.exp(-0.3 * 0)
LOG2E = 1.4426950408889634
C_WINDOW = 128
NA_ROWS = 8
NA_COLS = 16
TOP_K = 2
VMEM_LIMIT = 56 * 1024 * 1024
T_ROPE, T_QNORM, T_KNORM, T_SCALE_B, T_SCALE_A = 1, 2, 4, 8, 16


def _cparams(sem):
    return pltpu.CompilerParams(dimension_semantics=sem, vmem_limit_bytes=VMEM_LIMIT)


def _modulate(xf, g, scale, shift):
    ms = jnp.mean(xf * xf, axis=-1, keepdims=True)
    y = xf * lax.rsqrt(ms + EPS) * g
    return y * (1.0 + scale) + shift


def _ada_kernel(c_ref, w_ref, b_ref, o_ref):
    c = c_ref[...]
    a = c * (1.0 / (1.0 + jnp.exp(-c)))
    o_ref[...] = jnp.dot(a.astype(BF16), w_ref[...].astype(BF16), preferred_element_type=F32) + b_ref[...]


def _ada(cvecs, w_ada, b_ada):
    r, d = cvecs.shape
    n = w_ada.shape[1]
    tn = 1024
    return pl.pallas_call(
        _ada_kernel,
        out_shape=jax.ShapeDtypeStruct((r, n), F32),
        grid=(n // tn,),
        in_specs=[pl.BlockSpec((r, d), lambda j: (0, 0)),
                  pl.BlockSpec((d, tn), lambda j: (0, j)),
                  pl.BlockSpec((1, tn), lambda j: (0, j))],
        out_specs=pl.BlockSpec((r, tn), lambda j: (0, j)),
        compiler_params=_cparams(("parallel",)),
        name="ada",
    )(cvecs, w_ada, b_ada.reshape(1, n))


def _inproj_kernel(x_ref, g_ref, sc_ref, sh_ref, w_ref, cos_ref, sin_ref, qkg_ref, o_ref, h_scr, *, types, tabs):
    h_scr[...] = _modulate(x_ref[...], g_ref[...], sc_ref[...], sh_ref[...]).astype(BF16)
    chunk = 4 * HEAD_DIM
    for c in range(len(types) // 4):
        acc = jnp.dot(h_scr[...], w_ref[:, c * chunk:(c + 1) * chunk], preferred_element_type=F32)
        for hh in range(4):
            t = types[4 * c + hh]
            y = acc[:, hh * HEAD_DIM:(hh + 1) * HEAD_DIM]
            if t & (T_QNORM | T_KNORM):
                ms = jnp.mean(y * y, axis=-1, keepdims=True)
                row = 0 if t & T_QNORM else 1
                y = y * lax.rsqrt(ms + EPS) * qkg_ref[row:row + 1, :]
            if t & T_ROPE:
                tb = tabs[4 * c + hh]
                y = y * cos_ref[tb] + pltpu.roll(y, HEAD_DIM // 2, 1) * sin_ref[tb]
            if t & T_SCALE_B:
                y = y * (ATT_SCALE * LOG2E)
            elif t & T_SCALE_A:
                y = y * (A_SCALE * LOG2E)
            col = (4 * c + hh) * HEAD_DIM
            o_ref[:, col:col + HEAD_DIM] = y.astype(o_ref.dtype)


def _inproj(x2d, rows_per_batch, g, scale, shift, w_bf, cos_tab, sin_tab, qkg, types, tabs, tm):
    m, d = x2d.shape
    n = w_bf.shape[1]
    tiles_per_batch = rows_per_batch // tm
    tab_rows = cos_tab.shape[1] // tm
    kern = functools.partial(_inproj_kernel, types=tuple(int(t) for t in types), tabs=tuple(int(t) for t in tabs))
    return pl.pallas_call(
        kern,
        out_shape=jax.ShapeDtypeStruct((m, n), BF16),
        grid=(m // tm,),
        in_specs=[
            pl.BlockSpec((tm, d), lambda i: (i, 0)),
            pl.BlockSpec((1, d), lambda i: (0, 0)),
            pl.BlockSpec((None, 1, d), lambda i: (i // tiles_per_batch, 0, 0)),
            pl.BlockSpec((None, 1, d), lambda i: (i // tiles_per_batch, 0, 0)),
            pl.BlockSpec((d, n), lambda i: (0, 0), pipeline_mode=pl.Buffered(1)),
            pl.BlockSpec((2, tm, LANES), lambda i: (0, i % tab_rows, 0)),
            pl.BlockSpec((2, tm, LANES), lambda i: (0, i % tab_rows, 0)),
            pl.BlockSpec((2, HEAD_DIM), lambda i: (0, 0)),
        ],
        out_specs=pl.BlockSpec((tm, n), lambda i: (i, 0)),
        scratch_shapes=[pltpu.VMEM((tm, d), BF16)],
        compiler_params=_cparams(("parallel",)),
        name="inproj",
    )(x2d, g.reshape(1, d), scale, shift, w_bf, cos_tab, sin_tab, qkg)


def _flash_kernel(*refs, mode, tq, has_ctx, tk):
    refs = list(refs)
    q_ref, k_ref, v_ref = refs[:3]
    pos = 3
    if has_ctx:
        kc_ref, vc_ref = refs[3:5]
        pos = 5
    if mode == "diff":
        lam_ref, subg_ref = refs[pos:pos + 2]
        pos += 2
    o_ref, qs = refs[pos:pos + 2]

    q = q_ref[...]
    if mode == "diff":
        lane = lax.broadcasted_iota(jnp.int32, (1, LANES), 1)
        zero = jnp.zeros_like(q)
        first = (lane & (A_SUB // 2)) == 0
        qs[0:tq, :] = jnp.where(first, q, zero)
        qs[tq:2 * tq, :] = jnp.where(first, zero, q)
    else:
        for gi in range(4):
            qs[gi * tq:(gi + 1) * tq, :] = q[:, gi * HEAD_DIM:(gi + 1) * HEAD_DIM]

    def step(carry, k, v):
        s = lax.dot_general(qs[...], k, (((1,), (1,)), ((), ())), preferred_element_type=F32)
        slabs = [s[:, c * LANES:(c + 1) * LANES] for c in range(k.shape[0] // LANES)]
        mx = functools.reduce(jnp.maximum, slabs)
        row_max = jnp.broadcast_to(jnp.max(mx, axis=-1, keepdims=True), mx.shape)
        m_new = row_max if carry is None else jnp.maximum(carry[0], row_max)
        ps = [jnp.exp2(sl - m_new) for sl in slabs]
        p = jnp.concatenate([x.astype(BF16) for x in ps], axis=1)
        l_new = functools.reduce(jnp.add, ps)
        acc_new = jnp.dot(p, v, preferred_element_type=F32)
        if carry is not None:
            alpha = jnp.exp2(carry[0] - m_new)
            l_new = alpha * carry[1] + l_new
            acc_new = alpha * carry[2] + acc_new
        return m_new, l_new, acc_new

    carry = None
    for c in range(k_ref.shape[0] // tk):
        carry = step(carry, k_ref[c * tk:(c + 1) * tk, :], v_ref[c * tk:(c + 1) * tk, :])
    if has_ctx:
        carry = step(carry, kc_ref[...], vc_ref[...])
    _, l_fin, acc_fin = carry
    o = acc_fin / jnp.sum(l_fin, axis=-1, keepdims=True)
    if mode == "diff":
        lam1 = jnp.sum(lam_ref[0:1, :] * lam_ref[1:2, :], axis=-1, keepdims=True)
        lam2 = jnp.sum(lam_ref[2:3, :] * lam_ref[3:4, :], axis=-1, keepdims=True)
        lam = jnp.exp(lam1) - jnp.exp(lam2) + LAMBDA_INIT_L0
        od = o[0:tq, :] - lam * o[tq:2 * tq, :]
        ms = jnp.mean(od * od, axis=-1, keepdims=True)
        on = od * lax.rsqrt(ms + EPS) * subg_ref[...]
        o_ref[...] = (on * (1.0 - LAMBDA_INIT_L0)).astype(o_ref.dtype)
    else:
        for gi in range(4):
            o_ref[:, gi * HEAD_DIM:(gi + 1) * HEAD_DIM] = o[gi * tq:(gi + 1) * tq, :].astype(o_ref.dtype)


def _flash(mode, q_arr, kv_arr, ctx_arr, batch, tq, tk, qcol, kcol, vcol, lam=None, subg=None):
    mq = q_arr.shape[0]
    sq = mq // batch
    sk = kv_arr.shape[0] // batch
    nq = sq // tq
    has_ctx = ctx_arr is not None
    if mode == "diff":
        nh, qw, reps = 8, HEAD_DIM, 2
        q_spec = pl.BlockSpec((tq, qw), lambda b, h, qi: (b * nq + qi, qcol + h))
        o_spec = pl.BlockSpec((tq, qw), lambda b, h, qi: (b * nq + qi, h))
    else:
        nh, qw, reps = 2, 4 * HEAD_DIM, 4
        q_spec = pl.BlockSpec((tq, qw), lambda b, h, qi: (b * nq + qi, qcol // 4 + h))
        o_spec = pl.BlockSpec((tq, qw), lambda b, h, qi: (b * nq + qi, h))
    in_specs = [q_spec,
                pl.BlockSpec((sk, HEAD_DIM), lambda b, h, qi: (b, kcol + h)),
                pl.BlockSpec((sk, HEAD_DIM), lambda b, h, qi: (b, vcol + h))]
    args = [q_arr, kv_arr, kv_arr]
    if has_ctx:
        nc = ctx_arr.shape[0] // batch
        in_specs += [pl.BlockSpec((nc, HEAD_DIM), lambda b, h, qi: (b, kcol + h)),
                     pl.BlockSpec((nc, HEAD_DIM), lambda b, h, qi: (b, vcol + h))]
        args += [ctx_arr, ctx_arr]
    if mode == "diff":
        in_specs += [pl.BlockSpec((4, A_SUB), lambda b, h, qi: (0, 0)),
                     pl.BlockSpec((1, HEAD_DIM), lambda b, h, qi: (0, 0))]
        args += [lam, subg]
    kern = functools.partial(_flash_kernel, mode=mode, tq=tq, has_ctx=has_ctx, tk=tk)
    return pl.pallas_call(
        kern,
        out_shape=jax.ShapeDtypeStruct((mq, 8 * HEAD_DIM), BF16),
        grid=(batch, nh, nq),
        in_specs=in_specs,
        out_specs=o_spec,
        scratch_shapes=[pltpu.VMEM((reps * tq, HEAD_DIM), BF16)],
        compiler_params=_cparams(("parallel", "parallel", "parallel")),
        name="flash_" + mode,
    )(*args)


def _banded_kernel(start_ref, var_ref, q_ref, k_ref, v_ref, kc_ref, vc_ref, bias_ref, sink_ref, o_ref,
                   *, mode, tq, band):
    del var_ref
    hg = pl.program_id(1)
    j = pl.program_id(2)
    start = pl.multiple_of(start_ref[j], LANES)
    kb = k_ref[pl.ds(start, band), :]
    vb = v_ref[pl.ds(start, band), :]
    kc = kc_ref[...]
    vc = vc_ref[...]
    q = q_ref[...]
    dn = (((1,), (1,)), ((), ()))
    heads = [slice(gi * HEAD_DIM, (gi + 1) * HEAD_DIM) for gi in range(4)]
    if mode == "window":
        qs = jnp.concatenate([q[:, hd] for hd in heads], axis=0)
        s_loc = lax.dot_general(qs, kb, dn, preferred_element_type=F32)
        s_ctx = lax.dot_general(qs, kc, dn, preferred_element_type=F32)
        bias = jnp.concatenate([bias_ref[...]] * 4, axis=0)
    else:
        s_loc = jnp.concatenate([lax.dot_general(q[:, hd], kb[:, hd], dn, preferred_element_type=F32)
                                 for hd in heads], axis=0)
        s_ctx = jnp.concatenate([lax.dot_general(q[:, hd], kc[:, hd], dn, preferred_element_type=F32)
                                 for hd in heads], axis=0)
        bias = bias_ref[...]
    s_loc = s_loc + bias
    n_loc = band // LANES
    slabs = ([s_loc[:, c * LANES:(c + 1) * LANES] for c in range(n_loc)]
             + [s_ctx[:, c * LANES:(c + 1) * LANES] for c in range(s_ctx.shape[1] // LANES)])
    mx = functools.reduce(jnp.maximum, slabs)
    m = jnp.broadcast_to(jnp.max(mx, axis=-1, keepdims=True), mx.shape)
    if mode == "window":
        sink = jnp.concatenate([jnp.full((tq, LANES), sink_ref[hg * 4 + gi] * LOG2E, F32) for gi in range(4)], axis=0)
        m = jnp.maximum(m, sink)
    ps = [jnp.exp2(sl - m) for sl in slabs]
    l = jnp.sum(functools.reduce(jnp.add, ps), axis=-1, keepdims=True)
    if mode == "window":
        l = l + jnp.exp2(sink - m)[:, 0:1]
    p_loc = jnp.concatenate([x.astype(BF16) for x in ps[:n_loc]], axis=1)
    p_ctx = jnp.concatenate([x.astype(BF16) for x in ps[n_loc:]], axis=1)
    if mode == "window":
        o = jnp.dot(p_loc, vb, preferred_element_type=F32) + jnp.dot(p_ctx, vc, preferred_element_type=F32)
        o = (o / l).astype(o_ref.dtype)
        for gi in range(4):
            o_ref[:, heads[gi]] = o[gi * tq:(gi + 1) * tq, :]
    else:
        for gi in range(4):
            rows = slice(gi * tq, (gi + 1) * tq)
            o = (jnp.dot(p_loc[rows, :], vb[:, heads[gi]], preferred_element_type=F32)
                 + jnp.dot(p_ctx[rows, :], vc[:, heads[gi]], preferred_element_type=F32))
            o_ref[:, heads[gi]] = (o / l[rows, :]).astype(o_ref.dtype)


def _banded(mode, proj, proj_ctx, batch, starts, variants, bias, sinks, qcol, kcol, vcol, kc_col, vc_col):
    m = proj.shape[0]
    s = m // batch
    tq = 128
    nq = s // tq
    nc = proj_ctx.shape[0] // batch
    band = bias.shape[-1]
    qw = 4 * HEAD_DIM
    if mode == "window":
        kw, div = HEAD_DIM, 1
        bias_spec = pl.BlockSpec((None, tq, band), lambda b, h, j, st, va: (va[j], 0, 0))
    else:
        kw, div = qw, 4
        bias_spec = pl.BlockSpec((None, None, 4 * tq, band), lambda b, h, j, st, va: (h, va[j], 0, 0))
    grid_spec = pltpu.PrefetchScalarGridSpec(
        num_scalar_prefetch=2,
        grid=(batch, 2, nq),
        in_specs=[
            pl.BlockSpec((tq, qw), lambda b, h, j, st, va: (b * nq + j, qcol // 4 + h)),
            pl.BlockSpec((s, kw), lambda b, h, j, st, va: (b, kcol // div + h)),
            pl.BlockSpec((s, kw), lambda b, h, j, st, va: (b, vcol // div + h)),
            pl.BlockSpec((nc, kw), lambda b, h, j, st, va: (b, kc_col // div + h)),
            pl.BlockSpec((nc, kw), lambda b, h, j, st, va: (b, vc_col // div + h)),
            bias_spec,
            pl.BlockSpec(memory_space=pltpu.SMEM),
        ],
        out_specs=pl.BlockSpec((tq, qw), lambda b, h, j, st, va: (b * nq + j, h)),
    )
    kern = functools.partial(_banded_kernel, mode=mode, tq=tq, band=band)
    return pl.pallas_call(
        kern,
        out_shape=jax.ShapeDtypeStruct((m, 8 * HEAD_DIM), BF16),
        grid_spec=grid_spec,
        compiler_params=_cparams(("parallel", "parallel", "arbitrary")),
        name="banded_" + mode,
    )(starts, variants, proj, proj, proj, proj_ctx, proj_ctx, bias, sinks)


def _outproj_kernel(*refs, with_h):
    oa_ref, ob_ref, wa_ref, wb_ref, x_ref, gt_ref = refs[:6]
    acc = jnp.dot(oa_ref[...], wa_ref[...], preferred_element_type=F32)
    acc = acc + jnp.dot(ob_ref[...], wb_ref[...], preferred_element_type=F32)
    xo = x_ref[...] + gt_ref[...] * acc
    if with_h:
        g_ref, sc_ref, sh_ref, o_ref, h_ref = refs[6:]
        h_ref[...] = _modulate(xo, g_ref[...], sc_ref[...], sh_ref[...]).astype(BF16)
    else:
        (o_ref,) = refs[6:]
    o_ref[...] = xo


def _outproj(oa, ob, w_bf, x2d, gate, rows_per_batch, tm, norm=None):
    m, d = x2d.shape
    half = oa.shape[1]
    tiles_per_batch = rows_per_batch // tm
    vec = pl.BlockSpec((None, 1, d), lambda i: (i // tiles_per_batch, 0, 0))
    tile = pl.BlockSpec((tm, d), lambda i: (i, 0))
    in_specs = [pl.BlockSpec((tm, half), lambda i: (i, 0)),
                pl.BlockSpec((tm, half), lambda i: (i, 0)),
                pl.BlockSpec((half, d), lambda i: (0, 0)),
                pl.BlockSpec((half, d), lambda i: (1, 0)),
                tile, vec]
    args = [oa, ob, w_bf, w_bf, x2d, gate]
    out_shape = jax.ShapeDtypeStruct((m, d), F32)
    out_specs = tile
    if norm is not None:
        in_specs += [pl.BlockSpec((1, d), lambda i: (0, 0)), vec, vec]
        args += [norm[0].reshape(1, d), norm[1], norm[2]]
        out_shape = (out_shape, jax.ShapeDtypeStruct((m, d), BF16))
        out_specs = (tile, tile)
    return pl.pallas_call(
        functools.partial(_outproj_kernel, with_h=norm is not None),
        out_shape=out_shape,
        grid=(m // tm,),
        in_specs=in_specs,
        out_specs=out_specs,
        compiler_params=_cparams(("parallel",)),
        name="outproj",
    )(*args)


def _swiglu_step(h, wg, wu, wd):
    g = jnp.dot(h, wg, preferred_element_type=F32)
    u = jnp.dot(h, wu, preferred_element_type=F32)
    a = (g * (1.0 / (1.0 + jnp.exp(-g))) * u).astype(BF16)
    return jnp.dot(a, wd, preferred_element_type=F32)


def _ffn_kernel(x_ref, h_ref, gt_ref, wg_ref, wu_ref, wd_ref, o_ref, acc_scr, *, nf):
    j = pl.program_id(1)

    @pl.when(j == 0)
    def _():
        acc_scr[...] = jnp.zeros_like(acc_scr)

    acc_scr[...] += _swiglu_step(h_ref[...], wg_ref[...], wu_ref[...], wd_ref[...])

    @pl.when(j == nf - 1)
    def _():
        o_ref[...] = x_ref[...] + gt_ref[...] * acc_scr[...]


def _ffn(x2d, h2d, rows_per_batch, gate, wg, wu, wd, tm):
    m, d = x2d.shape
    f = wg.shape[1]
    tf = 512
    nf = f // tf
    tiles_per_batch = rows_per_batch // tm
    return pl.pallas_call(
        functools.partial(_ffn_kernel, nf=nf),
        out_shape=jax.ShapeDtypeStruct((m, d), F32),
        grid=(m // tm, nf),
        in_specs=[pl.BlockSpec((tm, d), lambda i, j: (i, 0)),
                  pl.BlockSpec((tm, d), lambda i, j: (i, 0)),
                  pl.BlockSpec((None, 1, d), lambda i, j: (i // tiles_per_batch, 0, 0)),
                  pl.BlockSpec((d, tf), lambda i, j: (0, j)),
                  pl.BlockSpec((d, tf), lambda i, j: (0, j)),
                  pl.BlockSpec((tf, d), lambda i, j: (j, 0))],
        out_specs=pl.BlockSpec((tm, d), lambda i, j: (i, 0)),
        scratch_shapes=[pltpu.VMEM((tm, d), F32)],
        compiler_params=_cparams(("parallel", "arbitrary")),
        name="ffn",
    )(x2d, h2d, gate, wg, wu, wd)


def _router_kernel(x_ref, g_ref, sc_ref, sh_ref, wr_ref, h_ref, ri_ref, rg_ref, *, n_exp, n_tiles):
    h = _modulate(x_ref[...], g_ref[...], sc_ref[...], sh_ref[...])
    h_ref[...] = jnp.where(pl.program_id(0) < n_tiles, h, 0.0)
    logits = jnp.dot(h, wr_ref[...], preferred_element_type=F32, precision=lax.Precision.HIGHEST)
    lane = lax.broadcasted_iota(jnp.int32, logits.shape, 1)
    logits = jnp.where(lane < n_exp, logits, -jnp.inf)
    lane_f = lane.astype(F32)
    m1 = jnp.max(logits, axis=-1, keepdims=True)
    i1 = jnp.min(jnp.where(logits == m1, lane_f, float(LANES)), axis=-1, keepdims=True)
    rest = jnp.where(lane_f == i1, -jnp.inf, logits)
    m2 = jnp.max(rest, axis=-1, keepdims=True)
    i2 = jnp.min(jnp.where(rest == m2, lane_f, float(LANES)), axis=-1, keepdims=True)
    e = jnp.exp(m2 - m1)
    g1 = 1.0 / (1.0 + e)
    g2 = e / (1.0 + e)
    ri_ref[...] = jnp.where(lane == 0, i1, jnp.where(lane == 1, i2, 0.0)).astype(jnp.int32)
    rg_ref[...] = jnp.where(lane == 0, g1, jnp.where(lane == 1, g2, 0.0))


def _router(x2d, rows_per_batch, g, scale, shift, wr_pad, n_exp, tm):
    m, d = x2d.shape
    tiles_per_batch = rows_per_batch // tm
    n_tiles = m // tm
    last = n_tiles - 1
    vec = pl.BlockSpec((None, 1, d), lambda i: (jnp.minimum(i, last) // tiles_per_batch, 0, 0))
    return pl.pallas_call(
        functools.partial(_router_kernel, n_exp=n_exp, n_tiles=n_tiles),
        out_shape=(jax.ShapeDtypeStruct((m + tm, d), F32),
                   jax.ShapeDtypeStruct((m + tm, LANES), jnp.int32),
                   jax.ShapeDtypeStruct((m + tm, LANES), F32)),
        grid=(n_tiles + 1,),
        in_specs=[pl.BlockSpec((tm, d), lambda i: (jnp.minimum(i, last), 0)),
                  pl.BlockSpec((1, d), lambda i: (0, 0)),
                  vec, vec,
                  pl.BlockSpec((d, LANES), lambda i: (0, 0))],
        out_specs=(pl.BlockSpec((tm, d), lambda i: (i, 0)),
                   pl.BlockSpec((tm, LANES), lambda i: (i, 0)),
                   pl.BlockSpec((tm, LANES), lambda i: (i, 0))),
        compiler_params=_cparams(("parallel",)),
        name="router",
    )(x2d, g.reshape(1, d), scale, shift, wr_pad)


def _moe_kernel(be_ref, nv_ref, tok_ref, h_hbm, wg_ref, wu_ref, wd_ref, o_ref, xbuf, xb_scr, sem,
                *, nf, tm, nblk, per_step):
    del be_ref
    i = pl.program_id(0)
    j = pl.program_id(1)
    n_valid = nv_ref[i]
    half = tm // 2
    n_copies = nf * per_step

    def start_row(blk, r):
        tok = tok_ref[blk * tm + jnp.minimum(r, tm - 1)]
        pltpu.make_async_copy(h_hbm.at[pl.ds(tok, 1)], xbuf.at[pl.ds(r, 1)], sem).start()

    def wait_all():
        pltpu.make_async_copy(h_hbm.at[pl.ds(0, n_copies)], xbuf.at[pl.ds(0, n_copies)], sem).wait()

    def start_next():
        nxt = jnp.minimum(i + 1, nblk - 1)
        for u in range(per_step):
            start_row(nxt, j * per_step + u)

    @pl.when(j == 0)
    def _():
        @pl.when(i == 0)
        def _():
            def body(r, carry):
                start_row(0, r)
                return carry
            lax.fori_loop(0, n_copies, body, 0)

        @pl.when(jnp.logical_or(i == 0, nv_ref[jnp.maximum(i - 1, 0)] > 0))
        def _():
            wait_all()
            xb_scr[...] = xbuf[0:tm, :].astype(BF16)

        o_ref[...] = jnp.zeros_like(o_ref)

    def accumulate(rows):
        start_next()
        o_ref[rows, :] += _swiglu_step(xb_scr[rows, :], wg_ref[...].astype(BF16), wu_ref[...].astype(BF16),
                                       wd_ref[...].astype(BF16))

    @pl.when(n_valid > half)
    def _():
        accumulate(slice(0, tm))

    @pl.when(jnp.logical_and(n_valid > 0, n_valid <= half))
    def _():
        accumulate(slice(0, half))

    @pl.when(jnp.logical_and(jnp.logical_and(i == nblk - 1, j == nf - 1), n_valid > 0))
    def _():
        wait_all()


def _moe(slot_tok, h_pad, block_expert, n_valid, wg, wu, wd, tm):
    cap = slot_tok.shape[0]
    d = h_pad.shape[1]
    f = wg.shape[2]
    tf = 256
    nf = f // tf
    nblk = cap // tm
    per_step = -(-tm // nf)
    while (nf * per_step) % 8:
        per_step += 1
    buf_rows = nf * per_step

    def col_j(i, j, nv):
        return jnp.where(nv[i] > 0, j, nf - 1)

    grid_spec = pltpu.PrefetchScalarGridSpec(
        num_scalar_prefetch=3,
        grid=(nblk, nf),
        in_specs=[pl.BlockSpec(memory_space=pl.ANY),
                  pl.BlockSpec((None, d, tf), lambda i, j, be, nv, tok: (be[i], 0, col_j(i, j, nv))),
                  pl.BlockSpec((None, d, tf), lambda i, j, be, nv, tok: (be[i], 0, col_j(i, j, nv))),
                  pl.BlockSpec((None, tf, d), lambda i, j, be, nv, tok: (be[i], col_j(i, j, nv), 0))],
        out_specs=pl.BlockSpec((tm, d), lambda i, j, be, nv, tok: (i, 0)),
        scratch_shapes=[pltpu.VMEM((buf_rows, d), F32), pltpu.VMEM((tm, d), BF16), pltpu.SemaphoreType.DMA],
    )
    return pl.pallas_call(
        functools.partial(_moe_kernel, nf=nf, tm=tm, nblk=nblk, per_step=per_step),
        out_shape=jax.ShapeDtypeStruct((cap, d), F32),
        grid_spec=grid_spec,
        compiler_params=_cparams(("arbitrary", "arbitrary")),
        name="moe",
    )(block_expert, n_valid, slot_tok, h_pad, wg, wu, wd)


def _combine_kernel(slot_ref, x_ref, gt_ref, rg_ref, fg_ref, y_hbm, o_ref, buf, sem, *, tc, n_steps):
    i = pl.program_id(0)

    def start_tile(t, slot):
        def issue(r, carry):
            for k in range(TOP_K):
                s = slot_ref[TOP_K * (t * tc + r) + k]
                pltpu.make_async_copy(y_hbm.at[pl.ds(s, 1)], buf.at[slot, pl.ds(k * tc + r, 1)],
                                      sem.at[slot]).start()
            return carry
        lax.fori_loop(0, tc, issue, 0, unroll=8)

    @pl.when(i == 0)
    def _():
        start_tile(0, 0)

    @pl.when(i + 1 < n_steps)
    def _():
        start_tile(i + 1, (i + 1) % 2)

    cur = i % 2
    pltpu.make_async_copy(y_hbm.at[pl.ds(0, TOP_K * tc)], buf.at[cur], sem.at[cur]).wait()
    rg = rg_ref[...]
    y = rg[:, 0:1] * buf[cur, 0:tc, :] + rg[:, 1:2] * buf[cur, tc:2 * tc, :]
    xo = x_ref[...] + gt_ref[...] * y
    ms = jnp.mean(xo * xo, axis=-1, keepdims=True)
    o_ref[...] = xo * lax.rsqrt(ms + EPS) * fg_ref[...]


def _combine(slot, x2d, rows_per_batch, gate, route_g, final_g, y, tc):
    m, d = x2d.shape
    tiles_per_batch = rows_per_batch // tc
    grid_spec = pltpu.PrefetchScalarGridSpec(
        num_scalar_prefetch=1,
        grid=(m // tc,),
        in_specs=[pl.BlockSpec((tc, d), lambda i, sl: (i, 0)),
                  pl.BlockSpec((None, 1, d), lambda i, sl: (i // tiles_per_batch, 0, 0)),
                  pl.BlockSpec((tc, LANES), lambda i, sl: (i, 0)),
                  pl.BlockSpec((1, d), lambda i, sl: (0, 0)),
                  pl.BlockSpec(memory_space=pl.ANY)],
        out_specs=pl.BlockSpec((tc, d), lambda i, sl: (i, 0)),
        scratch_shapes=[pltpu.VMEM((2, TOP_K * tc, d), F32), pltpu.SemaphoreType.DMA((2,))],
    )
    return pl.pallas_call(
        functools.partial(_combine_kernel, tc=tc, n_steps=m // tc),
        out_shape=jax.ShapeDtypeStruct((m, d), F32),
        grid_spec=grid_spec,
        compiler_params=_cparams(("arbitrary",)),
        name="combine",
    )(slot, x2d, gate, route_g, final_g.reshape(1, d), y)


def _rope_tables(n_tok, dim):
    t = jnp.arange(n_tok)
    pos = jnp.stack([t // GRID_W, t % GRID_W], axis=-1).astype(F32)
    n_pairs = dim // 4
    freq = ROPE_BASE ** (-jnp.arange(n_pairs, dtype=F32) / n_pairs)
    ang = (pos[:, :, None] * freq).reshape(n_tok, 2 * n_pairs)
    cos = jnp.tile(jnp.cos(ang), (1, HEAD_DIM // dim))
    sin = jnp.tile(jnp.sin(ang), (1, HEAD_DIM // dim))
    return jnp.concatenate([cos, cos], axis=1), jnp.concatenate([-sin, sin], axis=1)


def _deinterleave(w, comps):
    d, n = w.shape
    pairs = HEAD_DIM // (2 * comps)
    return w.reshape(d, n // HEAD_DIM, comps, pairs, 2).transpose(0, 1, 4, 2, 3).reshape(d, n)


def _dedupe(patterns):
    keys, variant = {}, []
    for p in patterns:
        variant.append(keys.setdefault(p.tobytes(), len(keys)))
    first = [variant.index(v) for v in range(len(keys))]
    return np.asarray(variant, np.int32), first


def _window_tables(s):
    tq = 128
    nb = s // tq
    band = 3 * tq
    starts = np.clip(np.arange(nb) - 1, 0, nb - 3) * tq
    qpos = np.arange(nb)[:, None] * tq + np.arange(tq)[None, :]
    kpos = starts[:, None] + np.arange(band)[None, :]
    ok = np.abs(kpos[:, None, :] - qpos[:, :, None]) <= C_WINDOW
    variant, first = _dedupe(list(ok))
    bias = np.where(ok[first], 0.0, NEG_INF).astype(np.float32)
    return starts.astype(np.int32), variant, jnp.asarray(bias)


def _na_tables(s, rpb):
    tq = 128
    rows = s // GRID_W
    wr = min(NA_ROWS, rows)
    nb = s // tq
    band_blocks = 5
    band = band_blocks * tq
    starts = np.clip(np.arange(nb) - 2, 0, nb - band_blocks) * tq
    n_h = rpb.shape[0]
    n_dc = 2 * NA_COLS - 1
    period = GRID_W + n_dc - 1
    wrap = jnp.zeros((n_h, 2 * NA_ROWS - 1, period), F32)
    wrap = wrap.at[..., :NA_COLS].set(rpb[..., NA_COLS - 1:].astype(F32))
    wrap = wrap.at[..., period - (NA_COLS - 1):].set(rpb[..., :NA_COLS - 1].astype(F32))
    toep = jnp.tile(wrap, (1, 1, GRID_W))[..., :GRID_W * (period - 1)]
    toep = toep.reshape(n_h, 2 * NA_ROWS - 1, GRID_W, period - 1)[..., :GRID_W]
    qc = np.arange(GRID_W)[:, None]
    kc = np.arange(GRID_W)[None, :]
    cstart = np.clip(qc - NA_COLS // 2, 0, GRID_W - NA_COLS)
    col_ok = (kc >= cstart) & (kc < cstart + NA_COLS)
    toep = jnp.where(jnp.asarray(col_ok), toep * LOG2E, NEG_INF)
    masked = jnp.full((n_h, GRID_W, GRID_W), NEG_INF, F32)
    pats = []
    for j in range(nb):
        pat = np.full((tq // GRID_W, band // GRID_W), -1, np.int32)
        for a in range(tq // GRID_W):
            qr = j * (tq // GRID_W) + a
            rs = min(max(qr - wr // 2, 0), rows - wr)
            for w in range(band // GRID_W):
                kr = starts[j] // GRID_W + w
                if rs <= kr < rs + wr:
                    pat[a, w] = kr - qr + NA_ROWS - 1
        pats.append(pat)
    variant, first = _dedupe(pats)
    tiles = []
    for j in first:
        rows_ = [jnp.concatenate([toep[:, dr] if dr >= 0 else masked for dr in pats[j][a]], axis=-1)
                 for a in range(tq // GRID_W)]
        tiles.append(jnp.concatenate(rows_, axis=-2))
    bias = jnp.stack(tiles, axis=1)
    bias = bias.reshape(n_h // 4, 4, len(first), tq, band).transpose(0, 2, 1, 3, 4)
    return starts.astype(np.int32), variant, bias.reshape(n_h // 4, len(first), 4 * tq, band)


def kernel(x, c, ctx, c_ctx, l0_norm1_g, l0_norm2_g, l0_w_ada, l0_b_ada, l0_w_in, l0_w_out, l0_lam_q1, l0_lam_k1, l0_lam_q2, l0_lam_k2, l0_subln_g, l0_q_norm_g, l0_k_norm_g, l0_ffn_w_gate, l0_ffn_w_up, l0_ffn_w_down, l1_norm1_g, l1_norm2_g, l1_w_ada, l1_b_ada, l1_w_in, l1_w_out, l1_sinks, l1_rpb, l1_router, l1_exp_w_gate, l1_exp_w_up, l1_exp_w_down, final_norm_g):
    b, s, d = x.shape
    n_ctx = ctx.shape[1]
    n_exp = l1_router.shape[1]
    n = b * s
    assert d == 2048 and s % 512 == 0 and n_ctx == 256 and s // GRID_W >= 10

    x2d = x.reshape(n, d)
    xc2d = ctx.reshape(b * n_ctx, d)

    cvecs = jnp.zeros((8, d), F32).at[:b].set(c).at[b].set(c_ctx)
    mod0 = _ada(cvecs, l0_w_ada, l0_b_ada)
    mod1 = _ada(cvecs, l1_w_ada, l1_b_ada)

    def lat(mod, k):
        return mod[:b, k * d:(k + 1) * d].reshape(b, 1, d)

    def cx(mod, k):
        return jnp.broadcast_to(mod[b, k * d:(k + 1) * d].reshape(1, 1, d), (b, 1, d))

    cos_a, sin_a = _rope_tables(s, A_SUB)
    cos_b, sin_b = _rope_tables(s, HEAD_DIM)
    cos_tab = jnp.stack([cos_a, cos_b])
    sin_tab = jnp.stack([sin_a, sin_b])
    tm = 512
    hq = 8 * HEAD_DIM

    qa = T_ROPE | T_SCALE_A
    qb = T_ROPE | T_QNORM | T_SCALE_B
    kb = T_ROPE | T_KNORM
    types0 = [qa] * 8 + [qb] * 8 + [T_ROPE] * 8 + [0] * 8 + [kb] * 2 + [0] * 2
    tabs0 = [0] * 8 + [1] * 8 + [0] * 8 + [0] * 8 + [1] * 2 + [0] * 2
    qkg0 = _deinterleave(jnp.stack([l0_q_norm_g, l0_k_norm_g]).astype(F32), 1)
    w0 = l0_w_in.astype(BF16)
    w_in0 = jnp.concatenate([_deinterleave(w0[:, :hq], 2), _deinterleave(w0[:, hq:2 * hq], 1),
                             _deinterleave(w0[:, 2 * hq:3 * hq], 2), w0[:, 3 * hq:4 * hq],
                             _deinterleave(w0[:, 4 * hq:4 * hq + 2 * HEAD_DIM], 1),
                             w0[:, 4 * hq + 2 * HEAD_DIM:]], axis=1)
    p0 = _inproj(x2d, s, l0_norm1_g, lat(mod0, 1), lat(mod0, 0), w_in0, cos_tab, sin_tab, qkg0,
                 types0, tabs0, tm)
    rc = b * n_ctx
    tmc = tm if rc % tm == 0 else n_ctx
    p0c = _inproj(xc2d, rc, l0_norm1_g, cx(mod0, 1), cx(mod0, 0), w_in0, cos_tab, sin_tab, qkg0,
                  [t & ~T_ROPE for t in types0], tabs0, tmc)
    lam = jnp.stack([l0_lam_q1, l0_lam_k1, l0_lam_q2, l0_lam_k2]).astype(F32)
    subg = l0_subln_g.reshape(1, HEAD_DIM).astype(F32)
    tk = min(2048, s)
    oa = _flash("diff", p0, p0, p0c, b, 512, tk, 0, 16, 24, lam, subg)
    ob = _flash("gqa", p0, p0, p0c, b, 256, tk, 8, 32, 34)
    oac = _flash("diff", p0c, p0c, None, b, n_ctx, n_ctx, 0, 16, 24, lam, subg)
    obc = _flash("gqa", p0c, p0c, None, b, n_ctx, n_ctx, 8, 32, 34)
    w_out0 = l0_w_out.astype(BF16)
    x1, h1 = _outproj(oa, ob, w_out0, x2d, lat(mod0, 2), s, tm, (l0_norm2_g, lat(mod0, 4), lat(mod0, 3)))
    xc1, hc1 = _outproj(oac, obc, w_out0, xc2d, cx(mod0, 2), rc, tmc, (l0_norm2_g, cx(mod0, 4), cx(mod0, 3)))
    wg0, wu0, wd0 = l0_ffn_w_gate.astype(BF16), l0_ffn_w_up.astype(BF16), l0_ffn_w_down.astype(BF16)
    x2 = _ffn(x1, h1, s, lat(mod0, 5), wg0, wu0, wd0, tm)
    xc2 = _ffn(xc1, hc1, rc, cx(mod0, 5), wg0, wu0, wd0, tmc)

    types1 = [T_ROPE | T_SCALE_B] * 8 + [T_SCALE_B] * 8 + [T_ROPE] * 2 + [0] * 2 + [0] * 16
    tabs1 = [1] * len(types1)
    w1 = l1_w_in.astype(BF16)
    w_in1 = jnp.concatenate([_deinterleave(w1[:, :hq], 1), w1[:, hq:2 * hq],
                             _deinterleave(w1[:, 2 * hq:2 * hq + 2 * HEAD_DIM], 1),
                             w1[:, 2 * hq + 2 * HEAD_DIM:]], axis=1)
    qkg1 = jnp.ones((2, HEAD_DIM), F32)
    p1 = _inproj(x2, s, l1_norm1_g, lat(mod1, 1), lat(mod1, 0), w_in1, cos_tab, sin_tab, qkg1,
                 types1, tabs1, tm)
    l1_q = 16 * HEAD_DIM
    n_kv_heads = (w_in1.shape[1] - l1_q) // HEAD_DIM
    p1c = _inproj(xc2, rc, l1_norm1_g, cx(mod1, 1), cx(mod1, 0), w_in1[:, l1_q:], cos_tab, sin_tab, qkg1,
                  [0] * n_kv_heads, [0] * n_kv_heads, tmc)
    w_st, w_var, w_bias = _window_tables(s)
    n_st, n_var, n_bias = _na_tables(s, l1_rpb)
    sinks = l1_sinks.astype(F32)
    oc = _banded("window", p1, p1c, b, jnp.asarray(w_st), jnp.asarray(w_var), w_bias, sinks, 0, 16, 18, 0, 2)
    od = _banded("na", p1, p1c, b, jnp.asarray(n_st), jnp.asarray(n_var), n_bias, sinks, 8, 20, 28, 4, 12)
    x3 = _outproj(oc, od, l1_w_out.astype(BF16), x2, lat(mod1, 2), s, tm)

    wr_pad = jnp.zeros((d, LANES), F32).at[:, :n_exp].set(l1_router.astype(F32))
    h2, route_i, route_g = _router(x3, s, l1_norm2_g, lat(mod1, 4), lat(mod1, 3), wr_pad, n_exp, tm)
    route_g = route_g[:n]
    e_flat = route_i[:n, :TOP_K].reshape(-1)
    onehot = (e_flat[:, None] == jnp.arange(n_exp, dtype=jnp.int32)[None, :]).astype(jnp.int32)
    csum = jnp.cumsum(onehot, axis=0)
    rank = jnp.sum((csum - onehot) * onehot, axis=1)
    counts = csum[-1]
    tmo = 2 * tm
    padded = (counts + tmo - 1) // tmo * tmo
    pad_end = jnp.cumsum(padded)
    pad_start = pad_end - padded
    slot = (pad_start[e_flat] + rank).astype(jnp.int32)
    n_blocks = (n * TOP_K + n_exp * (tmo - 1) + tmo - 1) // tmo
    cap = n_blocks * tmo
    tok_flat = jnp.repeat(jnp.arange(n, dtype=jnp.int32), TOP_K)
    slot_tok = jnp.full((cap,), n, jnp.int32).at[slot].set(tok_flat)
    block_row0 = jnp.arange(n_blocks, dtype=jnp.int32) * tmo
    block_expert = jnp.minimum(jnp.searchsorted(pad_end, block_row0, side="right"), n_exp - 1).astype(jnp.int32)
    n_valid = jnp.clip((pad_start + counts)[block_expert] - block_row0, 0, tmo).astype(jnp.int32)

    y = _moe(slot_tok, h2, block_expert, n_valid, l1_exp_w_gate, l1_exp_w_up, l1_exp_w_down, tmo)
    out = _combine(slot, x3, s, lat(mod1, 5), route_g, final_norm_g, y, 256)
    return out.reshape(b, s, d)
```

```python
import functools
import math

import numpy as np
import jax
import jax.numpy as jnp
from jax import lax
from jax.experimental import pallas as pl
from jax.experimental.pallas import tpu as pltpu

F32 = jnp.float32
BF16 = jnp.bfloat16

HEAD_DIM = 128
LANES = 128
GRID_W = 64
ROPE_BASE = 10000.0
EPS = 1e-6
NEG_INF = -1e30
A_SUB = HEAD_DIM // 2
A_SCALE = A_SUB ** -0.5
ATT_SCALE = HEAD_DIM ** -0.5
LAMBDA_INIT_L0 = 0.8 - 0.6 * math.exp(-0.3 * 0)
LOG2E = 1.4426950408889634
C_WINDOW = 128
NA_ROWS = 8
NA_COLS = 16
TOP_K = 2
VMEM_LIMIT = 56 * 1024 * 1024
ROW_TILE = 512
FF_TILE = 512
EXPERT_FF_TILE = 256
ADA_COL_TILE = 1024
FLASH_TQ_DIFF, FLASH_TQ_GQA = 1024, 512
FLASH_TK = 2048
BAND_TQ = 128
COMBINE_TILE = 512
T_ROPE, T_QNORM, T_KNORM, T_SCALE_B, T_SCALE_A = 1, 2, 4, 8, 16


def _cparams(sem):
    return pltpu.CompilerParams(dimension_semantics=sem, vmem_limit_bytes=VMEM_LIMIT)


def _modulate(xf, g, scale, shift):
    ms = jnp.mean(xf * xf, axis=-1, keepdims=True)
    y = xf * lax.rsqrt(ms + EPS) * g
    return y * (1.0 + scale) + shift


def _ada_kernel(c_ref, w_ref, b_ref, o_ref):
    c = c_ref[...]
    a = c * (1.0 / (1.0 + jnp.exp(-c)))
    o_ref[...] = jnp.dot(a.astype(BF16), w_ref[...].astype(BF16), preferred_element_type=F32) + b_ref[...]


def _ada(cvecs, w_ada, b_ada):
    r, d = cvecs.shape
    n = w_ada.shape[1]
    tn = ADA_COL_TILE
    return pl.pallas_call(
        _ada_kernel,
        out_shape=jax.ShapeDtypeStruct((r, n), F32),
        grid=(n // tn,),
        in_specs=[pl.BlockSpec((r, d), lambda j: (0, 0)),
                  pl.BlockSpec((d, tn), lambda j: (0, j)),
                  pl.BlockSpec((1, tn), lambda j: (0, j))],
        out_specs=pl.BlockSpec((r, tn), lambda j: (0, j)),
        compiler_params=_cparams(("parallel",)),
        name="ada",
    )(cvecs, w_ada, b_ada.reshape(1, n))


def _inproj_kernel(x_ref, g_ref, sc_ref, sh_ref, w_ref, cos_ref, sin_ref, qkg_ref, o_ref, h_scr, *, types, tabs):
    h_scr[...] = _modulate(x_ref[...], g_ref[...], sc_ref[...], sh_ref[...]).astype(BF16)
    chunk = 4 * HEAD_DIM
    for c in range(len(types) // 4):
        acc = jnp.dot(h_scr[...], w_ref[:, c * chunk:(c + 1) * chunk], preferred_element_type=F32)
        for hh in range(4):
            t = types[4 * c + hh]
            y = acc[:, hh * HEAD_DIM:(hh + 1) * HEAD_DIM]
            if t & (T_QNORM | T_KNORM):
                ms = jnp.mean(y * y, axis=-1, keepdims=True)
                row = 0 if t & T_QNORM else 1
                y = y * lax.rsqrt(ms + EPS) * qkg_ref[row:row + 1, :]
            if t & T_ROPE:
                tb = tabs[4 * c + hh]
                y = y * cos_ref[tb] + pltpu.roll(y, HEAD_DIM // 2, 1) * sin_ref[tb]
            if t & T_SCALE_B:
                y = y * (ATT_SCALE * LOG2E)
            elif t & T_SCALE_A:
                y = y * (A_SCALE * LOG2E)
            col = (4 * c + hh) * HEAD_DIM
            o_ref[:, col:col + HEAD_DIM] = y.astype(o_ref.dtype)


def _inproj(x2d, rows_per_batch, g, scale, shift, w_bf, cos_tab, sin_tab, qkg, types, tabs, tm):
    m, d = x2d.shape
    n = w_bf.shape[1]
    tiles_per_batch = rows_per_batch // tm
    tab_rows = cos_tab.shape[1] // tm
    kern = functools.partial(_inproj_kernel, types=tuple(int(t) for t in types), tabs=tuple(int(t) for t in tabs))
    return pl.pallas_call(
        kern,
        out_shape=jax.ShapeDtypeStruct((m, n), BF16),
        grid=(m // tm,),
        in_specs=[
            pl.BlockSpec((tm, d), lambda i: (i, 0)),
            pl.BlockSpec((1, d), lambda i: (0, 0)),
            pl.BlockSpec((None, 1, d), lambda i: (i // tiles_per_batch, 0, 0)),
            pl.BlockSpec((None, 1, d), lambda i: (i // tiles_per_batch, 0, 0)),
            pl.BlockSpec((d, n), lambda i: (0, 0), pipeline_mode=pl.Buffered(1)),
            pl.BlockSpec((2, tm, LANES), lambda i: (0, i % tab_rows, 0)),
            pl.BlockSpec((2, tm, LANES), lambda i: (0, i % tab_rows, 0)),
            pl.BlockSpec((2, HEAD_DIM), lambda i: (0, 0)),
        ],
        out_specs=pl.BlockSpec((tm, n), lambda i: (i, 0)),
        scratch_shapes=[pltpu.VMEM((tm, d), BF16)],
        compiler_params=_cparams(("parallel",)),
        name="inproj",
    )(x2d, g.reshape(1, d), scale, shift, w_bf, cos_tab, sin_tab, qkg)


def _flash_kernel(*refs, mode, tq, has_ctx, tk):
    refs = list(refs)
    q_ref, k_ref, v_ref = refs[:3]
    pos = 3
    if has_ctx:
        kc_ref, vc_ref = refs[3:5]
        pos = 5
    if mode == "diff":
        lam_ref, subg_ref = refs[pos:pos + 2]
        pos += 2
    o_ref, qs = refs[pos:pos + 2]

    q = q_ref[...]
    if mode == "diff":
        lane = lax.broadcasted_iota(jnp.int32, (1, LANES), 1)
        zero = jnp.zeros_like(q)
        first = (lane & (A_SUB // 2)) == 0
        qs[0:tq, :] = jnp.where(first, q, zero)
        qs[tq:2 * tq, :] = jnp.where(first, zero, q)
    else:
        for gi in range(4):
            qs[gi * tq:(gi + 1) * tq, :] = q[:, gi * HEAD_DIM:(gi + 1) * HEAD_DIM]

    def step(carry, k, v):
        s = lax.dot_general(qs[...], k, (((1,), (1,)), ((), ())), preferred_element_type=F32)
        slabs = [s[:, c * LANES:(c + 1) * LANES] for c in range(k.shape[0] // LANES)]
        mx = functools.reduce(jnp.maximum, slabs)
        row_max = jnp.broadcast_to(jnp.max(mx, axis=-1, keepdims=True), mx.shape)
        m_new = row_max if carry is None else jnp.maximum(carry[0], row_max)
        ps = [jnp.exp2(sl - m_new) for sl in slabs]
        p = jnp.concatenate([x.astype(BF16) for x in ps], axis=1)
        l_new = functools.reduce(jnp.add, ps)
        acc_new = jnp.dot(p, v, preferred_element_type=F32)
        if carry is not None:
            alpha = jnp.exp2(carry[0] - m_new)
            l_new = alpha * carry[1] + l_new
            acc_new = alpha * carry[2] + acc_new
        return m_new, l_new, acc_new

    carry = None
    for c in range(k_ref.shape[0] // tk):
        carry = step(carry, k_ref[c * tk:(c + 1) * tk, :], v_ref[c * tk:(c + 1) * tk, :])
    if has_ctx:
        carry = step(carry, kc_ref[...], vc_ref[...])
    _, l_fin, acc_fin = carry
    o = acc_fin / jnp.sum(l_fin, axis=-1, keepdims=True)
    if mode == "diff":
        lam1 = jnp.sum(lam_ref[0:1, :] * lam_ref[1:2, :], axis=-1, keepdims=True)
        lam2 = jnp.sum(lam_ref[2:3, :] * lam_ref[3:4, :], axis=-1, keepdims=True)
        lam = jnp.exp(lam1) - jnp.exp(lam2) + LAMBDA_INIT_L0
        od = o[0:tq, :] - lam * o[tq:2 * tq, :]
        ms = jnp.mean(od * od, axis=-1, keepdims=True)
        on = od * lax.rsqrt(ms + EPS) * subg_ref[...]
        o_ref[...] = (on * (1.0 - LAMBDA_INIT_L0)).astype(o_ref.dtype)
    else:
        for gi in range(4):
            o_ref[:, gi * HEAD_DIM:(gi + 1) * HEAD_DIM] = o[gi * tq:(gi + 1) * tq, :].astype(o_ref.dtype)


def _flash(mode, q_arr, kv_arr, ctx_arr, batch, tq, tk, qcol, kcol, vcol, lam=None, subg=None):
    mq = q_arr.shape[0]
    sq = mq // batch
    sk = kv_arr.shape[0] // batch
    nq = sq // tq
    has_ctx = ctx_arr is not None
    if mode == "diff":
        nh, qw, reps = 8, HEAD_DIM, 2
        q_spec = pl.BlockSpec((tq, qw), lambda b, h, qi: (b * nq + qi, qcol + h))
        o_spec = pl.BlockSpec((tq, qw), lambda b, h, qi: (b * nq + qi, h))
    else:
        nh, qw, reps = 2, 4 * HEAD_DIM, 4
        q_spec = pl.BlockSpec((tq, qw), lambda b, h, qi: (b * nq + qi, qcol // 4 + h))
        o_spec = pl.BlockSpec((tq, qw), lambda b, h, qi: (b * nq + qi, h))
    in_specs = [q_spec,
                pl.BlockSpec((sk, HEAD_DIM), lambda b, h, qi: (b, kcol + h)),
                pl.BlockSpec((sk, HEAD_DIM), lambda b, h, qi: (b, vcol + h))]
    args = [q_arr, kv_arr, kv_arr]
    if has_ctx:
        nc = ctx_arr.shape[0] // batch
        in_specs += [pl.BlockSpec((nc, HEAD_DIM), lambda b, h, qi: (b, kcol + h)),
                     pl.BlockSpec((nc, HEAD_DIM), lambda b, h, qi: (b, vcol + h))]
        args += [ctx_arr, ctx_arr]
    if mode == "diff":
        in_specs += [pl.BlockSpec((4, A_SUB), lambda b, h, qi: (0, 0)),
                     pl.BlockSpec((1, HEAD_DIM), lambda b, h, qi: (0, 0))]
        args += [lam, subg]
    kern = functools.partial(_flash_kernel, mode=mode, tq=tq, has_ctx=has_ctx, tk=tk)
    return pl.pallas_call(
        kern,
        out_shape=jax.ShapeDtypeStruct((mq, 8 * HEAD_DIM), BF16),
        grid=(batch, nh, nq),
        in_specs=in_specs,
        out_specs=o_spec,
        scratch_shapes=[pltpu.VMEM((reps * tq, HEAD_DIM), BF16)],
        compiler_params=_cparams(("parallel", "parallel", "parallel")),
        name="flash_" + mode,
    )(*args)


def _banded_kernel(start_ref, var_ref, q_ref, k_ref, v_ref, kc_ref, vc_ref, bias_ref, sink_ref, o_ref,
                   *, mode, tq, band):
    del var_ref
    hg = pl.program_id(1)
    j = pl.program_id(2)
    start = pl.multiple_of(start_ref[j], LANES)
    kb = k_ref[pl.ds(start, band), :]
    vb = v_ref[pl.ds(start, band), :]
    kc = kc_ref[...]
    vc = vc_ref[...]
    q = q_ref[...]
    dn = (((1,), (1,)), ((), ()))
    heads = [slice(gi * HEAD_DIM, (gi + 1) * HEAD_DIM) for gi in range(4)]
    if mode == "window":
        qs = jnp.concatenate([q[:, hd] for hd in heads], axis=0)
        s_loc = lax.dot_general(qs, kb, dn, preferred_element_type=F32)
        s_ctx = lax.dot_general(qs, kc, dn, preferred_element_type=F32)
        bias = jnp.concatenate([bias_ref[...]] * 4, axis=0)
    else:
        s_loc = jnp.concatenate([lax.dot_general(q[:, hd], kb[:, hd], dn, preferred_element_type=F32)
                                 for hd in heads], axis=0)
        s_ctx = jnp.concatenate([lax.dot_general(q[:, hd], kc[:, hd], dn, preferred_element_type=F32)
                                 for hd in heads], axis=0)
        bias = bias_ref[...]
    s_loc = s_loc + bias
    n_loc = band // LANES
    slabs = ([s_loc[:, c * LANES:(c + 1) * LANES] for c in range(n_loc)]
             + [s_ctx[:, c * LANES:(c + 1) * LANES] for c in range(s_ctx.shape[1] // LANES)])
    mx = functools.reduce(jnp.maximum, slabs)
    m = jnp.broadcast_to(jnp.max(mx, axis=-1, keepdims=True), mx.shape)
    if mode == "window":
        sink = jnp.concatenate([jnp.full((tq, LANES), sink_ref[hg * 4 + gi] * LOG2E, F32) for gi in range(4)], axis=0)
        m = jnp.maximum(m, sink)
    ps = [jnp.exp2(sl - m) for sl in slabs]
    l = jnp.sum(functools.reduce(jnp.add, ps), axis=-1, keepdims=True)
    if mode == "window":
        l = l + jnp.exp2(sink - m)[:, 0:1]
    p_loc = jnp.concatenate([x.astype(BF16) for x in ps[:n_loc]], axis=1)
    p_ctx = jnp.concatenate([x.astype(BF16) for x in ps[n_loc:]], axis=1)
    if mode == "window":
        o = jnp.dot(p_loc, vb, preferred_element_type=F32) + jnp.dot(p_ctx, vc, preferred_element_type=F32)
        o = (o / l).astype(o_ref.dtype)
        for gi in range(4):
            o_ref[:, heads[gi]] = o[gi * tq:(gi + 1) * tq, :]
    else:
        for gi in range(4):
            rows = slice(gi * tq, (gi + 1) * tq)
            o = (jnp.dot(p_loc[rows, :], vb[:, heads[gi]], preferred_element_type=F32)
                 + jnp.dot(p_ctx[rows, :], vc[:, heads[gi]], preferred_element_type=F32))
            o_ref[:, heads[gi]] = (o / l[rows, :]).astype(o_ref.dtype)


def _banded(mode, proj, proj_ctx, batch, starts, variants, bias, sinks, qcol, kcol, vcol, kc_col, vc_col):
    m = proj.shape[0]
    s = m // batch
    tq = BAND_TQ
    nq = s // tq
    nc = proj_ctx.shape[0] // batch
    band = bias.shape[-1]
    qw = 4 * HEAD_DIM
    if mode == "window":
        kw, div = HEAD_DIM, 1
        bias_spec = pl.BlockSpec((None, tq, band), lambda b, h, j, st, va: (va[j], 0, 0))
    else:
        kw, div = qw, 4
        bias_spec = pl.BlockSpec((None, None, 4 * tq, band), lambda b, h, j, st, va: (h, va[j], 0, 0))
    grid_spec = pltpu.PrefetchScalarGridSpec(
        num_scalar_prefetch=2,
        grid=(batch, 2, nq),
        in_specs=[
            pl.BlockSpec((tq, qw), lambda b, h, j, st, va: (b * nq + j, qcol // 4 + h)),
            pl.BlockSpec((s, kw), lambda b, h, j, st, va: (b, kcol // div + h)),
            pl.BlockSpec((s, kw), lambda b, h, j, st, va: (b, vcol // div + h)),
            pl.BlockSpec((nc, kw), lambda b, h, j, st, va: (b, kc_col // div + h)),
            pl.BlockSpec((nc, kw), lambda b, h, j, st, va: (b, vc_col // div + h)),
            bias_spec,
            pl.BlockSpec(memory_space=pltpu.SMEM),
        ],
        out_specs=pl.BlockSpec((tq, qw), lambda b, h, j, st, va: (b * nq + j, h)),
    )
    kern = functools.partial(_banded_kernel, mode=mode, tq=tq, band=band)
    return pl.pallas_call(
        kern,
        out_shape=jax.ShapeDtypeStruct((m, 8 * HEAD_DIM), BF16),
        grid_spec=grid_spec,
        compiler_params=_cparams(("parallel", "parallel", "arbitrary")),
        name="banded_" + mode,
    )(starts, variants, proj, proj, proj, proj_ctx, proj_ctx, bias, sinks)


def _outproj_kernel(oa_ref, ob_ref, wa_ref, wb_ref, x_ref, gt_ref, o_ref):
    acc = jnp.dot(oa_ref[...], wa_ref[...], preferred_element_type=F32)
    acc = acc + jnp.dot(ob_ref[...], wb_ref[...], preferred_element_type=F32)
    o_ref[...] = x_ref[...] + gt_ref[...] * acc


def _outproj(oa, ob, w_bf, x2d, gate, rows_per_batch, tm):
    m, d = x2d.shape
    half = oa.shape[1]
    tiles_per_batch = rows_per_batch // tm
    return pl.pallas_call(
        _outproj_kernel,
        out_shape=jax.ShapeDtypeStruct((m, d), F32),
        grid=(m // tm,),
        in_specs=[pl.BlockSpec((tm, half), lambda i: (i, 0)),
                  pl.BlockSpec((tm, half), lambda i: (i, 0)),
                  pl.BlockSpec((half, d), lambda i: (0, 0)),
                  pl.BlockSpec((half, d), lambda i: (1, 0)),
                  pl.BlockSpec((tm, d), lambda i: (i, 0)),
                  pl.BlockSpec((None, 1, d), lambda i: (i // tiles_per_batch, 0, 0))],
        out_specs=pl.BlockSpec((tm, d), lambda i: (i, 0)),
        compiler_params=_cparams(("parallel",)),
        name="outproj",
    )(oa, ob, w_bf, w_bf, x2d, gate)


def _swiglu_step(h, wg, wu, wd):
    g = jnp.dot(h, wg, preferred_element_type=F32)
    u = jnp.dot(h, wu, preferred_element_type=F32)
    a = (g * (1.0 / (1.0 + jnp.exp(-g))) * u).astype(BF16)
    return jnp.dot(a, wd, preferred_element_type=F32)


def _ffn_kernel(x_ref, g_ref, sc_ref, sh_ref, gt_ref, wg_ref, wu_ref, wd_ref, o_ref, h_scr, acc_scr, *, nf):
    j = pl.program_id(1)

    @pl.when(j == 0)
    def _():
        h_scr[...] = _modulate(x_ref[...], g_ref[...], sc_ref[...], sh_ref[...]).astype(BF16)
        acc_scr[...] = jnp.zeros_like(acc_scr)

    acc_scr[...] += _swiglu_step(h_scr[...], wg_ref[...], wu_ref[...], wd_ref[...])

    @pl.when(j == nf - 1)
    def _():
        o_ref[...] = x_ref[...] + gt_ref[...] * acc_scr[...]


def _ffn(x2d, rows_per_batch, g, scale, shift, gate, wg, wu, wd, tm):
    m, d = x2d.shape
    f = wg.shape[1]
    tf = FF_TILE
    nf = f // tf
    tiles_per_batch = rows_per_batch // tm
    vec = pl.BlockSpec((None, 1, d), lambda i, j: (i // tiles_per_batch, 0, 0))
    return pl.pallas_call(
        functools.partial(_ffn_kernel, nf=nf),
        out_shape=jax.ShapeDtypeStruct((m, d), F32),
        grid=(m // tm, nf),
        in_specs=[pl.BlockSpec((tm, d), lambda i, j: (i, 0)),
                  pl.BlockSpec((1, d), lambda i, j: (0, 0)),
                  vec, vec, vec,
                  pl.BlockSpec((d, tf), lambda i, j: (0, j)),
                  pl.BlockSpec((d, tf), lambda i, j: (0, j)),
                  pl.BlockSpec((tf, d), lambda i, j: (j, 0))],
        out_specs=pl.BlockSpec((tm, d), lambda i, j: (i, 0)),
        scratch_shapes=[pltpu.VMEM((tm, d), BF16), pltpu.VMEM((tm, d), F32)],
        compiler_params=_cparams(("parallel", "arbitrary")),
        name="ffn",
    )(x2d, g.reshape(1, d), scale, shift, gate, wg, wu, wd)


def _router_kernel(x_ref, g_ref, sc_ref, sh_ref, wr_ref, h_ref, ri_ref, rg_ref, *, n_exp, n_tiles):
    h = _modulate(x_ref[...], g_ref[...], sc_ref[...], sh_ref[...])
    h_ref[...] = jnp.where(pl.program_id(0) < n_tiles, h, 0.0)
    logits = jnp.dot(h, wr_ref[...], preferred_element_type=F32, precision=lax.Precision.HIGHEST)
    lane = lax.broadcasted_iota(jnp.int32, logits.shape, 1)
    logits = jnp.where(lane < n_exp, logits, -jnp.inf)
    lane_f = lane.astype(F32)
    m1 = jnp.max(logits, axis=-1, keepdims=True)
    i1 = jnp.min(jnp.where(logits == m1, lane_f, float(LANES)), axis=-1, keepdims=True)
    rest = jnp.where(lane_f == i1, -jnp.inf, logits)
    m2 = jnp.max(rest, axis=-1, keepdims=True)
    i2 = jnp.min(jnp.where(rest == m2, lane_f, float(LANES)), axis=-1, keepdims=True)
    e = jnp.exp(m2 - m1)
    g1 = 1.0 / (1.0 + e)
    g2 = e / (1.0 + e)
    ri_ref[...] = jnp.where(lane == 0, i1, jnp.where(lane == 1, i2, 0.0)).astype(jnp.int32)
    rg_ref[...] = jnp.where(lane == 0, g1, jnp.where(lane == 1, g2, 0.0))


def _router(x2d, rows_per_batch, g, scale, shift, wr_pad, n_exp, tm):
    m, d = x2d.shape
    tiles_per_batch = rows_per_batch // tm
    n_tiles = m // tm
    last = n_tiles - 1
    vec = pl.BlockSpec((None, 1, d), lambda i: (jnp.minimum(i, last) // tiles_per_batch, 0, 0))
    return pl.pallas_call(
        functools.partial(_router_kernel, n_exp=n_exp, n_tiles=n_tiles),
        out_shape=(jax.ShapeDtypeStruct((m + tm, d), F32),
                   jax.ShapeDtypeStruct((m + tm, LANES), jnp.int32),
                   jax.ShapeDtypeStruct((m + tm, LANES), F32)),
        grid=(n_tiles + 1,),
        in_specs=[pl.BlockSpec((tm, d), lambda i: (jnp.minimum(i, last), 0)),
                  pl.BlockSpec((1, d), lambda i: (0, 0)),
                  vec, vec,
                  pl.BlockSpec((d, LANES), lambda i: (0, 0))],
        out_specs=(pl.BlockSpec((tm, d), lambda i: (i, 0)),
                   pl.BlockSpec((tm, LANES), lambda i: (i, 0)),
                   pl.BlockSpec((tm, LANES), lambda i: (i, 0))),
        compiler_params=_cparams(("parallel",)),
        name="router",
    )(x2d, g.reshape(1, d), scale, shift, wr_pad)


def _moe_kernel(be_ref, nv_ref, tok_ref, h_hbm, wg_ref, wu_ref, wd_ref, o_ref, xbuf, xb_scr, sem,
                *, nf, tm, nblk, per_step):
    del be_ref
    i = pl.program_id(0)
    j = pl.program_id(1)
    n_valid = nv_ref[i]
    half = tm // 2
    n_copies = nf * per_step

    def start_row(blk, r):
        tok = tok_ref[blk * tm + jnp.minimum(r, tm - 1)]
        pltpu.make_async_copy(h_hbm.at[pl.ds(tok, 1)], xbuf.at[pl.ds(r, 1)], sem).start()

    def wait_all():
        pltpu.make_async_copy(h_hbm.at[pl.ds(0, n_copies)], xbuf.at[pl.ds(0, n_copies)], sem).wait()

    def start_next():
        nxt = jnp.minimum(i + 1, nblk - 1)
        for u in range(per_step):
            start_row(nxt, j * per_step + u)

    @pl.when(j == 0)
    def _():
        @pl.when(i == 0)
        def _():
            def body(r, carry):
                start_row(0, r)
                return carry
            lax.fori_loop(0, n_copies, body, 0)

        @pl.when(jnp.logical_or(i == 0, nv_ref[jnp.maximum(i - 1, 0)] > 0))
        def _():
            wait_all()
            xb_scr[...] = xbuf[0:tm, :].astype(BF16)

        o_ref[...] = jnp.zeros_like(o_ref)

    def accumulate(rows):
        start_next()
        o_ref[rows, :] += _swiglu_step(xb_scr[rows, :], wg_ref[...].astype(BF16), wu_ref[...].astype(BF16),
                                       wd_ref[...].astype(BF16))

    @pl.when(n_valid > half)
    def _():
        accumulate(slice(0, tm))

    @pl.when(jnp.logical_and(n_valid > 0, n_valid <= half))
    def _():
        accumulate(slice(0, half))

    @pl.when(jnp.logical_and(jnp.logical_and(i == nblk - 1, j == nf - 1), n_valid > 0))
    def _():
        wait_all()


def _moe(slot_tok, h_pad, block_expert, n_valid, wg, wu, wd, tm):
    cap = slot_tok.shape[0]
    d = h_pad.shape[1]
    f = wg.shape[2]
    tf = EXPERT_FF_TILE
    nf = f // tf
    nblk = cap // tm
    per_step = -(-tm // nf)
    while (nf * per_step) % 8:
        per_step += 1
    buf_rows = nf * per_step

    def col_j(i, j, nv):
        return jnp.where(nv[i] > 0, j, nf - 1)

    grid_spec = pltpu.PrefetchScalarGridSpec(
        num_scalar_prefetch=3,
        grid=(nblk, nf),
        in_specs=[pl.BlockSpec(memory_space=pl.ANY),
                  pl.BlockSpec((None, d, tf), lambda i, j, be, nv, tok: (be[i], 0, col_j(i, j, nv))),
                  pl.BlockSpec((None, d, tf), lambda i, j, be, nv, tok: (be[i], 0, col_j(i, j, nv))),
                  pl.BlockSpec((None, tf, d), lambda i, j, be, nv, tok: (be[i], col_j(i, j, nv), 0))],
        out_specs=pl.BlockSpec((tm, d), lambda i, j, be, nv, tok: (i, 0)),
        scratch_shapes=[pltpu.VMEM((buf_rows, d), F32), pltpu.VMEM((tm, d), BF16), pltpu.SemaphoreType.DMA],
    )
    return pl.pallas_call(
        functools.partial(_moe_kernel, nf=nf, tm=tm, nblk=nblk, per_step=per_step),
        out_shape=jax.ShapeDtypeStruct((cap, d), F32),
        grid_spec=grid_spec,
        compiler_params=_cparams(("arbitrary", "arbitrary")),
        name="moe",
    )(block_expert, n_valid, slot_tok, h_pad, wg, wu, wd)


def _combine_kernel(slot_ref, x_ref, gt_ref, rg_ref, fg_ref, y_hbm, o_ref, buf, sem, *, tc, n_steps):
    i = pl.program_id(0)

    def start_tile(t, slot):
        def issue(r, carry):
            for k in range(TOP_K):
                s = slot_ref[TOP_K * (t * tc + r) + k]
                pltpu.make_async_copy(y_hbm.at[pl.ds(s, 1)], buf.at[slot, pl.ds(k * tc + r, 1)],
                                      sem.at[slot]).start()
            return carry
        lax.fori_loop(0, tc, issue, 0, unroll=8)

    @pl.when(i == 0)
    def _():
        start_tile(0, 0)

    @pl.when(i + 1 < n_steps)
    def _():
        start_tile(i + 1, (i + 1) % 2)

    cur = i % 2
    pltpu.make_async_copy(y_hbm.at[pl.ds(0, TOP_K * tc)], buf.at[cur], sem.at[cur]).wait()
    rg = rg_ref[...]
    y = rg[:, 0:1] * buf[cur, 0:tc, :] + rg[:, 1:2] * buf[cur, tc:2 * tc, :]
    xo = x_ref[...] + gt_ref[...] * y
    ms = jnp.mean(xo * xo, axis=-1, keepdims=True)
    o_ref[...] = xo * lax.rsqrt(ms + EPS) * fg_ref[...]


def _combine(slot, x2d, rows_per_batch, gate, route_g, final_g, y, tc):
    m, d = x2d.shape
    tiles_per_batch = rows_per_batch // tc
    grid_spec = pltpu.PrefetchScalarGridSpec(
        num_scalar_prefetch=1,
        grid=(m // tc,),
        in_specs=[pl.BlockSpec((tc, d), lambda i, sl: (i, 0)),
                  pl.BlockSpec((None, 1, d), lambda i, sl: (i // tiles_per_batch, 0, 0)),
                  pl.BlockSpec((tc, LANES), lambda i, sl: (i, 0)),
                  pl.BlockSpec((1, d), lambda i, sl: (0, 0)),
                  pl.BlockSpec(memory_space=pl.ANY)],
        out_specs=pl.BlockSpec((tc, d), lambda i, sl: (i, 0)),
        scratch_shapes=[pltpu.VMEM((2, TOP_K * tc, d), F32), pltpu.SemaphoreType.DMA((2,))],
    )
    return pl.pallas_call(
        functools.partial(_combine_kernel, tc=tc, n_steps=m // tc),
        out_shape=jax.ShapeDtypeStruct((m, d), F32),
        grid_spec=grid_spec,
        compiler_params=_cparams(("arbitrary",)),
        name="combine",
    )(slot, x2d, gate, route_g, final_g.reshape(1, d), y)


def _rope_tables(n_tok, dim):
    t = jnp.arange(n_tok)
    pos = jnp.stack([t // GRID_W, t % GRID_W], axis=-1).astype(F32)
    n_pairs = dim // 4
    freq = ROPE_BASE ** (-jnp.arange(n_pairs, dtype=F32) / n_pairs)
    ang = (pos[:, :, None] * freq).reshape(n_tok, 2 * n_pairs)
    cos = jnp.tile(jnp.cos(ang), (1, HEAD_DIM // dim))
    sin = jnp.tile(jnp.sin(ang), (1, HEAD_DIM // dim))
    return jnp.concatenate([cos, cos], axis=1), jnp.concatenate([-sin, sin], axis=1)


def _deinterleave(w, comps):
    d, n = w.shape
    pairs = HEAD_DIM // (2 * comps)
    return w.reshape(d, n // HEAD_DIM, comps, pairs, 2).transpose(0, 1, 4, 2, 3).reshape(d, n)


def _dedupe(patterns):
    keys, variant = {}, []
    for p in patterns:
        variant.append(keys.setdefault(p.tobytes(), len(keys)))
    first = [variant.index(v) for v in range(len(keys))]
    return np.asarray(variant, np.int32), first


def _window_tables(s):
    tq = BAND_TQ
    nb = s // tq
    band = 3 * tq
    starts = np.clip(np.arange(nb) - 1, 0, nb - 3) * tq
    qpos = np.arange(nb)[:, None] * tq + np.arange(tq)[None, :]
    kpos = starts[:, None] + np.arange(band)[None, :]
    ok = np.abs(kpos[:, None, :] - qpos[:, :, None]) <= C_WINDOW
    variant, first = _dedupe(list(ok))
    bias = np.where(ok[first], 0.0, NEG_INF).astype(np.float32)
    return starts.astype(np.int32), variant, jnp.asarray(bias)


def _na_tables(s, rpb):
    tq = BAND_TQ
    rows = s // GRID_W
    wr = min(NA_ROWS, rows)
    nb = s // tq
    band_blocks = 5
    band = band_blocks * tq
    starts = np.clip(np.arange(nb) - 2, 0, nb - band_blocks) * tq
    n_h = rpb.shape[0]
    n_dc = 2 * NA_COLS - 1
    period = GRID_W + n_dc - 1
    wrap = jnp.zeros((n_h, 2 * NA_ROWS - 1, period), F32)
    wrap = wrap.at[..., :NA_COLS].set(rpb[..., NA_COLS - 1:].astype(F32))
    wrap = wrap.at[..., period - (NA_COLS - 1):].set(rpb[..., :NA_COLS - 1].astype(F32))
    toep = jnp.tile(wrap, (1, 1, GRID_W))[..., :GRID_W * (period - 1)]
    toep = toep.reshape(n_h, 2 * NA_ROWS - 1, GRID_W, period - 1)[..., :GRID_W]
    qc = np.arange(GRID_W)[:, None]
    kc = np.arange(GRID_W)[None, :]
    cstart = np.clip(qc - NA_COLS // 2, 0, GRID_W - NA_COLS)
    col_ok = (kc >= cstart) & (kc < cstart + NA_COLS)
    toep = jnp.where(jnp.asarray(col_ok), toep * LOG2E, NEG_INF)
    masked = jnp.full((n_h, GRID_W, GRID_W), NEG_INF, F32)
    pats = []
    for j in range(nb):
        pat = np.full((tq // GRID_W, band // GRID_W), -1, np.int32)
        for a in range(tq // GRID_W):
            qr = j * (tq // GRID_W) + a
            rs = min(max(qr - wr // 2, 0), rows - wr)
            for w in range(band // GRID_W):
                kr = starts[j] // GRID_W + w
                if rs <= kr < rs + wr:
                    pat[a, w] = kr - qr + NA_ROWS - 1
        pats.append(pat)
    variant, first = _dedupe(pats)
    tiles = []
    for j in first:
        rows_ = [jnp.concatenate([toep[:, dr] if dr >= 0 else masked for dr in pats[j][a]], axis=-1)
                 for a in range(tq // GRID_W)]
        tiles.append(jnp.concatenate(rows_, axis=-2))
    bias = jnp.stack(tiles, axis=1)
    bias = bias.reshape(n_h // 4, 4, len(first), tq, band).transpose(0, 2, 1, 3, 4)
    return starts.astype(np.int32), variant, bias.reshape(n_h // 4, len(first), 4 * tq, band)


def kernel(x, c, ctx, c_ctx, l0_norm1_g, l0_norm2_g, l0_w_ada, l0_b_ada, l0_w_in, l0_w_out, l0_lam_q1, l0_lam_k1, l0_lam_q2, l0_lam_k2, l0_subln_g, l0_q_norm_g, l0_k_norm_g, l0_ffn_w_gate, l0_ffn_w_up, l0_ffn_w_down, l1_norm1_g, l1_norm2_g, l1_w_ada, l1_b_ada, l1_w_in, l1_w_out, l1_sinks, l1_rpb, l1_router, l1_exp_w_gate, l1_exp_w_up, l1_exp_w_down, final_norm_g):
    b, s, d = x.shape
    n_ctx = ctx.shape[1]
    n_exp = l1_router.shape[1]
    n = b * s
    assert d == 2048 and s % ROW_TILE == 0 and n_ctx == 256 and s // GRID_W >= 10

    x2d = x.reshape(n, d)
    xc2d = ctx.reshape(b * n_ctx, d)

    cvecs = jnp.zeros((8, d), F32).at[:b].set(c).at[b].set(c_ctx)
    mod0 = _ada(cvecs, l0_w_ada, l0_b_ada)
    mod1 = _ada(cvecs, l1_w_ada, l1_b_ada)

    def lat(mod, k):
        return mod[:b, k * d:(k + 1) * d].reshape(b, 1, d)

    def cx(mod, k):
        return jnp.broadcast_to(mod[b, k * d:(k + 1) * d].reshape(1, 1, d), (b, 1, d))

    cos_a, sin_a = _rope_tables(s, A_SUB)
    cos_b, sin_b = _rope_tables(s, HEAD_DIM)
    cos_tab = jnp.stack([cos_a, cos_b])
    sin_tab = jnp.stack([sin_a, sin_b])
    tm = ROW_TILE
    hq = 8 * HEAD_DIM

    qa = T_ROPE | T_SCALE_A
    qb = T_ROPE | T_QNORM | T_SCALE_B
    kb = T_ROPE | T_KNORM
    types0 = [qa] * 8 + [qb] * 8 + [T_ROPE] * 8 + [0] * 8 + [kb] * 2 + [0] * 2
    tabs0 = [0] * 8 + [1] * 8 + [0] * 8 + [0] * 8 + [1] * 2 + [0] * 2
    qkg0 = _deinterleave(jnp.stack([l0_q_norm_g, l0_k_norm_g]).astype(F32), 1)
    w0 = l0_w_in.astype(BF16)
    w_in0 = jnp.concatenate([_deinterleave(w0[:, :hq], 2), _deinterleave(w0[:, hq:2 * hq], 1),
                             _deinterleave(w0[:, 2 * hq:3 * hq], 2), w0[:, 3 * hq:4 * hq],
                             _deinterleave(w0[:, 4 * hq:4 * hq + 2 * HEAD_DIM], 1),
                             w0[:, 4 * hq + 2 * HEAD_DIM:]], axis=1)
    p0 = _inproj(x2d, s, l0_norm1_g, lat(mod0, 1), lat(mod0, 0), w_in0, cos_tab, sin_tab, qkg0,
                 types0, tabs0, tm)
    rc = b * n_ctx
    tmc = tm if rc % tm == 0 else n_ctx
    p0c = _inproj(xc2d, rc, l0_norm1_g, cx(mod0, 1), cx(mod0, 0), w_in0, cos_tab, sin_tab, qkg0,
                  [t & ~T_ROPE for t in types0], tabs0, tmc)
    lam = jnp.stack([l0_lam_q1, l0_lam_k1, l0_lam_q2, l0_lam_k2]).astype(F32)
    subg = l0_subln_g.reshape(1, HEAD_DIM).astype(F32)
    tk = min(FLASH_TK, s)
    oa = _flash("diff", p0, p0, p0c, b, FLASH_TQ_DIFF, tk, 0, 16, 24, lam, subg)
    ob = _flash("gqa", p0, p0, p0c, b, FLASH_TQ_GQA, tk, 8, 32, 34)
    oac = _flash("diff", p0c, p0c, None, b, n_ctx, n_ctx, 0, 16, 24, lam, subg)
    obc = _flash("gqa", p0c, p0c, None, b, n_ctx, n_ctx, 8, 32, 34)
    w_out0 = l0_w_out.astype(BF16)
    x1 = _outproj(oa, ob, w_out0, x2d, lat(mod0, 2), s, tm)
    xc1 = _outproj(oac, obc, w_out0, xc2d, cx(mod0, 2), rc, tmc)
    wg0, wu0, wd0 = l0_ffn_w_gate.astype(BF16), l0_ffn_w_up.astype(BF16), l0_ffn_w_down.astype(BF16)
    x2 = _ffn(x1, s, l0_norm2_g, lat(mod0, 4), lat(mod0, 3), lat(mod0, 5), wg0, wu0, wd0, tm)
    xc2 = _ffn(xc1, rc, l0_norm2_g, cx(mod0, 4), cx(mod0, 3), cx(mod0, 5), wg0, wu0, wd0, tmc)

    types1 = [T_ROPE | T_SCALE_B] * 8 + [T_SCALE_B] * 8 + [T_ROPE] * 2 + [0] * 2 + [0] * 16
    tabs1 = [1] * len(types1)
    w1 = l1_w_in.astype(BF16)
    w_in1 = jnp.concatenate([_deinterleave(w1[:, :hq], 1), w1[:, hq:2 * hq],
                             _deinterleave(w1[:, 2 * hq:2 * hq + 2 * HEAD_DIM], 1),
                             w1[:, 2 * hq + 2 * HEAD_DIM:]], axis=1)
    qkg1 = jnp.ones((2, HEAD_DIM), F32)
    p1 = _inproj(x2, s, l1_norm1_g, lat(mod1, 1), lat(mod1, 0), w_in1, cos_tab, sin_tab, qkg1,
                 types1, tabs1, tm)
    l1_q = 16 * HEAD_DIM
    n_kv_heads = (w_in1.shape[1] - l1_q) // HEAD_DIM
    p1c = _inproj(xc2, rc, l1_norm1_g, cx(mod1, 1), cx(mod1, 0), w_in1[:, l1_q:], cos_tab, sin_tab, qkg1,
                  [0] * n_kv_heads, [0] * n_kv_heads, tmc)
    w_st, w_var, w_bias = _window_tables(s)
    n_st, n_var, n_bias = _na_tables(s, l1_rpb)
    sinks = l1_sinks.astype(F32)
    oc = _banded("window", p1, p1c, b, jnp.asarray(w_st), jnp.asarray(w_var), w_bias, sinks, 0, 16, 18, 0, 2)
    od = _banded("na", p1, p1c, b, jnp.asarray(n_st), jnp.asarray(n_var), n_bias, sinks, 8, 20, 28, 4, 12)
    x3 = _outproj(oc, od, l1_w_out.astype(BF16), x2, lat(mod1, 2), s, tm)

    wr_pad = jnp.zeros((d, LANES), F32).at[:, :n_exp].set(l1_router.astype(F32))
    h2, route_i, route_g = _router(x3, s, l1_norm2_g, lat(mod1, 4), lat(mod1, 3), wr_pad, n_exp, tm)
    route_g = route_g[:n]
    e_flat = route_i[:n, :TOP_K].reshape(-1)
    onehot = (e_flat[:, None] == jnp.arange(n_exp, dtype=jnp.int32)[None, :]).astype(jnp.int32)
    csum = jnp.cumsum(onehot, axis=0)
    rank = jnp.sum((csum - onehot) * onehot, axis=1)
    counts = csum[-1]
    tmo = 2 * tm
    padded = (counts + tmo - 1) // tmo * tmo
    pad_end = jnp.cumsum(padded)
    pad_start = pad_end - padded
    slot = (pad_start[e_flat] + rank).astype(jnp.int32)
    n_blocks = (n * TOP_K + n_exp * (tmo - 1) + tmo - 1) // tmo
    cap = n_blocks * tmo
    tok_flat = jnp.repeat(jnp.arange(n, dtype=jnp.int32), TOP_K)
    slot_tok = jnp.full((cap,), n, jnp.int32).at[slot].set(tok_flat)
    block_row0 = jnp.arange(n_blocks, dtype=jnp.int32) * tmo
    block_expert = jnp.minimum(jnp.searchsorted(pad_end, block_row0, side="right"), n_exp - 1).astype(jnp.int32)
    n_valid = jnp.clip((pad_start + counts)[block_expert] - block_row0, 0, tmo).astype(jnp.int32)

    y = _moe(slot_tok, h2, block_expert, n_valid, l1_exp_w_gate, l1_exp_w_up, l1_exp_w_down, tmo)
    out = _combine(slot, x3, s, lat(mod1, 5), route_g, final_norm_g, y, COMBINE_TILE)
    return out.reshape(b, s, d)
```

```python
import functools
import math

import numpy as np
import jax
import jax.numpy as jnp
from jax import lax
from jax.experimental import pallas as pl
from jax.experimental.pallas import tpu as pltpu

F32 = jnp.float32
BF16 = jnp.bfloat16

HEAD_DIM = 128
LANES = 128
GRID_W = 64
ROPE_BASE = 10000.0
EPS = 1e-6
NEG_INF = -1e30
A_SUB = HEAD_DIM // 2
A_SCALE = A_SUB ** -0.5
ATT_SCALE = HEAD_DIM ** -0.5
LAMBDA_INIT_L0 = 0.8 - 0.6 * math.exp(-0.3 * 0)
LOG2E = 1.4426950408889634
C_WINDOW = 128
NA_ROWS = 8
NA_COLS = 16
TOP_K = 2
VMEM_LIMIT = 56 * 1024 * 1024
ROW_TILE = 512
FF_TILE = 512
EXPERT_FF_TILE = 256
ADA_COL_TILE = 1024
FLASH_TQ_DIFF, FLASH_TQ_GQA = 1024, 512
FLASH_TK = 2048
BAND_TQ = 128
COMBINE_TILE = 512
T_ROPE, T_QNORM, T_KNORM, T_SCALE_B, T_SCALE_A = 1, 2, 4, 8, 16


def _cparams(sem):
    return pltpu.CompilerParams(dimension_semantics=sem, vmem_limit_bytes=VMEM_LIMIT)


def _modulate(xf, g, scale, shift):
    ms = jnp.mean(xf * xf, axis=-1, keepdims=True)
    y = xf * lax.rsqrt(ms + EPS) * g
    return y * (1.0 + scale) + shift


def _ada_kernel(c_ref, w_ref, b_ref, o_ref):
    c = c_ref[...]
    a = c * (1.0 / (1.0 + jnp.exp(-c)))
    o_ref[...] = jnp.dot(a.astype(BF16), w_ref[...].astype(BF16), preferred_element_type=F32) + b_ref[...]


def _ada(cvecs, w_ada, b_ada):
    r, d = cvecs.shape
    n = w_ada.shape[1]
    tn = ADA_COL_TILE
    return pl.pallas_call(
        _ada_kernel,
        out_shape=jax.ShapeDtypeStruct((r, n), F32),
        grid=(n // tn,),
        in_specs=[pl.BlockSpec((r, d), lambda j: (0, 0)),
                  pl.BlockSpec((d, tn), lambda j: (0, j)),
                  pl.BlockSpec((1, tn), lambda j: (0, j))],
        out_specs=pl.BlockSpec((r, tn), lambda j: (0, j)),
        compiler_params=_cparams(("parallel",)),
        name="ada",
    )(cvecs, w_ada, b_ada.reshape(1, n))


def _inproj_kernel(x_ref, g_ref, sc_ref, sh_ref, w_ref, cos_ref, sin_ref, qkg_ref, o_ref, h_scr, *, types, tabs):
    h_scr[...] = _modulate(x_ref[...], g_ref[...], sc_ref[...], sh_ref[...]).astype(BF16)
    chunk = 4 * HEAD_DIM
    for c in range(len(types) // 4):
        acc = jnp.dot(h_scr[...], w_ref[:, c * chunk:(c + 1) * chunk], preferred_element_type=F32)
        for hh in range(4):
            t = types[4 * c + hh]
            y = acc[:, hh * HEAD_DIM:(hh + 1) * HEAD_DIM]
            if t & (T_QNORM | T_KNORM):
                ms = jnp.mean(y * y, axis=-1, keepdims=True)
                row = 0 if t & T_QNORM else 1
                y = y * lax.rsqrt(ms + EPS) * qkg_ref[row:row + 1, :]
            if t & T_ROPE:
                tb = tabs[4 * c + hh]
                y = y * cos_ref[tb] + pltpu.roll(y, HEAD_DIM // 2, 1) * sin_ref[tb]
            if t & T_SCALE_B:
                y = y * (ATT_SCALE * LOG2E)
            elif t & T_SCALE_A:
                y = y * (A_SCALE * LOG2E)
            col = (4 * c + hh) * HEAD_DIM
            o_ref[:, col:col + HEAD_DIM] = y.astype(o_ref.dtype)


def _inproj(x2d, rows_per_batch, g, scale, shift, w_bf, cos_tab, sin_tab, qkg, types, tabs, tm):
    m, d = x2d.shape
    n = w_bf.shape[1]
    tiles_per_batch = rows_per_batch // tm
    tab_rows = cos_tab.shape[1] // tm
    kern = functools.partial(_inproj_kernel, types=tuple(int(t) for t in types), tabs=tuple(int(t) for t in tabs))
    return pl.pallas_call(
        kern,
        out_shape=jax.ShapeDtypeStruct((m, n), BF16),
        grid=(m // tm,),
        in_specs=[
            pl.BlockSpec((tm, d), lambda i: (i, 0)),
            pl.BlockSpec((1, d), lambda i: (0, 0)),
            pl.BlockSpec((None, 1, d), lambda i: (i // tiles_per_batch, 0, 0)),
            pl.BlockSpec((None, 1, d), lambda i: (i // tiles_per_batch, 0, 0)),
            pl.BlockSpec((d, n), lambda i: (0, 0), pipeline_mode=pl.Buffered(1)),
            pl.BlockSpec((2, tm, LANES), lambda i: (0, i % tab_rows, 0)),
            pl.BlockSpec((2, tm, LANES), lambda i: (0, i % tab_rows, 0)),
            pl.BlockSpec((2, HEAD_DIM), lambda i: (0, 0)),
        ],
        out_specs=pl.BlockSpec((tm, n), lambda i: (i, 0)),
        scratch_shapes=[pltpu.VMEM((tm, d), BF16)],
        compiler_params=_cparams(("parallel",)),
        name="inproj",
    )(x2d, g.reshape(1, d), scale, shift, w_bf, cos_tab, sin_tab, qkg)


def _flash_kernel(*refs, mode, tq, has_ctx, tk):
    refs = list(refs)
    q_ref, k_ref, v_ref = refs[:3]
    pos = 3
    if has_ctx:
        kc_ref, vc_ref = refs[3:5]
        pos = 5
    if mode == "diff":
        lam_ref, subg_ref = refs[pos:pos + 2]
        pos += 2
    o_ref, qs = refs[pos:pos + 2]

    q = q_ref[...]
    if mode == "diff":
        lane = lax.broadcasted_iota(jnp.int32, (1, LANES), 1)
        zero = jnp.zeros_like(q)
        first = (lane & (A_SUB // 2)) == 0
        qs[0:tq, :] = jnp.where(first, q, zero)
        qs[tq:2 * tq, :] = jnp.where(first, zero, q)
    else:
        for gi in range(4):
            qs[gi * tq:(gi + 1) * tq, :] = q[:, gi * HEAD_DIM:(gi + 1) * HEAD_DIM]

    def step(carry, k, v):
        s = lax.dot_general(qs[...], k, (((1,), (1,)), ((), ())), preferred_element_type=F32)
        slabs = [s[:, c * LANES:(c + 1) * LANES] for c in range(k.shape[0] // LANES)]
        mx = functools.reduce(jnp.maximum, slabs)
        row_max = jnp.broadcast_to(jnp.max(mx, axis=-1, keepdims=True), mx.shape)
        m_new = row_max if carry is None else jnp.maximum(carry[0], row_max)
        ps = [jnp.exp2(sl - m_new) for sl in slabs]
        p = jnp.concatenate([x.astype(BF16) for x in ps], axis=1)
        l_new = functools.reduce(jnp.add, ps)
        acc_new = jnp.dot(p, v, preferred_element_type=F32)
        if carry is not None:
            alpha = jnp.exp2(carry[0] - m_new)
            l_new = alpha * carry[1] + l_new
            acc_new = alpha * carry[2] + acc_new
        return m_new, l_new, acc_new

    carry = None
    for c in range(k_ref.shape[0] // tk):
        carry = step(carry, k_ref[c * tk:(c + 1) * tk, :], v_ref[c * tk:(c + 1) * tk, :])
    if has_ctx:
        carry = step(carry, kc_ref[...], vc_ref[...])
    _, l_fin, acc_fin = carry
    o = acc_fin / jnp.sum(l_fin, axis=-1, keepdims=True)
    if mode == "diff":
        lam1 = jnp.sum(lam_ref[0:1, :] * lam_ref[1:2, :], axis=-1, keepdims=True)
        lam2 = jnp.sum(lam_ref[2:3, :] * lam_ref[3:4, :], axis=-1, keepdims=True)
        lam = jnp.exp(lam1) - jnp.exp(lam2) + LAMBDA_INIT_L0
        od = o[0:tq, :] - lam * o[tq:2 * tq, :]
        ms = jnp.mean(od * od, axis=-1, keepdims=True)
        on = od * lax.rsqrt(ms + EPS) * subg_ref[...]
        o_ref[...] = (on * (1.0 - LAMBDA_INIT_L0)).astype(o_ref.dtype)
    else:
        for gi in range(4):
            o_ref[:, gi * HEAD_DIM:(gi + 1) * HEAD_DIM] = o[gi * tq:(gi + 1) * tq, :].astype(o_ref.dtype)


def _flash(mode, q_arr, kv_arr, ctx_arr, batch, tq, tk, qcol, kcol, vcol, lam=None, subg=None):
    mq = q_arr.shape[0]
    sq = mq // batch
    sk = kv_arr.shape[0] // batch
    nq = sq // tq
    has_ctx = ctx_arr is not None
    if mode == "diff":
        nh, qw, reps = 8, HEAD_DIM, 2
        q_spec = pl.BlockSpec((tq, qw), lambda b, h, qi: (b * nq + qi, qcol + h))
        o_spec = pl.BlockSpec((tq, qw), lambda b, h, qi: (b * nq + qi, h))
    else:
        nh, qw, reps = 2, 4 * HEAD_DIM, 4
        q_spec = pl.BlockSpec((tq, qw), lambda b, h, qi: (b * nq + qi, qcol // 4 + h))
        o_spec = pl.BlockSpec((tq, qw), lambda b, h, qi: (b * nq + qi, h))
    in_specs = [q_spec,
                pl.BlockSpec((sk, HEAD_DIM), lambda b, h, qi: (b, kcol + h)),
                pl.BlockSpec((sk, HEAD_DIM), lambda b, h, qi: (b, vcol + h))]
    args = [q_arr, kv_arr, kv_arr]
    if has_ctx:
        nc = ctx_arr.shape[0] // batch
        in_specs += [pl.BlockSpec((nc, HEAD_DIM), lambda b, h, qi: (b, kcol + h)),
                     pl.BlockSpec((nc, HEAD_DIM), lambda b, h, qi: (b, vcol + h))]
        args += [ctx_arr, ctx_arr]
    if mode == "diff":
        in_specs += [pl.BlockSpec((4, A_SUB), lambda b, h, qi: (0, 0)),
                     pl.BlockSpec((1, HEAD_DIM), lambda b, h, qi: (0, 0))]
        args += [lam, subg]
    kern = functools.partial(_flash_kernel, mode=mode, tq=tq, has_ctx=has_ctx, tk=tk)
    return pl.pallas_call(
        kern,
        out_shape=jax.ShapeDtypeStruct((mq, 8 * HEAD_DIM), BF16),
        grid=(batch, nh, nq),
        in_specs=in_specs,
        out_specs=o_spec,
        scratch_shapes=[pltpu.VMEM((reps * tq, HEAD_DIM), BF16)],
        compiler_params=_cparams(("parallel", "parallel", "parallel")),
        name="flash_" + mode,
    )(*args)


def _banded_kernel(start_ref, var_ref, q_ref, k_ref, v_ref, kc_ref, vc_ref, bias_ref, sink_ref, o_ref,
                   *, mode, tq, band):
    del var_ref
    hg = pl.program_id(1)
    j = pl.program_id(2)
    start = pl.multiple_of(start_ref[j], LANES)
    kb = k_ref[pl.ds(start, band), :]
    vb = v_ref[pl.ds(start, band), :]
    kc = kc_ref[...]
    vc = vc_ref[...]
    q = q_ref[...]
    dn = (((1,), (1,)), ((), ()))
    heads = [slice(gi * HEAD_DIM, (gi + 1) * HEAD_DIM) for gi in range(4)]
    if mode == "window":
        qs = jnp.concatenate([q[:, hd] for hd in heads], axis=0)
        s_loc = lax.dot_general(qs, kb, dn, preferred_element_type=F32)
        s_ctx = lax.dot_general(qs, kc, dn, preferred_element_type=F32)
        bias = jnp.concatenate([bias_ref[...]] * 4, axis=0)
    else:
        s_loc = jnp.concatenate([lax.dot_general(q[:, hd], kb[:, hd], dn, preferred_element_type=F32)
                                 for hd in heads], axis=0)
        s_ctx = jnp.concatenate([lax.dot_general(q[:, hd], kc[:, hd], dn, preferred_element_type=F32)
                                 for hd in heads], axis=0)
        bias = bias_ref[...]
    s_loc = s_loc + bias
    n_loc = band // LANES
    slabs = ([s_loc[:, c * LANES:(c + 1) * LANES] for c in range(n_loc)]
             + [s_ctx[:, c * LANES:(c + 1) * LANES] for c in range(s_ctx.shape[1] // LANES)])
    mx = functools.reduce(jnp.maximum, slabs)
    m = jnp.broadcast_to(jnp.max(mx, axis=-1, keepdims=True), mx.shape)
    if mode == "window":
        sink = jnp.concatenate([jnp.full((tq, LANES), sink_ref[hg * 4 + gi] * LOG2E, F32) for gi in range(4)], axis=0)
        m = jnp.maximum(m, sink)
    ps = [jnp.exp2(sl - m) for sl in slabs]
    l = jnp.sum(functools.reduce(jnp.add, ps), axis=-1, keepdims=True)
    if mode == "window":
        l = l + jnp.exp2(sink - m)[:, 0:1]
    p_loc = jnp.concatenate([x.astype(BF16) for x in ps[:n_loc]], axis=1)
    p_ctx = jnp.concatenate([x.astype(BF16) for x in ps[n_loc:]], axis=1)
    if mode == "window":
        o = jnp.dot(p_loc, vb, preferred_element_type=F32) + jnp.dot(p_ctx, vc, preferred_element_type=F32)
        o = (o / l).astype(o_ref.dtype)
        for gi in range(4):
            o_ref[:, heads[gi]] = o[gi * tq:(gi + 1) * tq, :]
    else:
        for gi in range(4):
            rows = slice(gi * tq, (gi + 1) * tq)
            o = (jnp.dot(p_loc[rows, :], vb[:, heads[gi]], preferred_element_type=F32)
                 + jnp.dot(p_ctx[rows, :], vc[:, heads[gi]], preferred_element_type=F32))
            o_ref[:, heads[gi]] = (o / l[rows, :]).astype(o_ref.dtype)


def _banded(mode, proj, proj_ctx, batch, starts, variants, bias, sinks, qcol, kcol, vcol, kc_col, vc_col):
    m = proj.shape[0]
    s = m // batch
    tq = BAND_TQ
    nq = s // tq
    nc = proj_ctx.shape[0] // batch
    band = bias.shape[-1]
    qw = 4 * HEAD_DIM
    if mode == "window":
        kw, div = HEAD_DIM, 1
        bias_spec = pl.BlockSpec((None, tq, band), lambda b, h, j, st, va: (va[j], 0, 0))
    else:
        kw, div = qw, 4
        bias_spec = pl.BlockSpec((None, None, 4 * tq, band), lambda b, h, j, st, va: (h, va[j], 0, 0))
    grid_spec = pltpu.PrefetchScalarGridSpec(
        num_scalar_prefetch=2,
        grid=(batch, 2, nq),
        in_specs=[
            pl.BlockSpec((tq, qw), lambda b, h, j, st, va: (b * nq + j, qcol // 4 + h)),
            pl.BlockSpec((s, kw), lambda b, h, j, st, va: (b, kcol // div + h)),
            pl.BlockSpec((s, kw), lambda b, h, j, st, va: (b, vcol // div + h)),
            pl.BlockSpec((nc, kw), lambda b, h, j, st, va: (b, kc_col // div + h)),
            pl.BlockSpec((nc, kw), lambda b, h, j, st, va: (b, vc_col // div + h)),
            bias_spec,
            pl.BlockSpec(memory_space=pltpu.SMEM),
        ],
        out_specs=pl.BlockSpec((tq, qw), lambda b, h, j, st, va: (b * nq + j, h)),
    )
    kern = functools.partial(_banded_kernel, mode=mode, tq=tq, band=band)
    return pl.pallas_call(
        kern,
        out_shape=jax.ShapeDtypeStruct((m, 8 * HEAD_DIM), BF16),
        grid_spec=grid_spec,
        compiler_params=_cparams(("parallel", "parallel", "arbitrary")),
        name="banded_" + mode,
    )(starts, variants, proj, proj, proj, proj_ctx, proj_ctx, bias, sinks)


def _outproj_kernel(oa_ref, ob_ref, wa_ref, wb_ref, x_ref, gt_ref, o_ref):
    acc = jnp.dot(oa_ref[...], wa_ref[...], preferred_element_type=F32)
    acc = acc + jnp.dot(ob_ref[...], wb_ref[...], preferred_element_type=F32)
    o_ref[...] = x_ref[...] + gt_ref[...] * acc


def _outproj(oa, ob, w_bf, x2d, gate, rows_per_batch, tm):
    m, d = x2d.shape
    half = oa.shape[1]
    tiles_per_batch = rows_per_batch // tm
    return pl.pallas_call(
        _outproj_kernel,
        out_shape=jax.ShapeDtypeStruct((m, d), F32),
        grid=(m // tm,),
        in_specs=[pl.BlockSpec((tm, half), lambda i: (i, 0)),
                  pl.BlockSpec((tm, half), lambda i: (i, 0)),
                  pl.BlockSpec((half, d), lambda i: (0, 0)),
                  pl.BlockSpec((half, d), lambda i: (1, 0)),
                  pl.BlockSpec((tm, d), lambda i: (i, 0)),
                  pl.BlockSpec((None, 1, d), lambda i: (i // tiles_per_batch, 0, 0))],
        out_specs=pl.BlockSpec((tm, d), lambda i: (i, 0)),
        compiler_params=_cparams(("parallel",)),
        name="outproj",
    )(oa, ob, w_bf, w_bf, x2d, gate)


def _swiglu_step(h, wg, wu, wd):
    g = jnp.dot(h, wg, preferred_element_type=F32)
    u = jnp.dot(h, wu, preferred_element_type=F32)
    a = (g * (1.0 / (1.0 + jnp.exp(-g))) * u).astype(BF16)
    return jnp.dot(a, wd, preferred_element_type=F32)


def _ffn_kernel(x_ref, g_ref, sc_ref, sh_ref, gt_ref, wg_ref, wu_ref, wd_ref, o_ref, h_scr, acc_scr, *, nf):
    j = pl.program_id(1)

    @pl.when(j == 0)
    def _():
        h_scr[...] = _modulate(x_ref[...], g_ref[...], sc_ref[...], sh_ref[...]).astype(BF16)
        acc_scr[...] = jnp.zeros_like(acc_scr)

    acc_scr[...] += _swiglu_step(h_scr[...], wg_ref[...], wu_ref[...], wd_ref[...])

    @pl.when(j == nf - 1)
    def _():
        o_ref[...] = x_ref[...] + gt_ref[...] * acc_scr[...]


def _ffn(x2d, rows_per_batch, g, scale, shift, gate, wg, wu, wd, tm):
    m, d = x2d.shape
    f = wg.shape[1]
    tf = FF_TILE
    nf = f // tf
    tiles_per_batch = rows_per_batch // tm
    vec = pl.BlockSpec((None, 1, d), lambda i, j: (i // tiles_per_batch, 0, 0))
    return pl.pallas_call(
        functools.partial(_ffn_kernel, nf=nf),
        out_shape=jax.ShapeDtypeStruct((m, d), F32),
        grid=(m // tm, nf),
        in_specs=[pl.BlockSpec((tm, d), lambda i, j: (i, 0)),
                  pl.BlockSpec((1, d), lambda i, j: (0, 0)),
                  vec, vec, vec,
                  pl.BlockSpec((d, tf), lambda i, j: (0, j)),
                  pl.BlockSpec((d, tf), lambda i, j: (0, j)),
                  pl.BlockSpec((tf, d), lambda i, j: (j, 0))],
        out_specs=pl.BlockSpec((tm, d), lambda i, j: (i, 0)),
        scratch_shapes=[pltpu.VMEM((tm, d), BF16), pltpu.VMEM((tm, d), F32)],
        compiler_params=_cparams(("parallel", "arbitrary")),
        name="ffn",
    )(x2d, g.reshape(1, d), scale, shift, gate, wg, wu, wd)


def _router_kernel(x_ref, g_ref, sc_ref, sh_ref, wr_ref, h_ref, ri_ref, rg_ref, *, n_exp, n_tiles):
    h = _modulate(x_ref[...], g_ref[...], sc_ref[...], sh_ref[...])
    h_ref[...] = jnp.where(pl.program_id(0) < n_tiles, h, 0.0)
    logits = jnp.dot(h, wr_ref[...], preferred_element_type=F32, precision=lax.Precision.HIGHEST)
    lane = lax.broadcasted_iota(jnp.int32, logits.shape, 1)
    logits = jnp.where(lane < n_exp, logits, -jnp.inf)
    lane_f = lane.astype(F32)
    m1 = jnp.max(logits, axis=-1, keepdims=True)
    i1 = jnp.min(jnp.where(logits == m1, lane_f, float(LANES)), axis=-1, keepdims=True)
    rest = jnp.where(lane_f == i1, -jnp.inf, logits)
    m2 = jnp.max(rest, axis=-1, keepdims=True)
    i2 = jnp.min(jnp.where(rest == m2, lane_f, float(LANES)), axis=-1, keepdims=True)
    e = jnp.exp(m2 - m1)
    g1 = 1.0 / (1.0 + e)
    g2 = e / (1.0 + e)
    ri_ref[...] = jnp.where(lane == 0, i1, jnp.where(lane == 1, i2, 0.0)).astype(jnp.int32)
    rg_ref[...] = jnp.where(lane == 0, g1, jnp.where(lane == 1, g2, 0.0))


def _router(x2d, rows_per_batch, g, scale, shift, wr_pad, n_exp, tm):
    m, d = x2d.shape
    tiles_per_batch = rows_per_batch // tm
    n_tiles = m // tm
    last = n_tiles - 1
    vec = pl.BlockSpec((None, 1, d), lambda i: (jnp.minimum(i, last) // tiles_per_batch, 0, 0))
    return pl.pallas_call(
        functools.partial(_router_kernel, n_exp=n_exp, n_tiles=n_tiles),
        out_shape=(jax.ShapeDtypeStruct((m + tm, d), F32),
                   jax.ShapeDtypeStruct((m + tm, LANES), jnp.int32),
                   jax.ShapeDtypeStruct((m + tm, LANES), F32)),
        grid=(n_tiles + 1,),
        in_specs=[pl.BlockSpec((tm, d), lambda i: (jnp.minimum(i, last), 0)),
                  pl.BlockSpec((1, d), lambda i: (0, 0)),
                  vec, vec,
                  pl.BlockSpec((d, LANES), lambda i: (0, 0))],
        out_specs=(pl.BlockSpec((tm, d), lambda i: (i, 0)),
                   pl.BlockSpec((tm, LANES), lambda i: (i, 0)),
                   pl.BlockSpec((tm, LANES), lambda i: (i, 0))),
        compiler_params=_cparams(("parallel",)),
        name="router",
    )(x2d, g.reshape(1, d), scale, shift, wr_pad)


def _moe_kernel(be_ref, nv_ref, tok_ref, h_hbm, wg_ref, wu_ref, wd_ref, o_ref, xbuf, xb_scr, sem,
                *, nf, tm, nblk, per_step):
    del be_ref
    i = pl.program_id(0)
    j = pl.program_id(1)
    n_valid = nv_ref[i]
    half = tm // 2
    n_copies = nf * per_step

    def start_row(blk, r):
        tok = tok_ref[blk * tm + jnp.minimum(r, tm - 1)]
        pltpu.make_async_copy(h_hbm.at[pl.ds(tok, 1)], xbuf.at[pl.ds(r, 1)], sem).start()

    def wait_all():
        pltpu.make_async_copy(h_hbm.at[pl.ds(0, n_copies)], xbuf.at[pl.ds(0, n_copies)], sem).wait()

    def start_next():
        nxt = jnp.minimum(i + 1, nblk - 1)
        for u in range(per_step):
            start_row(nxt, j * per_step + u)

    @pl.when(j == 0)
    def _():
        @pl.when(i == 0)
        def _():
            def body(r, carry):
                start_row(0, r)
                return carry
            lax.fori_loop(0, n_copies, body, 0)

        @pl.when(jnp.logical_or(i == 0, nv_ref[jnp.maximum(i - 1, 0)] > 0))
        def _():
            wait_all()
            xb_scr[...] = xbuf[0:tm, :].astype(BF16)

        o_ref[...] = jnp.zeros_like(o_ref)

    def accumulate(rows):
        start_next()
        o_ref[rows, :] += _swiglu_step(xb_scr[rows, :], wg_ref[...].astype(BF16), wu_ref[...].astype(BF16),
                                       wd_ref[...].astype(BF16))

    @pl.when(n_valid > half)
    def _():
        accumulate(slice(0, tm))

    @pl.when(jnp.logical_and(n_valid > 0, n_valid <= half))
    def _():
        accumulate(slice(0, half))

    @pl.when(jnp.logical_and(jnp.logical_and(i == nblk - 1, j == nf - 1), n_valid > 0))
    def _():
        wait_all()


def _moe(slot_tok, h_pad, block_expert, n_valid, wg, wu, wd, tm):
    cap = slot_tok.shape[0]
    d = h_pad.shape[1]
    f = wg.shape[2]
    tf = EXPERT_FF_TILE
    nf = f // tf
    nblk = cap // tm
    per_step = -(-tm // nf)
    while (nf * per_step) % 8:
        per_step += 1
    buf_rows = nf * per_step

    def col_j(i, j, nv):
        return jnp.where(nv[i] > 0, j, nf - 1)

    grid_spec = pltpu.PrefetchScalarGridSpec(
        num_scalar_prefetch=3,
        grid=(nblk, nf),
        in_specs=[pl.BlockSpec(memory_space=pl.ANY),
                  pl.BlockSpec((None, d, tf), lambda i, j, be, nv, tok: (be[i], 0, col_j(i, j, nv))),
                  pl.BlockSpec((None, d, tf), lambda i, j, be, nv, tok: (be[i], 0, col_j(i, j, nv))),
                  pl.BlockSpec((None, tf, d), lambda i, j, be, nv, tok: (be[i], col_j(i, j, nv), 0))],
        out_specs=pl.BlockSpec((tm, d), lambda i, j, be, nv, tok: (i, 0)),
        scratch_shapes=[pltpu.VMEM((buf_rows, d), F32), pltpu.VMEM((tm, d), BF16), pltpu.SemaphoreType.DMA],
    )
    return pl.pallas_call(
        functools.partial(_moe_kernel, nf=nf, tm=tm, nblk=nblk, per_step=per_step),
        out_shape=jax.ShapeDtypeStruct((cap, d), F32),
        grid_spec=grid_spec,
        compiler_params=_cparams(("arbitrary", "arbitrary")),
        name="moe",
    )(block_expert, n_valid, slot_tok, h_pad, wg, wu, wd)


def _combine_kernel(slot_ref, x_ref, gt_ref, rg_ref, fg_ref, y_hbm, o_ref, buf, sem, *, tc, n_steps):
    i = pl.program_id(0)

    def start_tile(t, slot):
        def issue(r, carry):
            for k in range(TOP_K):
                s = slot_ref[TOP_K * (t * tc + r) + k]
                pltpu.make_async_copy(y_hbm.at[pl.ds(s, 1)], buf.at[slot, pl.ds(k * tc + r, 1)],
                                      sem.at[slot]).start()
            return carry
        lax.fori_loop(0, tc, issue, 0, unroll=8)

    @pl.when(i == 0)
    def _():
        start_tile(0, 0)

    @pl.when(i + 1 < n_steps)
    def _():
        start_tile(i + 1, (i + 1) % 2)

    cur = i % 2
    pltpu.make_async_copy(y_hbm.at[pl.ds(0, TOP_K * tc)], buf.at[cur], sem.at[cur]).wait()
    rg = rg_ref[...]
    y = rg[:, 0:1] * buf[cur, 0:tc, :] + rg[:, 1:2] * buf[cur, tc:2 * tc, :]
    xo = x_ref[...] + gt_ref[...] * y
    ms = jnp.mean(xo * xo, axis=-1, keepdims=True)
    o_ref[...] = xo * lax.rsqrt(ms + EPS) * fg_ref[...]


def _combine(slot, x2d, rows_per_batch, gate, route_g, final_g, y, tc):
    m, d = x2d.shape
    tiles_per_batch = rows_per_batch // tc
    grid_spec = pltpu.PrefetchScalarGridSpec(
        num_scalar_prefetch=1,
        grid=(m // tc,),
        in_specs=[pl.BlockSpec((tc, d), lambda i, sl: (i, 0)),
                  pl.BlockSpec((None, 1, d), lambda i, sl: (i // tiles_per_batch, 0, 0)),
                  pl.BlockSpec((tc, LANES), lambda i, sl: (i, 0)),
                  pl.BlockSpec((1, d), lambda i, sl: (0, 0)),
                  pl.BlockSpec(memory_space=pl.ANY)],
        out_specs=pl.BlockSpec((tc, d), lambda i, sl: (i, 0)),
        scratch_shapes=[pltpu.VMEM((2, TOP_K * tc, d), F32), pltpu.SemaphoreType.DMA((2,))],
    )
    return pl.pallas_call(
        functools.partial(_combine_kernel, tc=tc, n_steps=m // tc),
        out_shape=jax.ShapeDtypeStruct((m, d), F32),
        grid_spec=grid_spec,
        compiler_params=_cparams(("arbitrary",)),
        name="combine",
    )(slot, x2d, gate, route_g, final_g.reshape(1, d), y)


def _rope_tables(n_tok, dim):
    t = jnp.arange(n_tok)
    pos = jnp.stack([t // GRID_W, t % GRID_W], axis=-1).astype(F32)
    n_pairs = dim // 4
    freq = ROPE_BASE ** (-jnp.arange(n_pairs, dtype=F32) / n_pairs)
    ang = (pos[:, :, None] * freq).reshape(n_tok, 2 * n_pairs)
    cos = jnp.tile(jnp.cos(ang), (1, HEAD_DIM // dim))
    sin = jnp.tile(jnp.sin(ang), (1, HEAD_DIM // dim))
    return jnp.concatenate([cos, cos], axis=1), jnp.concatenate([-sin, sin], axis=1)


def _deinterleave(w, comps):
    d, n = w.shape
    pairs = HEAD_DIM // (2 * comps)
    return w.reshape(d, n // HEAD_DIM, comps, pairs, 2).transpose(0, 1, 4, 2, 3).reshape(d, n)


def _dedupe(patterns):
    keys, variant = {}, []
    for p in patterns:
        variant.append(keys.setdefault(p.tobytes(), len(keys)))
    first = [variant.index(v) for v in range(len(keys))]
    return np.asarray(variant, np.int32), first


def _window_tables(s):
    tq = BAND_TQ
    nb = s // tq
    band = 3 * tq
    starts = np.clip(np.arange(nb) - 1, 0, nb - 3) * tq
    qpos = np.arange(nb)[:, None] * tq + np.arange(tq)[None, :]
    kpos = starts[:, None] + np.arange(band)[None, :]
    ok = np.abs(kpos[:, None, :] - qpos[:, :, None]) <= C_WINDOW
    variant, first = _dedupe(list(ok))
    bias = np.where(ok[first], 0.0, NEG_INF).astype(np.float32)
    return starts.astype(np.int32), variant, jnp.asarray(bias)


def _na_tables(s, rpb):
    tq = BAND_TQ
    rows = s // GRID_W
    wr = min(NA_ROWS, rows)
    nb = s // tq
    band_blocks = 5
    band = band_blocks * tq
    starts = np.clip(np.arange(nb) - 2, 0, nb - band_blocks) * tq
    n_h = rpb.shape[0]
    n_dc = 2 * NA_COLS - 1
    period = GRID_W + n_dc - 1
    wrap = jnp.zeros((n_h, 2 * NA_ROWS - 1, period), F32)
    wrap = wrap.at[..., :NA_COLS].set(rpb[..., NA_COLS - 1:].astype(F32))
    wrap = wrap.at[..., period - (NA_COLS - 1):].set(rpb[..., :NA_COLS - 1].astype(F32))
    toep = jnp.tile(wrap, (1, 1, GRID_W))[..., :GRID_W * (period - 1)]
    toep = toep.reshape(n_h, 2 * NA_ROWS - 1, GRID_W, period - 1)[..., :GRID_W]
    qc = np.arange(GRID_W)[:, None]
    kc = np.arange(GRID_W)[None, :]
    cstart = np.clip(qc - NA_COLS // 2, 0, GRID_W - NA_COLS)
    col_ok = (kc >= cstart) & (kc < cstart + NA_COLS)
    toep = jnp.where(jnp.asarray(col_ok), toep * LOG2E, NEG_INF)
    masked = jnp.full((n_h, GRID_W, GRID_W), NEG_INF, F32)
    pats = []
    for j in range(nb):
        pat = np.full((tq // GRID_W, band // GRID_W), -1, np.int32)
        for a in range(tq // GRID_W):
            qr = j * (tq // GRID_W) + a
            rs = min(max(qr - wr // 2, 0), rows - wr)
            for w in range(band // GRID_W):
                kr = starts[j] // GRID_W + w
                if rs <= kr < rs + wr:
                    pat[a, w] = kr - qr + NA_ROWS - 1
        pats.append(pat)
    variant, first = _dedupe(pats)
    tiles = []
    for j in first:
        rows_ = [jnp.concatenate([toep[:, dr] if dr >= 0 else masked for dr in pats[j][a]], axis=-1)
                 for a in range(tq // GRID_W)]
        tiles.append(jnp.concatenate(rows_, axis=-2))
    bias = jnp.stack(tiles, axis=1)
    bias = bias.reshape(n_h // 4, 4, len(first), tq, band).transpose(0, 2, 1, 3, 4)
    return starts.astype(np.int32), variant, bias.reshape(n_h // 4, len(first), 4 * tq, band)


def kernel(x, c, ctx, c_ctx, l0_norm1_g, l0_norm2_g, l0_w_ada, l0_b_ada, l0_w_in, l0_w_out, l0_lam_q1, l0_lam_k1, l0_lam_q2, l0_lam_k2, l0_subln_g, l0_q_norm_g, l0_k_norm_g, l0_ffn_w_gate, l0_ffn_w_up, l0_ffn_w_down, l1_norm1_g, l1_norm2_g, l1_w_ada, l1_b_ada, l1_w_in, l1_w_out, l1_sinks, l1_rpb, l1_router, l1_exp_w_gate, l1_exp_w_up, l1_exp_w_down, final_norm_g):
    b, s, d = x.shape
    n_ctx = ctx.shape[1]
    n_exp = l1_router.shape[1]
    n = b * s
    assert d == 2048 and s % ROW_TILE == 0 and n_ctx == 256 and s // GRID_W >= 10

    x2d = x.reshape(n, d)
    xc2d = ctx.reshape(b * n_ctx, d)

    cvecs = jnp.zeros((8, d), F32).at[:b].set(c).at[b].set(c_ctx)
    mod0 = _ada(cvecs, l0_w_ada, l0_b_ada)
    mod1 = _ada(cvecs, l1_w_ada, l1_b_ada)

    def lat(mod, k):
        return mod[:b, k * d:(k + 1) * d].reshape(b, 1, d)

    def cx(mod, k):
        return jnp.broadcast_to(mod[b, k * d:(k + 1) * d].reshape(1, 1, d), (b, 1, d))

    cos_a, sin_a = _rope_tables(s, A_SUB)
    cos_b, sin_b = _rope_tables(s, HEAD_DIM)
    cos_tab = jnp.stack([cos_a, cos_b])
    sin_tab = jnp.stack([sin_a, sin_b])
    tm = ROW_TILE
    hq = 8 * HEAD_DIM

    qa = T_ROPE | T_SCALE_A
    qb = T_ROPE | T_QNORM | T_SCALE_B
    kb = T_ROPE | T_KNORM
    types0 = [qa] * 8 + [qb] * 8 + [T_ROPE] * 8 + [0] * 8 + [kb] * 2 + [0] * 2
    tabs0 = [0] * 8 + [1] * 8 + [0] * 8 + [0] * 8 + [1] * 2 + [0] * 2
    qkg0 = _deinterleave(jnp.stack([l0_q_norm_g, l0_k_norm_g]).astype(F32), 1)
    w0 = l0_w_in.astype(BF16)
    w_in0 = jnp.concatenate([_deinterleave(w0[:, :hq], 2), _deinterleave(w0[:, hq:2 * hq], 1),
                             _deinterleave(w0[:, 2 * hq:3 * hq], 2), w0[:, 3 * hq:4 * hq],
                             _deinterleave(w0[:, 4 * hq:4 * hq + 2 * HEAD_DIM], 1),
                             w0[:, 4 * hq + 2 * HEAD_DIM:]], axis=1)
    p0 = _inproj(x2d, s, l0_norm1_g, lat(mod0, 1), lat(mod0, 0), w_in0, cos_tab, sin_tab, qkg0,
                 types0, tabs0, tm)
    rc = b * n_ctx
    tmc = tm if rc % tm == 0 else n_ctx
    p0c = _inproj(xc2d, rc, l0_norm1_g, cx(mod0, 1), cx(mod0, 0), w_in0, cos_tab, sin_tab, qkg0,
                  [t & ~T_ROPE for t in types0], tabs0, tmc)
    lam = jnp.stack([l0_lam_q1, l0_lam_k1, l0_lam_q2, l0_lam_k2]).astype(F32)
    subg = l0_subln_g.reshape(1, HEAD_DIM).astype(F32)
    tk = min(FLASH_TK, s)
    oa = _flash("diff", p0, p0, p0c, b, FLASH_TQ_DIFF, tk, 0, 16, 24, lam, subg)
    ob = _flash("gqa", p0, p0, p0c, b, FLASH_TQ_GQA, tk, 8, 32, 34)
    oac = _flash("diff", p0c, p0c, None, b, n_ctx, n_ctx, 0, 16, 24, lam, subg)
    obc = _flash("gqa", p0c, p0c, None, b, n_ctx, n_ctx, 8, 32, 34)
    w_out0 = l0_w_out.astype(BF16)
    x1 = _outproj(oa, ob, w_out0, x2d, lat(mod0, 2), s, tm)
    xc1 = _outproj(oac, obc, w_out0, xc2d, cx(mod0, 2), rc, tmc)
    wg0, wu0, wd0 = l0_ffn_w_gate.astype(BF16), l0_ffn_w_up.astype(BF16), l0_ffn_w_down.astype(BF16)
    x2 = _ffn(x1, s, l0_norm2_g, lat(mod0, 4), lat(mod0, 3), lat(mod0, 5), wg0, wu0, wd0, tm)
    xc2 = _ffn(xc1, rc, l0_norm2_g, cx(mod0, 4), cx(mod0, 3), cx(mod0, 5), wg0, wu0, wd0, tmc)

    types1 = [T_ROPE | T_SCALE_B] * 8 + [T_SCALE_B] * 8 + [T_ROPE] * 2 + [0] * 2 + [0] * 16
    tabs1 = [1] * len(types1)
    w1 = l1_w_in.astype(BF16)
    w_in1 = jnp.concatenate([_deinterleave(w1[:, :hq], 1), w1[:, hq:2 * hq],
                             _deinterleave(w1[:, 2 * hq:2 * hq + 2 * HEAD_DIM], 1),
                             w1[:, 2 * hq + 2 * HEAD_DIM:]], axis=1)
    qkg1 = jnp.ones((2, HEAD_DIM), F32)
    p1 = _inproj(x2, s, l1_norm1_g, lat(mod1, 1), lat(mod1, 0), w_in1, cos_tab, sin_tab, qkg1,
                 types1, tabs1, tm)
    l1_q = 16 * HEAD_DIM
    n_kv_heads = (w_in1.shape[1] - l1_q) // HEAD_DIM
    p1c = _inproj(xc2, rc, l1_norm1_g, cx(mod1, 1), cx(mod1, 0), w_in1[:, l1_q:], cos_tab, sin_tab, qkg1,
                  [0] * n_kv_heads, [0] * n_kv_heads, tmc)
    w_st, w_var, w_bias = _window_tables(s)
    n_st, n_var, n_bias = _na_tables(s, l1_rpb)
    sinks = l1_sinks.astype(F32)
    oc = _banded("window", p1, p1c, b, jnp.asarray(w_st), jnp.asarray(w_var), w_bias, sinks, 0, 16, 18, 0, 2)
    od = _banded("na", p1, p1c, b, jnp.asarray(n_st), jnp.asarray(n_var), n_bias, sinks, 8, 20, 28, 4, 12)
    x3 = _outproj(oc, od, l1_w_out.astype(BF16), x2, lat(mod1, 2), s, tm)

    wr_pad = jnp.zeros((d, LANES), F32).at[:, :n_exp].set(l1_router.astype(F32))
    h2, route_i, route_g = _router(x3, s, l1_norm2_g, lat(mod1, 4), lat(mod1, 3), wr_pad, n_exp, tm)
    route_g = route_g[:n]
    e_flat = route_i[:n, :TOP_K].reshape(-1)
    onehot = (e_flat[:, None] == jnp.arange(n_exp, dtype=jnp.int32)[None, :]).astype(jnp.int32)
    csum = jnp.cumsum(onehot, axis=0)
    rank = jnp.sum((csum - onehot) * onehot, axis=1)
    counts = csum[-1]
    tmo = 2 * tm
    padded = (counts + tmo - 1) // tmo * tmo
    pad_end = jnp.cumsum(padded)
    pad_start = pad_end - padded
    slot = (pad_start[e_flat] + rank).astype(jnp.int32)
    n_blocks = (n * TOP_K + n_exp * (tmo - 1) + tmo - 1) // tmo
    cap = n_blocks * tmo
    tok_flat = jnp.repeat(jnp.arange(n, dtype=jnp.int32), TOP_K)
    slot_tok = jnp.full((cap,), n, jnp.int32).at[slot].set(tok_flat, unique_indices=True,
                                                             mode="promise_in_bounds")
    block_row0 = jnp.arange(n_blocks, dtype=jnp.int32) * tmo
    block_expert = jnp.minimum(jnp.searchsorted(pad_end, block_row0, side="right"), n_exp - 1).astype(jnp.int32)
    n_valid = jnp.clip((pad_start + counts)[block_expert] - block_row0, 0, tmo).astype(jnp.int32)

    y = _moe(slot_tok, h2, block_expert, n_valid, l1_exp_w_gate, l1_exp_w_up, l1_exp_w_down, tmo)
    out = _combine(slot, x3, s, lat(mod1, 5), route_g, final_norm_g, y, COMBINE_TILE)
    return out.reshape(b, s, d)
```
